```python
import jax, jax.numpy as jnp
from jax import lax
import numpy as np

D_MODEL = 1024
BATCH = 8
SEQ = 2048
DEPTH = 2
DEC_BATCH = 32
DEC_SEQ = 4
PAST_LEN = 8192
PAGE_SIZE = 128

N_HEADS = 8
HEAD_DIM = 64
D_ATT = N_HEADS * HEAD_DIM
D_CONV = D_MODEL - D_ATT
CONV_W = 31
POOL_WINDOWS = (2, 4, 8, 16)
POOL_GROUP = D_MODEL // len(POOL_WINDOWS)
POOL_BUF = max(POOL_WINDOWS) - 1
D_FF = 2816
Q_BLOCK = 128
D_IN = 3 * D_ATT + N_HEADS + 2 * D_CONV
N_EVEN = (DEPTH + 1) // 2
N_ODD = DEPTH // 2
RMS_EPS = 1e-6
LN_EPS = 1e-5
NEG_INF = -1e30
FGATE_BIAS_INIT = 7.0

kernel_name = "hybrid_conv_fox_pool_macaron_step"


def rmsnorm(x, g):
    xf = x.astype(jnp.float32)
    y = xf * lax.rsqrt(jnp.mean(xf * xf, axis=-1, keepdims=True) + RMS_EPS)
    return (y * g.astype(jnp.float32)).astype(x.dtype)


def swiglu(h, wg, wu, wd):
    return (jax.nn.silu(h @ wg) * (h @ wu)) @ wd


def fox_attend(q, cq, qpos, k, ck, kpos, v):
    s = jnp.einsum('bqhd,bkhd->bhqk', q, k).astype(jnp.float32) * (HEAD_DIM ** -0.5)
    bias = jnp.swapaxes(cq, 1, 2)[..., :, None] - jnp.swapaxes(ck, 1, 2)[..., None, :]
    mask = kpos[None, :] <= qpos[:, None]
    p = jax.nn.softmax(jnp.where(mask, s + bias, NEG_INF), axis=-1)
    return jnp.einsum('bhqk,bkhd->bqhd', p.astype(v.dtype), v)


def fox_prompt(q, k, v, logf):
    B, S, H, Dh = q.shape
    c = jnp.cumsum(logf, axis=1)
    nb = S // Q_BLOCK
    qb = jnp.swapaxes(q.reshape(B, nb, Q_BLOCK, H, Dh), 0, 1)
    cb = jnp.swapaxes(c.reshape(B, nb, Q_BLOCK, H), 0, 1)
    kpos = jnp.arange(S)

    def one_block(args):
        qi, ci, i = args
        qpos = i * Q_BLOCK + jnp.arange(Q_BLOCK)
        return fox_attend(qi, ci, qpos, k, c, kpos, v)

    o = lax.map(one_block, (qb, cb, jnp.arange(nb)))
    return jnp.swapaxes(o, 0, 1).reshape(B, S, H * Dh)


def fox_sample(q, k, v, logf, k_past, v_past, logf_past):
    B, T, H, Dh = q.shape
    P = k_past.shape[1]
    cn = jnp.cumsum(logf, axis=1)
    suffix = lax.cumsum(logf_past, axis=1, reverse=True) - logf_past
    ck = jnp.concatenate([-suffix, cn], axis=1)
    k_all = jnp.concatenate([k_past, k], axis=1)
    v_all = jnp.concatenate([v_past, v], axis=1)
    qpos = P + jnp.arange(T)
    kpos = jnp.arange(P + T)
    o = fox_attend(q, cn, qpos, k_all, ck, kpos, v_all)
    return o.reshape(B, T, H * Dh)


def even_project(h, w_in, b_f):
    B, T, _ = h.shape
    z = h @ w_in
    q, k, v, fg, a, g = jnp.split(
        z, [D_ATT, 2 * D_ATT, 3 * D_ATT, 3 * D_ATT + N_HEADS, 3 * D_ATT + N_HEADS + D_CONV], axis=-1)
    heads = lambda t: t.reshape(B, T, N_HEADS, HEAD_DIM)
    logf = jax.nn.log_sigmoid((fg + b_f).astype(jnp.float32))
    u = a * jax.nn.sigmoid(g)
    return heads(q), heads(k), heads(v), logf, u


def conv_tail(buf, dw_w, dw_b, ln_g, ln_b):
    y = lax.conv_general_dilated(buf, dw_w[:, None, :].astype(buf.dtype), (1,), 'VALID',
                                 dimension_numbers=('NWC', 'WIO', 'NWC'),
                                 feature_group_count=D_CONV) + dw_b
    yf = y.astype(jnp.float32)
    mu = jnp.mean(yf, axis=-1, keepdims=True)
    var = jnp.mean((yf - mu) ** 2, axis=-1, keepdims=True)
    yn = (yf - mu) * lax.rsqrt(var + LN_EPS) * ln_g.astype(jnp.float32) + ln_b.astype(jnp.float32)
    return jax.nn.silu(yn).astype(buf.dtype)


def multiscale_pool_mix(buf, pos0, w_groups, scale):
    B, L, D = buf.shape
    T = L - POOL_BUF
    bf = buf.astype(jnp.float32)
    cs = jnp.concatenate([jnp.zeros((B, 1, D), jnp.float32), jnp.cumsum(bf, axis=1)], axis=1)
    hi = cs[:, POOL_BUF + 1:]
    x_cur = bf[:, POOL_BUF:]
    pos = pos0 + jnp.arange(T)
    outs = []
    for gi, w in enumerate(POOL_WINDOWS):
        sl = slice(gi * POOL_GROUP, (gi + 1) * POOL_GROUP)
        lo = cs[:, POOL_BUF + 1 - w: POOL_BUF + 1 - w + T, sl]
        cnt = jnp.minimum(pos + 1, w).astype(jnp.float32)[None, :, None]
        mixed = (hi[..., sl] - lo) / cnt - x_cur[..., sl]
        outs.append(mixed.astype(buf.dtype) @ w_groups[gi])
    return jnp.concatenate(outs, axis=-1) * scale


def setup_inputs(seed: int = 0) -> dict:
    key = jax.random.key(seed)
    ks = jax.random.split(key, 24)
    f32 = jnp.float32
    n_pages = PAST_LEN // PAGE_SIZE
    n_used = DEC_BATCH * n_pages
    n_pool = n_used + n_used // 4
    nrm = lambda k, shp, s=1.0: jax.random.normal(k, shp, f32) * s
    page_table = jax.random.permutation(ks[0], n_pool)[:n_used].reshape(DEC_BATCH, n_pages).astype(jnp.int32)
    return {
        "x_prompt": nrm(ks[1], (BATCH, SEQ, D_MODEL)),
        "x_sample": nrm(ks[2], (DEC_BATCH, DEC_SEQ, D_MODEL)),
        "cache_k": nrm(ks[3], (N_EVEN, n_pool, PAGE_SIZE, N_HEADS, HEAD_DIM)),
        "cache_v": nrm(ks[4], (N_EVEN, n_pool, PAGE_SIZE, N_HEADS, HEAD_DIM)),
        "cache_logf": jax.nn.log_sigmoid(nrm(ks[5], (N_EVEN, n_pool, PAGE_SIZE, N_HEADS)) + FGATE_BIAS_INIT),
        "state_conv": nrm(ks[6], (N_EVEN, DEC_BATCH, CONV_W - 1, D_CONV), 0.5),
        "state_pool": nrm(ks[7], (N_ODD, DEC_BATCH, POOL_BUF, D_MODEL)),
        "page_table": page_table,
        "norm_g": 1.0 + nrm(ks[8], (DEPTH, 6, D_MODEL), 0.05),
        "ffn_w_gate": nrm(ks[9], (DEPTH, 2, D_MODEL, D_FF), D_MODEL ** -0.5),
        "ffn_w_up": nrm(ks[10], (DEPTH, 2, D_MODEL, D_FF), D_MODEL ** -0.5),
        "ffn_w_down": nrm(ks[11], (DEPTH, 2, D_FF, D_MODEL), D_FF ** -0.5),
        "mix_w_in": nrm(ks[12], (N_EVEN, D_MODEL, D_IN), D_MODEL ** -0.5),
        "fgate_b": FGATE_BIAS_INIT + nrm(ks[13], (N_EVEN, N_HEADS), 0.1),
        "conv_dw_w": nrm(ks[14], (N_EVEN, CONV_W, D_CONV), CONV_W ** -0.5),
        "conv_dw_b": nrm(ks[15], (N_EVEN, D_CONV), 0.02),
        "conv_ln_g": 1.0 + nrm(ks[16], (N_EVEN, D_CONV), 0.05),
        "conv_ln_b": nrm(ks[17], (N_EVEN, D_CONV), 0.02),
        "mix_w_out": nrm(ks[18], (N_EVEN, D_ATT + D_CONV, D_MODEL), (D_ATT + D_CONV) ** -0.5),
        "pool_w": nrm(ks[19], (N_ODD, len(POOL_WINDOWS), POOL_GROUP, POOL_GROUP), POOL_GROUP ** -0.5),
        "pool_scale": 1.0 + nrm(ks[20], (N_ODD, D_MODEL), 0.1),
    }


def reference(x_prompt, x_sample, cache_k, cache_v, cache_logf, state_conv, state_pool, page_table,
              norm_g, ffn_w_gate, ffn_w_up, ffn_w_down, mix_w_in, fgate_b, conv_dw_w, conv_dw_b,
              conv_ln_g, conv_ln_b, mix_w_out, pool_w, pool_scale):
    B = x_prompt.shape[0]
    Bd = x_sample.shape[0]
    P = page_table.shape[1] * cache_k.shape[2]
    xp, xs = x_prompt, x_sample
    kp_l, vp_l, lp_l, cp_l, pp_l = [], [], [], [], []
    ks_l, vs_l, ls_l, cs_l, ps_l = [], [], [], [], []
    for layer in range(DEPTH):
        g = norm_g[layer]
        w1 = (ffn_w_gate[layer, 0], ffn_w_up[layer, 0], ffn_w_down[layer, 0])
        xp = xp + 0.5 * rmsnorm(swiglu(rmsnorm(xp, g[0]), *w1), g[1])
        xs = xs + 0.5 * rmsnorm(swiglu(rmsnorm(xs, g[0]), *w1), g[1])
        if layer % 2 == 0:
            e = layer // 2
            conv_p = (conv_dw_w[e], conv_dw_b[e], conv_ln_g[e], conv_ln_b[e])
            hp = rmsnorm(xp, g[2])
            q, k, v, lf, u = even_project(hp, mix_w_in[e], fgate_b[e])
            att = fox_prompt(q, k, v, lf)
            cbuf = jnp.concatenate([jnp.zeros((B, CONV_W - 1, D_CONV), u.dtype), u], axis=1)
            cv = conv_tail(cbuf, *conv_p)
            mp = jnp.concatenate([att, cv], axis=-1) @ mix_w_out[e]
            xp = xp + rmsnorm(mp, g[3])
            kp_l.append(k); vp_l.append(v); lp_l.append(lf); cp_l.append(u[:, -(CONV_W - 1):])
            hs = rmsnorm(xs, g[2])
            q, k, v, lf, u = even_project(hs, mix_w_in[e], fgate_b[e])
            k_past = cache_k[e][page_table].reshape(Bd, P, N_HEADS, HEAD_DIM)
            v_past = cache_v[e][page_table].reshape(Bd, P, N_HEADS, HEAD_DIM)
            lf_past = cache_logf[e][page_table].reshape(Bd, P, N_HEADS).astype(jnp.float32)
            att = fox_sample(q, k.astype(k_past.dtype), v.astype(v_past.dtype), lf, k_past, v_past, lf_past)
            cbuf = jnp.concatenate([state_conv[e].astype(u.dtype), u], axis=1)
            cv = conv_tail(cbuf, *conv_p)
            ms = jnp.concatenate([att.astype(cv.dtype), cv], axis=-1) @ mix_w_out[e]
            xs = xs + rmsnorm(ms, g[3])
            ks_l.append(k); vs_l.append(v); ls_l.append(lf); cs_l.append(cbuf[:, -(CONV_W - 1):])
        else:
            o = layer // 2
            hp = rmsnorm(xp, g[2])
            pbuf = jnp.concatenate([jnp.zeros((B, POOL_BUF, D_MODEL), hp.dtype), hp], axis=1)
            mp = multiscale_pool_mix(pbuf, 0, pool_w[o], pool_scale[o])
            xp = xp + rmsnorm(mp, g[3])
            pp_l.append(hp[:, -POOL_BUF:])
            hs = rmsnorm(xs, g[2])
            sbuf = jnp.concatenate([state_pool[o].astype(hs.dtype), hs], axis=1)
            ms = multiscale_pool_mix(sbuf, P, pool_w[o], pool_scale[o])
            xs = xs + rmsnorm(ms, g[3])
            ps_l.append(sbuf[:, -POOL_BUF:])
        w2 = (ffn_w_gate[layer, 1], ffn_w_up[layer, 1], ffn_w_down[layer, 1])
        xp = xp + 0.5 * rmsnorm(swiglu(rmsnorm(xp, g[4]), *w2), g[5])
        xs = xs + 0.5 * rmsnorm(swiglu(rmsnorm(xs, g[4]), *w2), g[5])
    return (xp, xs,
            jnp.stack(kp_l), jnp.stack(vp_l), jnp.stack(lp_l), jnp.stack(cp_l), jnp.stack(pp_l),
            jnp.stack(ks_l), jnp.stack(vs_l), jnp.stack(ls_l), jnp.stack(cs_l), jnp.stack(ps_l))
```

```python
import functools

import jax
import jax.numpy as jnp
from jax import lax
from jax.experimental import pallas as pl
from jax.experimental.pallas import tpu as pltpu

F32 = jnp.float32
BF16 = jnp.bfloat16

RMS_EPS = 1e-6
LN_EPS = 1e-5
NEG_INF = -1e30

LANES = 128
HEAD_DIM = 64
N_HEADS = 8
CONV_W = 31
CONV_HALO = 32
POOL_WINDOWS = (2, 4, 8, 16)
POOL_HALO = 32
POOL_TAIL = 16
VMEM_LIMIT_BYTES = 56 * 1024 * 1024


def _params(n_axes):
    return pltpu.CompilerParams(dimension_semantics=("arbitrary",) * n_axes,
                                vmem_limit_bytes=VMEM_LIMIT_BYTES)


def _const_spec(shape, single_buffer=False):
    zeros = (0,) * len(shape)
    if single_buffer:
        return pl.BlockSpec(shape, lambda *_: zeros, pipeline_mode=pl.Buffered(1))
    return pl.BlockSpec(shape, lambda *_: zeros)


def _rms(x, g):
    ms = jnp.mean(x * x, axis=-1, keepdims=True)
    return x * lax.rsqrt(ms + RMS_EPS) * g


def _sigmoid(x):
    return 1.0 / (1.0 + jnp.exp(-x))


def _log_sigmoid(x):
    return jnp.minimum(x, 0.0) - jnp.log1p(jnp.exp(-jnp.abs(x)))


def _split3(x):
    hi = x.astype(BF16).astype(F32)
    r = x - hi
    mid = r.astype(BF16).astype(F32)
    return hi, mid, r - mid


def _dot3(x, w):
    hi, mid, lo = _split3(x)
    d = lambda a: jnp.dot(a, w, preferred_element_type=F32)
    return d(hi) + d(mid) + d(lo)


def _ffn_kernel(x_ref, gpre_ref, gpost_ref, wg_ref, wu_ref, wd_ref, o_ref, *, n_chunks):
    x = x_ref[...]
    h = _rms(x, gpre_ref[...]).astype(BF16)
    ck = wg_ref.shape[1] // n_chunks
    acc = None
    for c in range(n_chunks):
        sl = slice(c * ck, (c + 1) * ck)
        g = jnp.dot(h, wg_ref[:, sl], preferred_element_type=F32)
        u = jnp.dot(h, wu_ref[:, sl], preferred_element_type=F32)
        a = (g * _sigmoid(g) * u).astype(BF16)
        d = jnp.dot(a, wd_ref[sl, :], preferred_element_type=F32)
        acc = d if acc is None else acc + d
    o_ref[...] = x + 0.5 * _rms(acc, gpost_ref[...])


def _ffn(x, g_pre, g_post, wg, wu, wd, tm):
    m, d = x.shape
    tile = pl.BlockSpec((tm, d), lambda i: (i, 0))
    return pl.pallas_call(
        functools.partial(_ffn_kernel, n_chunks=2),
        grid=(m // tm,),
        in_specs=[tile, _const_spec((1, d)), _const_spec((1, d)),
                  _const_spec(wg.shape, True), _const_spec(wu.shape, True), _const_spec(wd.shape, True)],
        out_specs=tile,
        out_shape=jax.ShapeDtypeStruct((m, d), F32),
        compiler_params=_params(1),
        name="ffn",
    )(x, g_pre, g_post, wg, wu, wd)


def _inproj_kernel(x_ref, g_ref, w_ref, bf_ref, tri_ref,
                   q_ref, k_ref, v_ref, kb_ref, vb_ref, lf_ref, u_ref, ccol_ref, crow_ref,
                   carry_ref, *, d_att, d_conv):
    tm = x_ref.shape[1]

    @pl.when(pl.program_id(1) == 0)
    def _():
        carry_ref[...] = jnp.zeros_like(carry_ref)

    h = _rms(x_ref[0], g_ref[...]).astype(BF16)
    z = jnp.dot(h, w_ref[...], preferred_element_type=F32)
    k = z[:, d_att:2 * d_att]
    v = z[:, 2 * d_att:3 * d_att]
    o = 3 * d_att
    a = z[:, o:o + d_conv]
    gate = z[:, o + d_conv:o + 2 * d_conv]
    fg = z[:, o + 2 * d_conv:o + 2 * d_conv + LANES]
    q_ref[0] = (z[:, :d_att] * (HEAD_DIM ** -0.5)).astype(BF16)
    k_ref[0] = k
    v_ref[0] = v
    kb_ref[0] = k.astype(BF16)
    vb_ref[0] = v.astype(BF16)
    u_ref[0] = a * _sigmoid(gate)
    lf = _log_sigmoid(fg + bf_ref[...])
    lf_ref[0] = lf[:, :N_HEADS]

    lft = lf.T[:N_HEADS]
    carry = carry_ref[...]
    tri = tri_ref[...]
    chunks = []
    for c in range(tm // LANES):
        cs = _dot3(lft[:, c * LANES:(c + 1) * LANES], tri) + carry
        crow_ref[0, c] = cs
        chunks.append(cs)
        carry = jnp.broadcast_to(cs[:, LANES - 1:LANES], cs.shape)
    carry_ref[...] = carry
    crow = jnp.concatenate(chunks, axis=1)
    cfull = jnp.concatenate([crow, jnp.zeros((LANES - N_HEADS, tm), F32)], axis=0)
    ccol_ref[0] = cfull.T[:, :N_HEADS]


def _inproj(x, g, w, bf, tri, tm, d_att, d_conv):
    b, s, d = x.shape
    nt = s // tm
    tok = lambda width: pl.BlockSpec((1, tm, width), lambda i, j: (i, j, 0))
    shp = lambda width, dt: jax.ShapeDtypeStruct((b, s, width), dt)
    return pl.pallas_call(
        functools.partial(_inproj_kernel, d_att=d_att, d_conv=d_conv),
        grid=(b, nt),
        in_specs=[tok(d), _const_spec((1, d)), _const_spec(w.shape, True), _const_spec((1, LANES)),
                  _const_spec((LANES, LANES))],
        out_specs=[tok(d_att), tok(d_att), tok(d_att), tok(d_att), tok(d_att), tok(N_HEADS), tok(d_conv),
                   tok(N_HEADS),
                   pl.BlockSpec((1, tm // LANES, N_HEADS, LANES), lambda i, j: (i, j, 0, 0))],
        out_shape=[shp(d_att, BF16), shp(d_att, F32), shp(d_att, F32), shp(d_att, BF16), shp(d_att, BF16),
                   shp(N_HEADS, F32), shp(d_conv, F32), shp(N_HEADS, F32),
                   jax.ShapeDtypeStruct((b, s // LANES, N_HEADS, LANES), F32)],
        scratch_shapes=[pltpu.VMEM((N_HEADS, LANES), F32)],
        compiler_params=_params(2),
        name="inproj",
    )(x, g, w, bf, tri)


def _attn_kernel(q_ref, k_ref, v_ref, ccol_ref, crow_ref, o_ref, *, tq):
    i = pl.program_id(1)
    nl = tq // LANES
    row = lax.broadcasted_iota(jnp.int32, (tq, tq), 0)
    col = lax.broadcasted_iota(jnp.int32, (tq, tq), 1)
    causal = col <= row
    low = lax.broadcasted_iota(jnp.int32, (tq, LANES), 1) < HEAD_DIM
    ccol = ccol_ref[0]
    nt_dims = (((1,), (1,)), ((), ()))

    for p in range(N_HEADS // 2):
        ls = slice(LANES * p, LANES * (p + 1))
        q2 = q_ref[0, :, ls]
        zero = jnp.zeros_like(q2)
        q_heads = (jnp.where(low, q2, zero), jnp.where(low, zero, q2))
        cq_heads = (ccol[:, 2 * p:2 * p + 1], ccol[:, 2 * p + 1:2 * p + 2])

        def step(jb, carry, masked, ls=ls, q_heads=q_heads, cq_heads=cq_heads, p=p):
            j0 = pl.multiple_of(jb * tq, tq)
            k2 = k_ref[0, pl.ds(j0, tq), ls]
            v2 = v_ref[0, pl.ds(j0, tq), ls]
            new = []
            for hh in range(2):
                m, l = carry[2 * hh], carry[2 * hh + 1]
                ck = jnp.concatenate(
                    [crow_ref[0, jb * nl + c, 2 * p + hh:2 * p + hh + 1, :] for c in range(nl)], axis=1)
                s = lax.dot_general(q_heads[hh], k2, nt_dims, preferred_element_type=F32)
                s = s + (cq_heads[hh] - ck)
                if masked:
                    s = jnp.where(causal, s, NEG_INF)
                m_new = jnp.maximum(m, jnp.max(s, axis=-1, keepdims=True))
                alpha = jnp.exp(m - m_new)
                pr = jnp.exp(s - m_new)
                l_new = alpha * l + jnp.sum(pr, axis=-1, keepdims=True)
                pv = jnp.dot(pr.astype(BF16), v2, preferred_element_type=F32)
                new.append((m_new, l_new, alpha, pv))
            (ma, la, aa, pva), (mb, lb, ab, pvb) = new
            acc = jnp.where(low, aa, ab) * carry[4] + jnp.where(low, pva, pvb)
            return ma, la, mb, lb, acc

        neg = jnp.full((tq, 1), NEG_INF, F32)
        zcol = jnp.zeros((tq, 1), F32)
        init = (neg, zcol, neg, zcol, jnp.zeros((tq, LANES), F32))
        carry = lax.fori_loop(0, i, lambda jb, c: step(jb, c, False), init)
        ma, la, mb, lb, acc = step(i, carry, True)
        o_ref[0, :, ls] = (acc * jnp.where(low, 1.0 / la, 1.0 / lb)).astype(BF16)


def _attn(q, kb, vb, ccol, crow, tq):
    b, s, d_att = q.shape
    qtile = pl.BlockSpec((1, tq, d_att), lambda i, j: (i, j, 0))
    full = pl.BlockSpec((1, s, d_att), lambda i, j: (i, 0, 0))
    return pl.pallas_call(
        functools.partial(_attn_kernel, tq=tq),
        grid=(b, s // tq),
        in_specs=[qtile, full, full,
                  pl.BlockSpec((1, tq, N_HEADS), lambda i, j: (i, j, 0)),
                  pl.BlockSpec((1, s // LANES, N_HEADS, LANES), lambda i, j: (i, 0, 0, 0))],
        out_specs=qtile,
        out_shape=jax.ShapeDtypeStruct((b, s, d_att), BF16),
        compiler_params=_params(2),
        name="attn",
    )(q, kb, vb, ccol, crow)


def _decode_kernel(pt_ref, q_ref, kn_ref, vn_ref, cqcol_ref, cqrow_ref, us_ref, *rest, n_group):
    k_refs = rest[:n_group]
    v_refs = rest[n_group:2 * n_group]
    lf_refs = rest[2 * n_group:3 * n_group]
    o_ref = rest[3 * n_group]
    qbd_ref, m_ref, l_ref, acc_ref, carry_ref = rest[3 * n_group + 1:]
    js = pl.program_id(1)
    t_new = q_ref.shape[1]
    d_att = q_ref.shape[2]
    n_rows = t_new * N_HEADS
    nt_dims = (((1,), (1,)), ((), ()))
    cqcol = cqcol_ref[0]

    @pl.when(js == 0)
    def _():
        q = q_ref[0].astype(F32)
        sub = lax.broadcasted_iota(jnp.int32, (N_HEADS, d_att), 0)
        lane = lax.broadcasted_iota(jnp.int32, (N_HEADS, d_att), 1)
        own = (lane // HEAD_DIM) == sub
        rows = [jnp.where(own, jnp.broadcast_to(q[t:t + 1, :], (N_HEADS, d_att)), 0.0) for t in range(t_new)]
        qbd = jnp.concatenate(rows, axis=0).astype(BF16)
        qbd_ref[...] = qbd
        s = lax.dot_general(qbd, kn_ref[0].astype(BF16), nt_dims, preferred_element_type=F32)
        cqk = jnp.concatenate([cqrow_ref[0]] * t_new, axis=0)
        r = lax.broadcasted_iota(jnp.int32, s.shape, 0)
        c = lax.broadcasted_iota(jnp.int32, s.shape, 1)
        s = jnp.where(c <= r // N_HEADS, s + (cqcol - cqk), NEG_INF)
        m = jnp.max(s, axis=-1, keepdims=True)
        pr = jnp.exp(s - m)
        m_ref[...] = m
        l_ref[...] = jnp.sum(pr, axis=-1, keepdims=True)
        acc_ref[...] = jnp.dot(pr.astype(BF16), vn_ref[0].astype(BF16), preferred_element_type=F32)
        carry_ref[...] = jnp.zeros_like(carry_ref)

    qbd = qbd_ref[...]
    carry = carry_ref[...]
    us = us_ref[...]
    s_parts = []
    for g in range(n_group):
        lf = lf_refs[g][0]
        page = lf.shape[0]
        lfp = jnp.concatenate([lf, jnp.zeros((page, LANES - N_HEADS), F32)], axis=1)
        lft = lfp.T[:N_HEADS]
        r = _dot3(lft, us)
        suffix = r[:, :page] + carry
        carry = carry + r[:, page:]
        bias = jnp.concatenate([suffix] * t_new, axis=0) + cqcol
        kp = k_refs[g][0].astype(BF16)
        s_parts.append(lax.dot_general(qbd, kp, nt_dims, preferred_element_type=F32) + bias)
    carry_ref[...] = carry
    s = jnp.concatenate(s_parts, axis=1)
    m = m_ref[...]
    m_new = jnp.maximum(m, jnp.max(s, axis=-1, keepdims=True))
    alpha = jnp.exp(m - m_new)
    pr = jnp.exp(s - m_new)
    m_ref[...] = m_new
    l_ref[...] = alpha * l_ref[...] + jnp.sum(pr, axis=-1, keepdims=True)
    prb = pr.astype(BF16)
    acc = alpha * acc_ref[...]
    for g in range(n_group):
        page = v_refs[g].shape[1]
        acc = acc + jnp.dot(prb[:, g * page:(g + 1) * page], v_refs[g][0].astype(BF16),
                            preferred_element_type=F32)
    acc_ref[...] = acc

    @pl.when(js == pl.num_programs(1) - 1)
    def _():
        o = acc * (1.0 / l_ref[...])
        sub = lax.broadcasted_iota(jnp.int32, (N_HEADS, d_att), 0)
        lane = lax.broadcasted_iota(jnp.int32, (N_HEADS, d_att), 1)
        own = (lane // HEAD_DIM) == sub
        rows = [jnp.sum(jnp.where(own, o[t * N_HEADS:(t + 1) * N_HEADS], 0.0), axis=0, keepdims=True)
                for t in range(t_new)]
        o_ref[0] = jnp.concatenate(rows, axis=0).astype(o_ref.dtype)


def _decode(page_table, q, kn, vn, cqcol, cqrow, us, cache_k, cache_v, cache_lf, n_group):
    bd, t_new, d_att = q.shape
    n_pages = page_table.shape[1]
    page = cache_k.shape[1]
    n_rows = t_new * N_HEADS
    per_b = lambda shape: pl.BlockSpec((1,) + shape, lambda i, j, pt: (i,) + (0,) * len(shape))

    def paged(shape, g):
        return pl.BlockSpec((1,) + shape,
                            lambda i, j, pt: (pt[i, n_pages - 1 - (j * n_group + g)],) + (0,) * len(shape))

    in_specs = [per_b((t_new, d_att)), per_b(kn.shape[1:]), per_b(vn.shape[1:]), per_b((n_rows, 1)),
                per_b(cqrow.shape[1:]), pl.BlockSpec(us.shape, lambda i, j, pt: (0, 0))]
    in_specs += [paged((page, d_att), g) for g in range(n_group)]
    in_specs += [paged((page, d_att), g) for g in range(n_group)]
    in_specs += [paged((page, N_HEADS), g) for g in range(n_group)]
    grid_spec = pltpu.PrefetchScalarGridSpec(
        num_scalar_prefetch=1,
        grid=(bd, n_pages // n_group),
        in_specs=in_specs,
        out_specs=per_b((t_new, d_att)),
        scratch_shapes=[pltpu.VMEM((n_rows, d_att), BF16), pltpu.VMEM((n_rows, 1), F32),
                        pltpu.VMEM((n_rows, 1), F32), pltpu.VMEM((n_rows, d_att), F32),
                        pltpu.VMEM((N_HEADS, page), F32)],
    )
    return pl.pallas_call(
        functools.partial(_decode_kernel, n_group=n_group),
        grid_spec=grid_spec,
        out_shape=jax.ShapeDtypeStruct((bd, t_new, d_att), BF16),
        compiler_params=_params(2),
        name="decode",
    )(page_table, q, kn, vn, cqcol, cqrow, us,
      *([cache_k] * n_group), *([cache_v] * n_group), *([cache_lf] * n_group))


def _ln_swish(y, g, b):
    mu = jnp.mean(y, axis=-1, keepdims=True)
    d = y - mu
    var = jnp.mean(d * d, axis=-1, keepdims=True)
    yn = d * lax.rsqrt(var + LN_EPS) * g + b
    return yn * _sigmoid(yn)


def _conv_kernel(u_ref, halo_ref, w_ref, b_ref, lg_ref, lb_ref, o_ref, buf_ref, y_ref, *, rows, halo_is_state):
    tm = u_ref.shape[1]
    d_conv = u_ref.shape[2]
    halo = halo_ref[0]
    if not halo_is_state:
        halo = jnp.where(pl.program_id(1) == 0, 0.0, halo)
    buf_ref[0:CONV_HALO, :] = halo
    buf_ref[CONV_HALO:, :] = u_ref[0]
    base = CONV_HALO - (CONV_W - 1)
    for cb in range(d_conv // LANES):
        ls = slice(cb * LANES, (cb + 1) * LANES)
        for rb in range(tm // rows):
            acc = jnp.zeros((rows, LANES), F32)
            for tap in range(CONV_W):
                r0 = rb * rows + base + tap
                acc = acc + w_ref[tap:tap + 1, ls] * buf_ref[r0:r0 + rows, ls]
            y_ref[rb * rows:(rb + 1) * rows, ls] = acc + b_ref[:, ls]
    o_ref[0] = _ln_swish(y_ref[...], lg_ref[...], lb_ref[...]).astype(o_ref.dtype)


def _conv(u, state, w, b, lg, lb, tm):
    bsz, s, d_conv = u.shape
    per_halo = tm // CONV_HALO
    tile = pl.BlockSpec((1, tm, d_conv), lambda i, j: (i, j, 0))
    if state is None:
        halo = pl.BlockSpec((1, CONV_HALO, d_conv), lambda i, j: (i, jnp.maximum(j * per_halo - 1, 0), 0))
    else:
        assert s == tm and state.shape == (bsz, CONV_HALO, d_conv)
        halo = pl.BlockSpec((1, CONV_HALO, d_conv), lambda i, j: (i, 0, 0))
    vec = _const_spec((1, d_conv))
    return pl.pallas_call(
        functools.partial(_conv_kernel, rows=min(tm, 128), halo_is_state=state is not None),
        grid=(bsz, s // tm),
        in_specs=[tile, halo, _const_spec(w.shape), vec, vec, vec],
        out_specs=tile,
        out_shape=jax.ShapeDtypeStruct((bsz, s, d_conv), F32 if tm < 16 else BF16),
        scratch_shapes=[pltpu.VMEM((tm + CONV_HALO, d_conv), F32), pltpu.VMEM((tm, d_conv), F32)],
        compiler_params=_params(2),
        name="conv",
    )(u, u if state is None else state, w, b, lg, lb)


def _outproj_kernel(x_ref, att_ref, cv_ref, wa_ref, wc_ref, g_ref, o_ref):
    m = jnp.dot(att_ref[...].astype(BF16), wa_ref[...], preferred_element_type=F32)
    m = m + jnp.dot(cv_ref[...].astype(BF16), wc_ref[...], preferred_element_type=F32)
    o_ref[...] = x_ref[...] + _rms(m, g_ref[...])


def _outproj(x, att, cv, wa, wc, g, tm):
    m, d = x.shape
    tile = lambda width: pl.BlockSpec((tm, width), lambda i: (i, 0))
    return pl.pallas_call(
        _outproj_kernel,
        grid=(m // tm,),
        in_specs=[tile(d), tile(att.shape[1]), tile(cv.shape[1]), _const_spec(wa.shape), _const_spec(wc.shape),
                  _const_spec((1, d))],
        out_specs=tile(d),
        out_shape=jax.ShapeDtypeStruct((m, d), F32),
        compiler_params=_params(1),
        name="outproj",
    )(x, att, cv, wa, wc, g)


def _pool_kernel(x_ref, halo_ref, g2_ref, g3_ref, w_ref, sc_ref, o_ref, tail_ref, buf_ref, ta_ref, tb_ref,
                 *, pos0, halo_is_state):
    tm = x_ref.shape[1]
    d = x_ref.shape[2]
    grp = d // len(POOL_WINDOWS)
    n = tm + POOL_HALO
    j = pl.program_id(1)
    x = x_ref[0]
    h = _rms(x, g2_ref[...])
    if halo_is_state:
        halo = halo_ref[0]
    else:
        halo = jnp.where(j == 0, 0.0, _rms(halo_ref[0], g2_ref[...]))
    buf_ref[0:POOL_HALO, :] = halo
    buf_ref[POOL_HALO:, :] = h

    @pl.when(j == pl.num_programs(1) - 1)
    def _():
        tail_ref[0] = h[tm - tail_ref.shape[1]:, :]

    ta_ref[8:n, :] = buf_ref[8:n, :] + buf_ref[7:n - 1, :]
    tb_ref[16:n, grp:] = ta_ref[16:n, grp:] + ta_ref[14:n - 2, grp:]
    ta_ref[24:n, 2 * grp:] = tb_ref[24:n, 2 * grp:] + tb_ref[20:n - 4, 2 * grp:]
    s16 = ta_ref[32:n, 3 * grp:] + ta_ref[24:n - 8, 3 * grp:]
    totals = (ta_ref[POOL_HALO:n, 0:grp], tb_ref[POOL_HALO:n, grp:2 * grp],
              ta_ref[POOL_HALO:n, 2 * grp:3 * grp], s16)

    pos = pos0 + j * tm + lax.broadcasted_iota(jnp.int32, (tm, 1), 0)
    outs = []
    for gi, win in enumerate(POOL_WINDOWS):
        cnt = jnp.minimum(pos + 1, win).astype(F32)
        mixed = totals[gi] / cnt - h[:, gi * grp:(gi + 1) * grp]
        outs.append(jnp.dot(mixed.astype(BF16), w_ref[gi], preferred_element_type=F32))
    mix = jnp.concatenate(outs, axis=-1) * sc_ref[...]
    o_ref[0] = x + _rms(mix, g3_ref[...])


def _pool(x, state, g2, g3, w, scale, tm, pos0):
    b, s, d = x.shape
    per_halo = tm // POOL_HALO
    n_tail = min(tm, POOL_TAIL)
    tile = pl.BlockSpec((1, tm, d), lambda i, j: (i, j, 0))
    if state is None:
        halo = pl.BlockSpec((1, POOL_HALO, d), lambda i, j: (i, jnp.maximum(j * per_halo - 1, 0), 0))
    else:
        assert s == tm and state.shape == (b, POOL_HALO, d)
        halo = pl.BlockSpec((1, POOL_HALO, d), lambda i, j: (i, 0, 0))
    vec = _const_spec((1, d))
    buf = pltpu.VMEM((tm + POOL_HALO, d), F32)
    return pl.pallas_call(
        functools.partial(_pool_kernel, pos0=pos0, halo_is_state=state is not None),
        grid=(b, s // tm),
        in_specs=[tile, halo, vec, vec, _const_spec(w.shape), vec],
        out_specs=[tile, pl.BlockSpec((1, n_tail, d), lambda i, j: (i, 0, 0))],
        out_shape=[jax.ShapeDtypeStruct((b, s, d), F32), jax.ShapeDtypeStruct((b, n_tail, d), F32)],
        scratch_shapes=[buf, buf, buf],
        compiler_params=_params(2),
        name="pool",
    )(x, x if state is None else state, g2, g3, w, scale)


def kernel(x_prompt, x_sample, cache_k, cache_v, cache_logf, state_conv, state_pool, page_table,
           norm_g, ffn_w_gate, ffn_w_up, ffn_w_down, mix_w_in, fgate_b, conv_dw_w, conv_dw_b,
           conv_ln_g, conv_ln_b, mix_w_out, pool_w, pool_scale):
    b, s, d = x_prompt.shape
    bd, t_new, _ = x_sample.shape
    depth = norm_g.shape[0]
    d_att = N_HEADS * HEAD_DIM
    d_conv = d - d_att
    page = cache_k.shape[2]
    n_pool = cache_k.shape[1]
    past_len = page_table.shape[1] * page
    tm_prompt = 512
    m_sample = bd * t_new
    t_pad = 8

    xp = x_prompt.reshape(b * s, d)
    xs = x_sample.reshape(m_sample, d)
    vec = lambda a: a.reshape(1, -1).astype(F32)

    def ffn_both(xp, xs, layer, half):
        wg = ffn_w_gate[layer, half].astype(BF16)
        wu = ffn_w_up[layer, half].astype(BF16)
        wd = ffn_w_down[layer, half].astype(BF16)
        gp, gq = vec(norm_g[layer, 4 * half]), vec(norm_g[layer, 4 * half + 1])
        return (_ffn(xp, gp, gq, wg, wu, wd, tm_prompt), _ffn(xs, gp, gq, wg, wu, wd, m_sample))

    r = jnp.arange(LANES)
    tri_incl = (r[:, None] <= r[None, :]).astype(F32)
    tri_group = tri_incl * (r[:, None] // t_new == r[None, :] // t_new).astype(F32)
    rp = jnp.arange(page)
    suffix_total = jnp.concatenate([(rp[:, None] > rp[None, :]).astype(F32), jnp.ones((page, page), F32)], axis=1)

    outs_p = {k: [] for k in ("k", "v", "lf", "conv", "pool")}
    outs_s = {k: [] for k in ("k", "v", "lf", "conv", "pool")}
    for layer in range(depth):
        g = norm_g[layer]
        xp, xs = ffn_both(xp, xs, layer, 0)
        if layer % 2 == 0:
            e = layer // 2
            w_in = mix_w_in[e]
            o = 3 * d_att + N_HEADS
            w_cat = jnp.concatenate(
                [w_in[:, :3 * d_att], w_in[:, o:], w_in[:, 3 * d_att:o],
                 jnp.zeros((d, LANES - N_HEADS), w_in.dtype)], axis=1).astype(BF16)
            bf = jnp.concatenate([fgate_b[e].astype(F32), jnp.zeros((LANES - N_HEADS,), F32)]).reshape(1, LANES)
            wa = mix_w_out[e, :d_att].astype(BF16)
            wc = mix_w_out[e, d_att:].astype(BF16)
            cw, cb = conv_dw_w[e].astype(F32), vec(conv_dw_b[e])
            lg, lb = vec(conv_ln_g[e]), vec(conv_ln_b[e])
            g2, g3 = vec(g[2]), vec(g[3])

            q, k, v, kb, vb, lf, u, ccol, crow = _inproj(
                xp.reshape(b, s, d), g2, w_cat, bf, tri_incl, tm_prompt, d_att, d_conv)
            att = _attn(q, kb, vb, ccol, crow, 256)
            cv = _conv(u, None, cw, cb, lg, lb, 256)
            xp = _outproj(xp, att.reshape(b * s, d_att), cv.reshape(b * s, d_conv), wa, wc, g3, tm_prompt)
            outs_p["k"].append(k.reshape(b, s, N_HEADS, HEAD_DIM))
            outs_p["v"].append(v.reshape(b, s, N_HEADS, HEAD_DIM))
            outs_p["lf"].append(lf)
            outs_p["conv"].append(u[:, s - (CONV_W - 1):])

            q, k, v, _, _, lf, u, ccol, crow = _inproj(
                xs.reshape(1, m_sample, d), g2, w_cat, bf, tri_group, m_sample, d_att, d_conv)
            pad_rows = lambda a: jnp.pad(a.reshape(bd, t_new, d_att), ((0, 0), (0, 16 - t_new), (0, 0)))
            cqcol = ccol.reshape(bd, t_new * N_HEADS, 1)
            cqrow = jnp.pad(crow.reshape(N_HEADS, bd, t_new).transpose(1, 0, 2), ((0, 0), (0, 0), (0, 16 - t_new)))
            att = _decode(page_table, q.reshape(bd, t_new, d_att), pad_rows(k), pad_rows(v), cqcol, cqrow,
                          suffix_total, cache_k[e].reshape(n_pool, page, d_att),
                          cache_v[e].reshape(n_pool, page, d_att), cache_logf[e].astype(F32), 8)
            st = state_conv[e].astype(F32)
            cv = _conv(jnp.pad(u.reshape(bd, t_new, d_conv), ((0, 0), (0, t_pad - t_new), (0, 0))),
                       jnp.pad(st, ((0, 0), (CONV_HALO - (CONV_W - 1), 0), (0, 0))), cw, cb, lg, lb, t_pad)
            xs = _outproj(xs, att.reshape(m_sample, d_att), cv[:, :t_new].reshape(m_sample, d_conv),
                          wa, wc, g3, m_sample)
            outs_s["k"].append(k.reshape(bd, t_new, N_HEADS, HEAD_DIM))
            outs_s["v"].append(v.reshape(bd, t_new, N_HEADS, HEAD_DIM))
            outs_s["lf"].append(lf.reshape(bd, t_new, N_HEADS))
            outs_s["conv"].append(jnp.concatenate([st, u.reshape(bd, t_new, d_conv)], axis=1)[:, t_new:])
        else:
            o = layer // 2
            g2, g3 = vec(g[2]), vec(g[3])
            pw = pool_w[o].astype(BF16)
            sc = vec(pool_scale[o])
            n_keep = POOL_WINDOWS[-1] - 1
            xp3, tail = _pool(xp.reshape(b, s, d), None, g2, g3, pw, sc, 256, 0)
            xp = xp3.reshape(b * s, d)
            outs_p["pool"].append(tail[:, POOL_TAIL - n_keep:])
            st = state_pool[o].astype(F32)
            xs3, hs = _pool(jnp.pad(xs.reshape(bd, t_new, d), ((0, 0), (0, t_pad - t_new), (0, 0))),
                            jnp.pad(st, ((0, 0), (POOL_HALO - n_keep, 0), (0, 0))), g2, g3, pw, sc, t_pad, past_len)
            xs = xs3[:, :t_new].reshape(m_sample, d)
            outs_s["pool"].append(jnp.concatenate([st, hs[:, :t_new]], axis=1)[:, t_new:])
        xp, xs = ffn_both(xp, xs, layer, 1)

    stack = lambda l: jnp.stack(l)
    return (xp.reshape(b, s, d), xs.reshape(bd, t_new, d),
            stack(outs_p["k"]), stack(outs_p["v"]), stack(outs_p["lf"]), stack(outs_p["conv"]),
            stack(outs_p["pool"]),
            stack(outs_s["k"]), stack(outs_s["v"]), stack(outs_s["lf"]), stack(outs_s["conv"]),
            stack(outs_s["pool"]))
```

```python
import functools

import jax
import jax.numpy as jnp
from jax import lax
from jax.experimental import pallas as pl
from jax.experimental.pallas import tpu as pltpu

F32 = jnp.float32
BF16 = jnp.bfloat16

RMS_EPS = 1e-6
LN_EPS = 1e-5
NEG_INF = -1e30
LOG2E = 1.4426950408889634

LANES = 128
HEAD_DIM = 64
N_HEADS = 8
Q_SCALE = HEAD_DIM ** -0.5 * LOG2E
CONV_W = 31
CONV_HALO = 32
POOL_WINDOWS = (2, 4, 8, 16)
POOL_HALO = 32
POOL_TAIL = 16
VMEM_LIMIT_BYTES = 56 * 1024 * 1024


def _params(n_axes):
    return pltpu.CompilerParams(dimension_semantics=("arbitrary",) * n_axes,
                                vmem_limit_bytes=VMEM_LIMIT_BYTES)


def _const_spec(shape, single_buffer=False):
    zeros = (0,) * len(shape)
    if single_buffer:
        return pl.BlockSpec(shape, lambda *_: zeros, pipeline_mode=pl.Buffered(1))
    return pl.BlockSpec(shape, lambda *_: zeros)


def _rms(x, g):
    ms = jnp.mean(x * x, axis=-1, keepdims=True)
    return x * lax.rsqrt(ms + RMS_EPS) * g


def _sigmoid(x):
    return 1.0 / (1.0 + jnp.exp(-x))


def _log_sigmoid(x):
    return jnp.minimum(x, 0.0) - jnp.log1p(jnp.exp(-jnp.abs(x)))


def _split3(x):
    hi = x.astype(BF16).astype(F32)
    r = x - hi
    mid = r.astype(BF16).astype(F32)
    return hi, mid, r - mid


def _dot3(x, w):
    hi, mid, lo = _split3(x)
    d = lambda a: jnp.dot(a, w, preferred_element_type=F32)
    return d(hi) + d(mid) + d(lo)


def _ffn_kernel(x_ref, gpre_ref, gpost_ref, wg_ref, wu_ref, wd_ref, o_ref, *, n_chunks):
    x = x_ref[...]
    h = _rms(x, gpre_ref[...]).astype(BF16)
    ck = wg_ref.shape[1] // n_chunks
    acc = None
    for c in range(n_chunks):
        sl = slice(c * ck, (c + 1) * ck)
        g = jnp.dot(h, wg_ref[:, sl], preferred_element_type=F32)
        u = jnp.dot(h, wu_ref[:, sl], preferred_element_type=F32)
        a = (g * _sigmoid(g) * u).astype(BF16)
        d = jnp.dot(a, wd_ref[sl, :], preferred_element_type=F32)
        acc = d if acc is None else acc + d
    o_ref[...] = x + 0.5 * _rms(acc, gpost_ref[...])


def _ffn(x, g_pre, g_post, wg, wu, wd, tm):
    m, d = x.shape
    tile = pl.BlockSpec((tm, d), lambda i: (i, 0))
    return pl.pallas_call(
        functools.partial(_ffn_kernel, n_chunks=2),
        grid=(m // tm,),
        in_specs=[tile, _const_spec((1, d)), _const_spec((1, d)),
                  _const_spec(wg.shape, True), _const_spec(wu.shape, True), _const_spec(wd.shape, True)],
        out_specs=tile,
        out_shape=jax.ShapeDtypeStruct((m, d), F32),
        compiler_params=_params(1),
        name="ffn",
    )(x, g_pre, g_post, wg, wu, wd)


def _inproj_kernel(x_ref, g_ref, w_ref, bf_ref, tri_ref, *refs, d_att, d_conv, transposed_kv):
    if transposed_kv:
        q_ref, kt_ref, vt_ref, ktb_ref, vb_ref, lft_ref, u_ref, ccol_ref, crow_ref, carry_ref = refs
    else:
        q_ref, k_ref, v_ref, lft_ref, u_ref, ccol_ref, crow_ref, carry_ref = refs
    tm = x_ref.shape[1]

    @pl.when(pl.program_id(1) == 0)
    def _():
        carry_ref[...] = jnp.zeros_like(carry_ref)

    h = _rms(x_ref[0], g_ref[...]).astype(BF16)
    z = jnp.dot(h, w_ref[...], preferred_element_type=F32)
    k = z[:, d_att:2 * d_att]
    v = z[:, 2 * d_att:3 * d_att]
    o = 3 * d_att
    a = z[:, o:o + d_conv]
    gate = z[:, o + d_conv:o + 2 * d_conv]
    fg = z[:, o + 2 * d_conv:o + 2 * d_conv + LANES]
    q_ref[0] = (z[:, :d_att] * Q_SCALE).astype(BF16)
    if transposed_kv:
        kt = k.T
        kt_ref[0] = kt
        vt_ref[0] = v.T
        ktb_ref[0] = kt.astype(BF16)
        vb_ref[0] = v.astype(BF16)
    else:
        k_ref[0] = k
        v_ref[0] = v
    u_ref[0] = a * _sigmoid(gate)
    lf = _log_sigmoid(fg + bf_ref[...])
    lft = lf.T[:N_HEADS]
    lft_ref[0] = lft

    carry = carry_ref[...]
    tri = tri_ref[...]
    chunks = []
    for c in range(tm // LANES):
        cs = _dot3(lft[:, c * LANES:(c + 1) * LANES], tri) + carry
        chunks.append(cs)
        carry = jnp.broadcast_to(cs[:, LANES - 1:LANES], cs.shape)
    carry_ref[...] = carry
    crow = jnp.concatenate(chunks, axis=1) * LOG2E
    crow_ref[0] = crow
    cfull = jnp.concatenate([crow, jnp.zeros((LANES - N_HEADS, tm), F32)], axis=0)
    ccol_ref[0] = cfull.T[:, :N_HEADS]


def _inproj(x, g, w, bf, tri, tm, d_att, d_conv, transposed_kv):
    b, s, d = x.shape
    tok = lambda width: pl.BlockSpec((1, tm, width), lambda i, j: (i, j, 0))
    tr = lambda rows: pl.BlockSpec((1, rows, tm), lambda i, j: (i, 0, j))
    shp = lambda width, dt: jax.ShapeDtypeStruct((b, s, width), dt)
    sht = lambda rows, dt: jax.ShapeDtypeStruct((b, rows, s), dt)
    if transposed_kv:
        kv_specs = [tr(d_att), tr(d_att), tr(d_att), tok(d_att)]
        kv_shapes = [sht(d_att, F32), sht(d_att, F32), sht(d_att, BF16), shp(d_att, BF16)]
    else:
        kv_specs = [tok(d_att), tok(d_att)]
        kv_shapes = [shp(d_att, F32), shp(d_att, F32)]
    return pl.pallas_call(
        functools.partial(_inproj_kernel, d_att=d_att, d_conv=d_conv, transposed_kv=transposed_kv),
        grid=(b, s // tm),
        in_specs=[tok(d), _const_spec((1, d)), _const_spec(w.shape, True), _const_spec((1, LANES)),
                  _const_spec((LANES, LANES))],
        out_specs=[tok(d_att)] + kv_specs + [tr(N_HEADS), tok(d_conv), tok(N_HEADS), tr(N_HEADS)],
        out_shape=[shp(d_att, BF16)] + kv_shapes + [sht(N_HEADS, F32), shp(d_conv, F32), shp(N_HEADS, F32),
                                                    sht(N_HEADS, F32)],
        scratch_shapes=[pltpu.VMEM((N_HEADS, LANES), F32)],
        compiler_params=_params(2),
        name="inproj",
    )(x, g, w, bf, tri)


def _attn_kernel(q_ref, kt_ref, v_ref, ccol_ref, crow_ref, o_ref, *, tq):
    s_len = q_ref.shape[1]
    pair = pl.program_id(1)
    row = lax.broadcasted_iota(jnp.int32, (tq, tq), 0)
    col = lax.broadcasted_iota(jnp.int32, (tq, tq), 1)
    causal = col <= row
    low = lax.broadcasted_iota(jnp.int32, (tq, LANES), 1) < HEAD_DIM
    ccol = ccol_ref[0]
    head_lane = lax.broadcasted_iota(jnp.int32, ccol.shape, 1)
    cq_all, ck_all = [], []
    for hh in range(2):
        head = 2 * pair + hh
        cq_all.append(jnp.sum(jnp.where(head_lane == head, ccol, 0.0), axis=-1, keepdims=True))
        ck_all.append(crow_ref[0, pl.ds(head, 1), :])

    for i in range(s_len // tq):
        k0 = i * tq
        q2 = q_ref[0, k0:k0 + tq, :]
        zero = jnp.zeros_like(q2)
        halves = []
        for hh in range(2):
            qh = jnp.where(low, q2, zero) if hh == 0 else jnp.where(low, zero, q2)
            cq = cq_all[hh][k0:k0 + tq]
            ck = ck_all[hh]
            s_diag = jnp.dot(qh, kt_ref[0, :, k0:k0 + tq], preferred_element_type=F32)
            s_diag = jnp.where(causal, s_diag + (cq - ck[:, k0:k0 + tq]), NEG_INF)
            m = jnp.max(s_diag, axis=-1, keepdims=True)
            if i > 0:
                s_off = jnp.dot(qh, kt_ref[0, :, :k0], preferred_element_type=F32) + (cq - ck[:, :k0])
                m = jnp.maximum(m, jnp.max(s_off, axis=-1, keepdims=True))
            p_diag = jnp.exp2(s_diag - m)
            l = jnp.sum(p_diag, axis=-1, keepdims=True)
            pv = jnp.dot(p_diag.astype(BF16), v_ref[0, k0:k0 + tq, :], preferred_element_type=F32)
            if i > 0:
                p_off = jnp.exp2(s_off - m)
                l = l + jnp.sum(p_off, axis=-1, keepdims=True)
                pv = pv + jnp.dot(p_off.astype(BF16), v_ref[0, :k0, :], preferred_element_type=F32)
            halves.append(pv * (1.0 / l))
        o_ref[0, k0:k0 + tq, :] = jnp.where(low, halves[0], halves[1]).astype(BF16)


def _attn(q, ktb, vb, ccol, crow, tq):
    b, s, d_att = q.shape
    cols = pl.BlockSpec((1, s, LANES), lambda i, j: (i, 0, j))
    return pl.pallas_call(
        functools.partial(_attn_kernel, tq=tq),
        grid=(b, d_att // LANES),
        in_specs=[cols, pl.BlockSpec((1, LANES, s), lambda i, j: (i, j, 0)), cols,
                  pl.BlockSpec((1, s, N_HEADS), lambda i, j: (i, 0, 0)),
                  pl.BlockSpec((1, N_HEADS, s), lambda i, j: (i, 0, 0))],
        out_specs=cols,
        out_shape=jax.ShapeDtypeStruct((b, s, d_att), BF16),
        compiler_params=_params(2),
        name="attn",
    )(q, ktb, vb, ccol, crow)


def _decode_kernel(pt_ref, q_ref, kn_ref, vn_ref, cqcol_ref, cqrow_ref, us_ref, *rest, n_group):
    k_refs = rest[:n_group]
    v_refs = rest[n_group:2 * n_group]
    lf_refs = rest[2 * n_group:3 * n_group]
    o_ref = rest[3 * n_group]
    qbd_ref, m_ref, l_ref, acc_ref, carry_ref = rest[3 * n_group + 1:]
    js = pl.program_id(1)
    t_new = q_ref.shape[1]
    d_att = q_ref.shape[2]
    n_rows = t_new * N_HEADS
    nt_dims = (((1,), (1,)), ((), ()))
    cqcol = cqcol_ref[0]

    @pl.when(js == 0)
    def _():
        q = q_ref[0].astype(F32)
        sub = lax.broadcasted_iota(jnp.int32, (N_HEADS, d_att), 0)
        lane = lax.broadcasted_iota(jnp.int32, (N_HEADS, d_att), 1)
        own = (lane // HEAD_DIM) == sub
        rows = [jnp.where(own, jnp.broadcast_to(q[t:t + 1, :], (N_HEADS, d_att)), 0.0) for t in range(t_new)]
        qbd = jnp.concatenate(rows, axis=0).astype(BF16)
        qbd_ref[...] = qbd
        s = lax.dot_general(qbd, kn_ref[0].astype(BF16), nt_dims, preferred_element_type=F32)
        cqk = jnp.concatenate([cqrow_ref[0]] * t_new, axis=0)
        r = lax.broadcasted_iota(jnp.int32, s.shape, 0)
        c = lax.broadcasted_iota(jnp.int32, s.shape, 1)
        s = jnp.where(c <= r // N_HEADS, s + (cqcol - cqk), NEG_INF)
        m = jnp.max(s, axis=-1, keepdims=True)
        pr = jnp.exp2(s - m)
        m_ref[...] = m
        l_ref[...] = jnp.sum(pr, axis=-1, keepdims=True)
        acc_ref[...] = jnp.dot(pr.astype(BF16), vn_ref[0].astype(BF16), preferred_element_type=F32)
        carry_ref[...] = jnp.zeros_like(carry_ref)

    qbd = qbd_ref[...]
    carry = carry_ref[...]
    us = us_ref[...]
    s_parts = []
    for g in range(n_group):
        lft = lf_refs[g][0]
        page = lft.shape[1]
        r = _dot3(lft, us)
        suffix = (r[:, :page] + carry) * LOG2E
        carry = carry + r[:, page:]
        bias = jnp.concatenate([suffix] * t_new, axis=0) + cqcol
        kt = k_refs[g][0].reshape(d_att, page).astype(BF16)
        s_parts.append(jnp.dot(qbd, kt, preferred_element_type=F32) + bias)
    carry_ref[...] = carry
    s = jnp.concatenate(s_parts, axis=1)
    m = m_ref[...]
    m_new = jnp.maximum(m, jnp.max(s, axis=-1, keepdims=True))
    alpha = jnp.exp2(m - m_new)
    pr = jnp.exp2(s - m_new)
    m_ref[...] = m_new
    l_ref[...] = alpha * l_ref[...] + jnp.sum(pr, axis=-1, keepdims=True)
    prb = pr.astype(BF16)
    acc = alpha * acc_ref[...]
    for g in range(n_group):
        page = v_refs[g].shape[3]
        vt = v_refs[g][0].reshape(d_att, page).astype(BF16)
        acc = acc + lax.dot_general(prb[:, g * page:(g + 1) * page], vt, nt_dims, preferred_element_type=F32)
    acc_ref[...] = acc

    @pl.when(js == pl.num_programs(1) - 1)
    def _():
        o = acc * (1.0 / l_ref[...])
        sub = lax.broadcasted_iota(jnp.int32, (N_HEADS, d_att), 0)
        lane = lax.broadcasted_iota(jnp.int32, (N_HEADS, d_att), 1)
        own = (lane // HEAD_DIM) == sub
        rows = [jnp.sum(jnp.where(own, o[t * N_HEADS:(t + 1) * N_HEADS], 0.0), axis=0, keepdims=True)
                for t in range(t_new)]
        o_ref[0] = jnp.concatenate(rows, axis=0).astype(o_ref.dtype)


def _decode(page_table, q, kn, vn, cqcol, cqrow, us, cache_k, cache_v, cache_lf, n_group):
    bd, t_new, d_att = q.shape
    n_pages = page_table.shape[1]
    page = cache_k.shape[3]
    n_rows = t_new * N_HEADS
    per_b = lambda shape: pl.BlockSpec((1,) + shape, lambda i, j, pt: (i,) + (0,) * len(shape))

    def paged(shape, g):
        return pl.BlockSpec((1,) + shape,
                            lambda i, j, pt: (pt[i, n_pages - 1 - (j * n_group + g)],) + (0,) * len(shape))

    in_specs = [per_b((t_new, d_att)), per_b(kn.shape[1:]), per_b(vn.shape[1:]), per_b((n_rows, 1)),
                per_b(cqrow.shape[1:]), pl.BlockSpec(us.shape, lambda i, j, pt: (0, 0))]
    in_specs += [paged((N_HEADS, HEAD_DIM, page), g) for g in range(n_group)]
    in_specs += [paged((N_HEADS, HEAD_DIM, page), g) for g in range(n_group)]
    in_specs += [paged((N_HEADS, page), g) for g in range(n_group)]
    grid_spec = pltpu.PrefetchScalarGridSpec(
        num_scalar_prefetch=1,
        grid=(bd, n_pages // n_group),
        in_specs=in_specs,
        out_specs=per_b((t_new, d_att)),
        scratch_shapes=[pltpu.VMEM((n_rows, d_att), BF16), pltpu.VMEM((n_rows, 1), F32),
                        pltpu.VMEM((n_rows, 1), F32), pltpu.VMEM((n_rows, d_att), F32),
                        pltpu.VMEM((N_HEADS, page), F32)],
    )
    return pl.pallas_call(
        functools.partial(_decode_kernel, n_group=n_group),
        grid_spec=grid_spec,
        out_shape=jax.ShapeDtypeStruct((bd, t_new, d_att), BF16),
        compiler_params=_params(2),
        name="decode",
    )(page_table, q, kn, vn, cqcol, cqrow, us,
      *([cache_k] * n_group), *([cache_v] * n_group), *([cache_lf] * n_group))


def _ln_swish(y, g, b):
    mu = jnp.mean(y, axis=-1, keepdims=True)
    d = y - mu
    var = jnp.mean(d * d, axis=-1, keepdims=True)
    yn = d * lax.rsqrt(var + LN_EPS) * g + b
    return yn * _sigmoid(yn)


def _conv_kernel(u_ref, halo_ref, w_ref, b_ref, lg_ref, lb_ref, o_ref, buf_ref, y_ref, *, rows, halo_is_state):
    tm = u_ref.shape[1]
    d_conv = u_ref.shape[2]
    halo = halo_ref[0]
    if not halo_is_state:
        halo = jnp.where(pl.program_id(1) == 0, 0.0, halo)
    buf_ref[0:CONV_HALO, :] = halo
    buf_ref[CONV_HALO:, :] = u_ref[0]
    base = CONV_HALO - (CONV_W - 1)
    for cb in range(d_conv // LANES):
        ls = slice(cb * LANES, (cb + 1) * LANES)
        for rb in range(tm // rows):
            acc = jnp.zeros((rows, LANES), F32)
            for tap in range(CONV_W):
                r0 = rb * rows + base + tap
                acc = acc + w_ref[tap:tap + 1, ls] * buf_ref[r0:r0 + rows, ls]
            y_ref[rb * rows:(rb + 1) * rows, ls] = acc + b_ref[:, ls]
    o_ref[0] = _ln_swish(y_ref[...], lg_ref[...], lb_ref[...]).astype(o_ref.dtype)


def _conv(u, state, w, b, lg, lb, tm):
    bsz, s, d_conv = u.shape
    per_halo = tm // CONV_HALO
    tile = pl.BlockSpec((1, tm, d_conv), lambda i, j: (i, j, 0))
    if state is None:
        halo = pl.BlockSpec((1, CONV_HALO, d_conv), lambda i, j: (i, jnp.maximum(j * per_halo - 1, 0), 0))
    else:
        assert s == tm and state.shape == (bsz, CONV_HALO, d_conv)
        halo = pl.BlockSpec((1, CONV_HALO, d_conv), lambda i, j: (i, 0, 0))
    vec = _const_spec((1, d_conv))
    return pl.pallas_call(
        functools.partial(_conv_kernel, rows=min(tm, 128), halo_is_state=state is not None),
        grid=(bsz, s // tm),
        in_specs=[tile, halo, _const_spec(w.shape), vec, vec, vec],
        out_specs=tile,
        out_shape=jax.ShapeDtypeStruct((bsz, s, d_conv), F32 if tm < 16 else BF16),
        scratch_shapes=[pltpu.VMEM((tm + CONV_HALO, d_conv), F32), pltpu.VMEM((tm, d_conv), F32)],
        compiler_params=_params(2),
        name="conv",
    )(u, u if state is None else state, w, b, lg, lb)


def _outproj_kernel(x_ref, att_ref, cv_ref, wa_ref, wc_ref, g_ref, o_ref):
    m = jnp.dot(att_ref[...].astype(BF16), wa_ref[...], preferred_element_type=F32)
    m = m + jnp.dot(cv_ref[...].astype(BF16), wc_ref[...], preferred_element_type=F32)
    o_ref[...] = x_ref[...] + _rms(m, g_ref[...])


def _outproj(x, att, cv, wa, wc, g, tm):
    m, d = x.shape
    tile = lambda width: pl.BlockSpec((tm, width), lambda i: (i, 0))
    return pl.pallas_call(
        _outproj_kernel,
        grid=(m // tm,),
        in_specs=[tile(d), tile(att.shape[1]), tile(cv.shape[1]), _const_spec(wa.shape), _const_spec(wc.shape),
                  _const_spec((1, d))],
        out_specs=tile(d),
        out_shape=jax.ShapeDtypeStruct((m, d), F32),
        compiler_params=_params(1),
        name="outproj",
    )(x, att, cv, wa, wc, g)


def _pool_kernel(x_ref, halo_ref, g2_ref, g3_ref, w_ref, sc_ref, o_ref, tail_ref, buf_ref, ta_ref, tb_ref,
                 *, pos0, halo_is_state):
    tm = x_ref.shape[1]
    d = x_ref.shape[2]
    grp = d // len(POOL_WINDOWS)
    n = tm + POOL_HALO
    j = pl.program_id(1)
    x = x_ref[0]
    h = _rms(x, g2_ref[...])
    if halo_is_state:
        halo = halo_ref[0]
    else:
        halo = jnp.where(j == 0, 0.0, _rms(halo_ref[0], g2_ref[...]))
    buf_ref[0:POOL_HALO, :] = halo
    buf_ref[POOL_HALO:, :] = h

    @pl.when(j == pl.num_programs(1) - 1)
    def _():
        tail_ref[0] = h[tm - tail_ref.shape[1]:, :]

    ta_ref[8:n, :] = buf_ref[8:n, :] + buf_ref[7:n - 1, :]
    tb_ref[16:n, grp:] = ta_ref[16:n, grp:] + ta_ref[14:n - 2, grp:]
    ta_ref[24:n, 2 * grp:] = tb_ref[24:n, 2 * grp:] + tb_ref[20:n - 4, 2 * grp:]
    s16 = ta_ref[32:n, 3 * grp:] + ta_ref[24:n - 8, 3 * grp:]
    totals = (ta_ref[POOL_HALO:n, 0:grp], tb_ref[POOL_HALO:n, grp:2 * grp],
              ta_ref[POOL_HALO:n, 2 * grp:3 * grp], s16)

    pos = pos0 + j * tm + lax.broadcasted_iota(jnp.int32, (tm, 1), 0)
    outs = []
    for gi, win in enumerate(POOL_WINDOWS):
        cnt = jnp.minimum(pos + 1, win).astype(F32)
        mixed = totals[gi] / cnt - h[:, gi * grp:(gi + 1) * grp]
        outs.append(jnp.dot(mixed.astype(BF16), w_ref[gi], preferred_element_type=F32))
    mix = jnp.concatenate(outs, axis=-1) * sc_ref[...]
    o_ref[0] = x + _rms(mix, g3_ref[...])


def _pool(x, state, g2, g3, w, scale, tm, pos0):
    b, s, d = x.shape
    per_halo = tm // POOL_HALO
    n_tail = min(tm, POOL_TAIL)
    tile = pl.BlockSpec((1, tm, d), lambda i, j: (i, j, 0))
    if state is None:
        halo = pl.BlockSpec((1, POOL_HALO, d), lambda i, j: (i, jnp.maximum(j * per_halo - 1, 0), 0))
    else:
        assert s == tm and state.shape == (b, POOL_HALO, d)
        halo = pl.BlockSpec((1, POOL_HALO, d), lambda i, j: (i, 0, 0))
    vec = _const_spec((1, d))
    buf = pltpu.VMEM((tm + POOL_HALO, d), F32)
    return pl.pallas_call(
        functools.partial(_pool_kernel, pos0=pos0, halo_is_state=state is not None),
        grid=(b, s // tm),
        in_specs=[tile, halo, vec, vec, _const_spec(w.shape), vec],
        out_specs=[tile, pl.BlockSpec((1, n_tail, d), lambda i, j: (i, 0, 0))],
        out_shape=[jax.ShapeDtypeStruct((b, s, d), F32), jax.ShapeDtypeStruct((b, n_tail, d), F32)],
        scratch_shapes=[buf, buf, buf],
        compiler_params=_params(2),
        name="pool",
    )(x, x if state is None else state, g2, g3, w, scale)


def kernel(x_prompt, x_sample, cache_k, cache_v, cache_logf, state_conv, state_pool, page_table,
           norm_g, ffn_w_gate, ffn_w_up, ffn_w_down, mix_w_in, fgate_b, conv_dw_w, conv_dw_b,
           conv_ln_g, conv_ln_b, mix_w_out, pool_w, pool_scale):
    b, s, d = x_prompt.shape
    bd, t_new, _ = x_sample.shape
    depth = norm_g.shape[0]
    d_att = N_HEADS * HEAD_DIM
    d_conv = d - d_att
    page = cache_k.shape[2]
    n_pool = cache_k.shape[1]
    past_len = page_table.shape[1] * page
    tm_prompt = 512
    m_sample = bd * t_new
    t_pad = 8

    xp = x_prompt.reshape(b * s, d)
    xs = x_sample.reshape(m_sample, d)
    vec = lambda a: a.reshape(1, -1).astype(F32)

    def ffn_both(xp, xs, layer, half):
        wg = ffn_w_gate[layer, half].astype(BF16)
        wu = ffn_w_up[layer, half].astype(BF16)
        wd = ffn_w_down[layer, half].astype(BF16)
        gp, gq = vec(norm_g[layer, 4 * half]), vec(norm_g[layer, 4 * half + 1])
        return (_ffn(xp, gp, gq, wg, wu, wd, tm_prompt), _ffn(xs, gp, gq, wg, wu, wd, m_sample))

    r = jnp.arange(LANES)
    tri_incl = (r[:, None] <= r[None, :]).astype(F32)
    tri_group = tri_incl * (r[:, None] // t_new == r[None, :] // t_new).astype(F32)
    rp = jnp.arange(page)
    suffix_total = jnp.concatenate([(rp[:, None] > rp[None, :]).astype(F32), jnp.ones((page, page), F32)], axis=1)

    outs_p = {k: [] for k in ("k", "v", "lf", "conv", "pool")}
    outs_s = {k: [] for k in ("k", "v", "lf", "conv", "pool")}
    for layer in range(depth):
        g = norm_g[layer]
        xp, xs = ffn_both(xp, xs, layer, 0)
        if layer % 2 == 0:
            e = layer // 2
            w_in = mix_w_in[e]
            o = 3 * d_att + N_HEADS
            w_cat = jnp.concatenate(
                [w_in[:, :3 * d_att], w_in[:, o:], w_in[:, 3 * d_att:o],
                 jnp.zeros((d, LANES - N_HEADS), w_in.dtype)], axis=1).astype(BF16)
            bf = jnp.concatenate([fgate_b[e].astype(F32), jnp.zeros((LANES - N_HEADS,), F32)]).reshape(1, LANES)
            wa = mix_w_out[e, :d_att].astype(BF16)
            wc = mix_w_out[e, d_att:].astype(BF16)
            cw, cb = conv_dw_w[e].astype(F32), vec(conv_dw_b[e])
            lg, lb = vec(conv_ln_g[e]), vec(conv_ln_b[e])
            g2, g3 = vec(g[2]), vec(g[3])

            q, kt, vt, ktb, vb, lft, u, ccol, crow = _inproj(
                xp.reshape(b, s, d), g2, w_cat, bf, tri_incl, tm_prompt, d_att, d_conv, True)
            att = _attn(q, ktb, vb, ccol, crow, 256)
            cv = _conv(u, None, cw, cb, lg, lb, 256)
            xp = _outproj(xp, att.reshape(b * s, d_att), cv.reshape(b * s, d_conv), wa, wc, g3, tm_prompt)
            outs_p["k"].append(kt.reshape(b, N_HEADS, HEAD_DIM, s).transpose(0, 3, 1, 2))
            outs_p["v"].append(vt.reshape(b, N_HEADS, HEAD_DIM, s).transpose(0, 3, 1, 2))
            outs_p["lf"].append(lft.transpose(0, 2, 1))
            outs_p["conv"].append(u[:, s - (CONV_W - 1):])

            q, k, v, lft, u, ccol, crow = _inproj(
                xs.reshape(1, m_sample, d), g2, w_cat, bf, tri_group, m_sample, d_att, d_conv, False)
            lf = lft.reshape(N_HEADS, m_sample).T
            pad_rows = lambda a: jnp.pad(a.reshape(bd, t_new, d_att), ((0, 0), (0, 16 - t_new), (0, 0)))
            cqcol = ccol.reshape(bd, t_new * N_HEADS, 1)
            cqrow = jnp.pad(crow.reshape(N_HEADS, bd, t_new).transpose(1, 0, 2), ((0, 0), (0, 0), (0, 16 - t_new)))
            att = _decode(page_table, q.reshape(bd, t_new, d_att), pad_rows(k), pad_rows(v), cqcol, cqrow,
                          suffix_total, cache_k[e].transpose(0, 2, 3, 1), cache_v[e].transpose(0, 2, 3, 1),
                          cache_logf[e].astype(F32).transpose(0, 2, 1), 8)
            st = state_conv[e].astype(F32)
            cv = _conv(jnp.pad(u.reshape(bd, t_new, d_conv), ((0, 0), (0, t_pad - t_new), (0, 0))),
                       jnp.pad(st, ((0, 0), (CONV_HALO - (CONV_W - 1), 0), (0, 0))), cw, cb, lg, lb, t_pad)
            xs = _outproj(xs, att.reshape(m_sample, d_att), cv[:, :t_new].reshape(m_sample, d_conv),
                          wa, wc, g3, m_sample)
            outs_s["k"].append(k.reshape(bd, t_new, N_HEADS, HEAD_DIM))
            outs_s["v"].append(v.reshape(bd, t_new, N_HEADS, HEAD_DIM))
            outs_s["lf"].append(lf.reshape(bd, t_new, N_HEADS))
            outs_s["conv"].append(jnp.concatenate([st, u.reshape(bd, t_new, d_conv)], axis=1)[:, t_new:])
        else:
            o = layer // 2
            g2, g3 = vec(g[2]), vec(g[3])
            pw = pool_w[o].astype(BF16)
            sc = vec(pool_scale[o])
            n_keep = POOL_WINDOWS[-1] - 1
            xp3, tail = _pool(xp.reshape(b, s, d), None, g2, g3, pw, sc, 256, 0)
            xp = xp3.reshape(b * s, d)
            outs_p["pool"].append(tail[:, POOL_TAIL - n_keep:])
            st = state_pool[o].astype(F32)
            xs3, hs = _pool(jnp.pad(xs.reshape(bd, t_new, d), ((0, 0), (0, t_pad - t_new), (0, 0))),
                            jnp.pad(st, ((0, 0), (POOL_HALO - n_keep, 0), (0, 0))), g2, g3, pw, sc, t_pad, past_len)
            xs = xs3[:, :t_new].reshape(m_sample, d)
            outs_s["pool"].append(jnp.concatenate([st, hs[:, :t_new]], axis=1)[:, t_new:])
        xp, xs = ffn_both(xp, xs, layer, 1)

    stack = lambda l: jnp.stack(l)
    return (xp.reshape(b, s, d), xs.reshape(bd, t_new, d),
            stack(outs_p["k"]), stack(outs_p["v"]), stack(outs_p["lf"]), stack(outs_p["conv"]),
            stack(outs_p["pool"]),
            stack(outs_s["k"]), stack(outs_s["v"]), stack(outs_s["lf"]), stack(outs_s["conv"]),
            stack(outs_s["pool"]))
```

```python
import functools

import jax
import jax.numpy as jnp
from jax import lax
from jax.experimental import pallas as pl
from jax.experimental.pallas import tpu as pltpu

F32 = jnp.float32
BF16 = jnp.bfloat16

RMS_EPS = 1e-6
LN_EPS = 1e-5
NEG_INF = -1e30
LOG2E = 1.4426950408889634

LANES = 128
HEAD_DIM = 64
N_HEADS = 8
Q_SCALE = HEAD_DIM ** -0.5 * LOG2E
CONV_W = 31
CONV_HALO = 32
POOL_WINDOWS = (2, 4, 8, 16)
POOL_HALO = 32
POOL_TAIL = 16
VMEM_LIMIT_BYTES = 56 * 1024 * 1024


def _params(n_axes):
    return pltpu.CompilerParams(dimension_semantics=("arbitrary",) * n_axes,
                                vmem_limit_bytes=VMEM_LIMIT_BYTES)


def _const_spec(shape, single_buffer=False):
    zeros = (0,) * len(shape)
    if single_buffer:
        return pl.BlockSpec(shape, lambda *_: zeros, pipeline_mode=pl.Buffered(1))
    return pl.BlockSpec(shape, lambda *_: zeros)


def _rms(x, g):
    ms = jnp.mean(x * x, axis=-1, keepdims=True)
    return x * lax.rsqrt(ms + RMS_EPS) * g


def _sigmoid(x):
    return 1.0 / (1.0 + jnp.exp(-x))


def _log_sigmoid(x):
    return jnp.minimum(x, 0.0) - jnp.log1p(jnp.exp(-jnp.abs(x)))


def _split3(x):
    hi = x.astype(BF16).astype(F32)
    r = x - hi
    mid = r.astype(BF16).astype(F32)
    return hi, mid, r - mid


def _dot3(x, w):
    hi, mid, lo = _split3(x)
    d = lambda a: jnp.dot(a, w, preferred_element_type=F32)
    return d(hi) + d(mid) + d(lo)


def _ffn_kernel(x_ref, gpre_ref, gpost_ref, wg_ref, wu_ref, wd_ref, o_ref, a_ref, *, chunk):
    x = x_ref[...]
    h = _rms(x, gpre_ref[...]).astype(BF16)
    d_ff = wg_ref.shape[1]
    for c0 in range(0, d_ff, chunk):
        sl = slice(c0, min(c0 + chunk, d_ff))
        g = jnp.dot(h, wg_ref[:, sl], preferred_element_type=F32)
        u = jnp.dot(h, wu_ref[:, sl], preferred_element_type=F32)
        a_ref[:, sl] = (g * _sigmoid(g) * u).astype(BF16)
    y = jnp.dot(a_ref[...], wd_ref[...], preferred_element_type=F32)
    o_ref[...] = x + 0.5 * _rms(y, gpost_ref[...])


def _ffn(x, g_pre, g_post, wg, wu, wd, tm):
    m, d = x.shape
    tile = pl.BlockSpec((tm, d), lambda i: (i, 0))
    return pl.pallas_call(
        functools.partial(_ffn_kernel, chunk=512),
        grid=(m // tm,),
        in_specs=[tile, _const_spec((1, d)), _const_spec((1, d)),
                  _const_spec(wg.shape, True), _const_spec(wu.shape, True), _const_spec(wd.shape, True)],
        out_specs=tile,
        out_shape=jax.ShapeDtypeStruct((m, d), F32),
        scratch_shapes=[pltpu.VMEM((tm, wg.shape[1]), BF16)],
        compiler_params=_params(1),
        name="ffn",
    )(x, g_pre, g_post, wg, wu, wd)


def _ln_swish(y, g, b):
    mu = jnp.mean(y, axis=-1, keepdims=True)
    d = y - mu
    var = jnp.mean(d * d, axis=-1, keepdims=True)
    yn = d * lax.rsqrt(var + LN_EPS) * g + b
    return yn * _sigmoid(yn)


def _causal_conv(buf_ref, rot_ref, w_ref, b_ref, y_ref, *, t0, tm, rows):
    d_conv = buf_ref.shape[1]
    base = CONV_HALO - (CONV_W - 1)
    n_rot = tm + CONV_HALO - 8
    for r in range(1, 8):
        rot_ref[r - 1, t0:t0 + n_rot, :] = buf_ref[t0 + r:t0 + r + n_rot, :]
    for cb in range(d_conv // LANES):
        ls = slice(cb * LANES, (cb + 1) * LANES)
        for rb in range(tm // rows):
            acc = jnp.zeros((rows, LANES), F32)
            for tap in range(CONV_W):
                a, r = divmod(base + tap, 8)
                r0 = t0 + rb * rows + 8 * a
                src = buf_ref[r0:r0 + rows, ls] if r == 0 else rot_ref[r - 1, r0:r0 + rows, ls]
                acc = acc + w_ref[tap:tap + 1, ls] * src
            y_ref[t0 + rb * rows:t0 + (rb + 1) * rows, ls] = acc + b_ref[:, ls]


def _conv_scratch(tm, d_conv):
    return [pltpu.VMEM((tm + CONV_HALO, d_conv), F32), pltpu.VMEM((7, tm + CONV_HALO - 8, d_conv), F32),
            pltpu.VMEM((tm, d_conv), F32)]


def _inproj_kernel(x_ref, g_ref, w_ref, bf_ref, tri_ref, *refs, d_att, d_conv, prompt, sub):
    if prompt:
        (cw_ref, cb_ref, lg_ref, lb_ref,
         q_ref, kt_ref, vt_ref, ktb_ref, vb_ref, lft_ref, cv_ref, utail_ref, ccol_ref, crow_ref,
         carry_ref, buf_ref, rot_ref, y_ref) = refs
    else:
        q_ref, k_ref, v_ref, lft_ref, u_ref, ccol_ref, crow_ref, carry_ref = refs
    tm = x_ref.shape[1]
    j = pl.program_id(1)

    @pl.when(j == 0)
    def _():
        carry_ref[...] = jnp.zeros_like(carry_ref)
        if prompt:
            buf_ref[0:CONV_HALO, :] = jnp.zeros((CONV_HALO, d_conv), F32)

    if prompt:
        @pl.when(j > 0)
        def _():
            buf_ref[0:CONV_HALO, :] = buf_ref[tm:tm + CONV_HALO, :]

    carry = carry_ref[...]
    tri = tri_ref[...]
    for t0 in range(0, tm, sub):
        rs = slice(t0, t0 + sub)
        h = _rms(x_ref[0, rs, :], g_ref[...]).astype(BF16)
        z = jnp.dot(h, w_ref[...], preferred_element_type=F32)
        k = z[:, d_att:2 * d_att]
        v = z[:, 2 * d_att:3 * d_att]
        o = 3 * d_att
        a = z[:, o:o + d_conv]
        gate = z[:, o + d_conv:o + 2 * d_conv]
        fg = z[:, o + 2 * d_conv:o + 2 * d_conv + LANES]
        q_ref[0, rs, :] = (z[:, :d_att] * Q_SCALE).astype(BF16)
        u = a * _sigmoid(gate)
        if prompt:
            kt = k.T
            kt_ref[0, :, rs] = kt
            vt_ref[0, :, rs] = v.T
            ktb_ref[0, :, rs] = kt.astype(BF16)
            vb_ref[0, rs, :] = v.astype(BF16)
            buf_ref[CONV_HALO + t0:CONV_HALO + t0 + sub, :] = u
            _causal_conv(buf_ref, rot_ref, cw_ref, cb_ref, y_ref, t0=t0, tm=sub, rows=min(sub, 128))
            cv_ref[0, rs, :] = _ln_swish(y_ref[rs, :], lg_ref[...], lb_ref[...]).astype(cv_ref.dtype)
        else:
            k_ref[0, rs, :] = k
            v_ref[0, rs, :] = v
            u_ref[0, rs, :] = u
        lf = _log_sigmoid(fg + bf_ref[...])
        lft = lf.T[:N_HEADS]
        lft_ref[0, :, rs] = lft

        chunks = []
        for c in range(sub // LANES):
            cs = _dot3(lft[:, c * LANES:(c + 1) * LANES], tri) + carry
            chunks.append(cs)
            carry = jnp.broadcast_to(cs[:, LANES - 1:LANES], cs.shape)
        crow = jnp.concatenate(chunks, axis=1) * LOG2E
        crow_ref[0, :, rs] = crow
        cfull = jnp.concatenate([crow, jnp.zeros((LANES - N_HEADS, sub), F32)], axis=0)
        ccol_ref[0, rs, :] = cfull.T[:, :N_HEADS]
    carry_ref[...] = carry

    if prompt:
        @pl.when(j == pl.num_programs(1) - 1)
        def _():
            utail_ref[0] = buf_ref[tm:tm + CONV_HALO, :]


def _inproj(x, g, w, bf, tri, conv, tm, d_att, d_conv):
    b, s, d = x.shape
    prompt = conv is not None
    tok = lambda width: pl.BlockSpec((1, tm, width), lambda i, j: (i, j, 0))
    tr = lambda rows: pl.BlockSpec((1, rows, tm), lambda i, j: (i, 0, j))
    shp = lambda width, dt: jax.ShapeDtypeStruct((b, s, width), dt)
    sht = lambda rows, dt: jax.ShapeDtypeStruct((b, rows, s), dt)
    in_specs = [tok(d), _const_spec((1, d)), _const_spec(w.shape, True), _const_spec((1, LANES)),
                _const_spec((LANES, LANES))]
    scratch = [pltpu.VMEM((N_HEADS, LANES), F32)]
    if prompt:
        in_specs += [_const_spec(conv[0].shape)] + [_const_spec((1, d_conv))] * 3
        out_specs = [tok(d_att), tr(d_att), tr(d_att), tr(d_att), tok(d_att), tr(N_HEADS), tok(d_conv),
                     pl.BlockSpec((1, CONV_HALO, d_conv), lambda i, j: (i, 0, 0))]
        out_shape = [shp(d_att, BF16), sht(d_att, F32), sht(d_att, F32), sht(d_att, BF16), shp(d_att, BF16),
                     sht(N_HEADS, F32), shp(d_conv, BF16), jax.ShapeDtypeStruct((b, CONV_HALO, d_conv), F32)]
        scratch += _conv_scratch(tm, d_conv)
    else:
        out_specs = [tok(d_att), tok(d_att), tok(d_att), tr(N_HEADS), tok(d_conv)]
        out_shape = [shp(d_att, BF16), shp(d_att, F32), shp(d_att, F32), sht(N_HEADS, F32), shp(d_conv, F32)]
    return pl.pallas_call(
        functools.partial(_inproj_kernel, d_att=d_att, d_conv=d_conv, prompt=prompt, sub=min(tm, 256)),
        grid=(b, s // tm),
        in_specs=in_specs,
        out_specs=out_specs + [tok(N_HEADS), tr(N_HEADS)],
        out_shape=out_shape + [shp(N_HEADS, F32), sht(N_HEADS, F32)],
        scratch_shapes=scratch,
        compiler_params=_params(2),
        name="inproj",
    )(x, g, w, bf, tri, *(conv or ()))


def _attn_kernel(q_ref, kt_ref, v_ref, ccol_ref, crow_ref, o_ref, *, tq):
    s_len = q_ref.shape[1]
    pair = pl.program_id(1)
    row = lax.broadcasted_iota(jnp.int32, (tq, tq), 0)
    col = lax.broadcasted_iota(jnp.int32, (tq, tq), 1)
    causal = col <= row
    low = lax.broadcasted_iota(jnp.int32, (tq, LANES), 1) < HEAD_DIM
    ccol = ccol_ref[0]
    head_lane = lax.broadcasted_iota(jnp.int32, ccol.shape, 1)
    cq_all, ck_all = [], []
    for hh in range(2):
        head = 2 * pair + hh
        cq_all.append(jnp.sum(jnp.where(head_lane == head, ccol, 0.0), axis=-1, keepdims=True))
        ck_all.append(crow_ref[0, pl.ds(head, 1), :])

    for i in range(s_len // tq):
        k0 = i * tq
        q2 = q_ref[0, k0:k0 + tq, :]
        zero = jnp.zeros_like(q2)
        halves = []
        for hh in range(2):
            qh = jnp.where(low, q2, zero) if hh == 0 else jnp.where(low, zero, q2)
            cq = cq_all[hh][k0:k0 + tq]
            ck = ck_all[hh]
            s_diag = jnp.dot(qh, kt_ref[0, :, k0:k0 + tq], preferred_element_type=F32)
            s_diag = jnp.where(causal, s_diag + (cq - ck[:, k0:k0 + tq]), NEG_INF)
            m = jnp.max(s_diag, axis=-1, keepdims=True)
            if i > 0:
                s_off = jnp.dot(qh, kt_ref[0, :, :k0], preferred_element_type=F32) + (cq - ck[:, :k0])
                m = jnp.maximum(m, jnp.max(s_off, axis=-1, keepdims=True))
            p_diag = jnp.exp2(s_diag - m)
            l = jnp.sum(p_diag, axis=-1, keepdims=True)
            pv = jnp.dot(p_diag.astype(BF16), v_ref[0, k0:k0 + tq, :], preferred_element_type=F32)
            if i > 0:
                p_off = jnp.exp2(s_off - m)
                l = l + jnp.sum(p_off, axis=-1, keepdims=True)
                pv = pv + jnp.dot(p_off.astype(BF16), v_ref[0, :k0, :], preferred_element_type=F32)
            halves.append(pv * (1.0 / l))
        o_ref[0, k0:k0 + tq, :] = jnp.where(low, halves[0], halves[1]).astype(BF16)


def _attn(q, ktb, vb, ccol, crow, tq):
    b, s, d_att = q.shape
    cols = pl.BlockSpec((1, s, LANES), lambda i, j: (i, 0, j))
    return pl.pallas_call(
        functools.partial(_attn_kernel, tq=tq),
        grid=(b, d_att // LANES),
        in_specs=[cols, pl.BlockSpec((1, LANES, s), lambda i, j: (i, j, 0)), cols,
                  pl.BlockSpec((1, s, N_HEADS), lambda i, j: (i, 0, 0)),
                  pl.BlockSpec((1, N_HEADS, s), lambda i, j: (i, 0, 0))],
        out_specs=cols,
        out_shape=jax.ShapeDtypeStruct((b, s, d_att), BF16),
        compiler_params=_params(2),
        name="attn",
    )(q, ktb, vb, ccol, crow)


def _decode_kernel(pt_ref, q_ref, kn_ref, vn_ref, cqcol_ref, cqrow_ref, us_ref, kc_ref, vc_ref, lc_ref, o_ref,
                   kbuf, vbuf, lbuf, sem, qbd_ref, m_ref, l_ref, acc_ref, carry_ref, *, n_group, n_steps):
    t = pl.program_id(0)
    js = lax.rem(t, n_steps)
    slot = lax.rem(t, 2)
    n_pages = pt_ref.shape[1]
    t_new = q_ref.shape[1]
    d_att = q_ref.shape[2]
    page = kbuf.shape[-1]
    n_rows = t_new * N_HEADS
    nt_dims = (((1,), (1,)), ((), ()))
    cqcol = cqcol_ref[0]

    def page_copies(step, to_slot, pages):
        out = []
        for g in range(n_group):
            p = pages(step, g)
            out.append(pltpu.make_async_copy(kc_ref.at[p], kbuf.at[to_slot, g], sem.at[to_slot, 0]))
            out.append(pltpu.make_async_copy(vc_ref.at[p], vbuf.at[to_slot, g], sem.at[to_slot, 1]))
            out.append(pltpu.make_async_copy(lc_ref.at[p], lbuf.at[to_slot, g], sem.at[to_slot, 2]))
        return out

    def table_page(step, g):
        return pt_ref[lax.div(step, n_steps), n_pages - 1 - (lax.rem(step, n_steps) * n_group + g)]

    @pl.when(t == 0)
    def _():
        for c in page_copies(t, slot, table_page):
            c.start()

    @pl.when(t + 1 < pl.num_programs(0))
    def _():
        for c in page_copies(t + 1, 1 - slot, table_page):
            c.start()

    @pl.when(js == 0)
    def _():
        q = q_ref[0].astype(F32)
        sub = lax.broadcasted_iota(jnp.int32, (N_HEADS, d_att), 0)
        lane = lax.broadcasted_iota(jnp.int32, (N_HEADS, d_att), 1)
        own = (lane // HEAD_DIM) == sub
        rows = [jnp.where(own, jnp.broadcast_to(q[i:i + 1, :], (N_HEADS, d_att)), 0.0) for i in range(t_new)]
        qbd = jnp.concatenate(rows, axis=0).astype(BF16)
        qbd_ref[...] = qbd
        s = lax.dot_general(qbd, kn_ref[0].astype(BF16), nt_dims, preferred_element_type=F32)
        cqk = jnp.concatenate([cqrow_ref[0]] * t_new, axis=0)
        r = lax.broadcasted_iota(jnp.int32, s.shape, 0)
        c = lax.broadcasted_iota(jnp.int32, s.shape, 1)
        s = jnp.where(c <= r // N_HEADS, s + (cqcol - cqk), NEG_INF)
        m = jnp.max(s, axis=-1, keepdims=True)
        pr = jnp.exp2(s - m)
        m_ref[...] = m
        l_ref[...] = jnp.sum(pr, axis=-1, keepdims=True)
        acc_ref[...] = jnp.dot(pr.astype(BF16), vn_ref[0].astype(BF16), preferred_element_type=F32)
        carry_ref[...] = jnp.zeros_like(carry_ref)

    for c in page_copies(t, slot, lambda step, g: 0):
        c.wait()

    qbd = qbd_ref[...]
    carry = carry_ref[...]
    us = us_ref[...]
    s_parts = []
    for g in range(n_group):
        r = _dot3(lbuf[slot, g], us)
        suffix = (r[:, :page] + carry) * LOG2E
        carry = carry + r[:, page:]
        bias = jnp.concatenate([suffix] * t_new, axis=0) + cqcol
        kt = kbuf[slot, g].reshape(d_att, page).astype(BF16)
        s_parts.append(jnp.dot(qbd, kt, preferred_element_type=F32) + bias)
    carry_ref[...] = carry
    s = jnp.concatenate(s_parts, axis=1)
    m = m_ref[...]
    m_new = jnp.maximum(m, jnp.max(s, axis=-1, keepdims=True))
    alpha = jnp.exp2(m - m_new)
    pr = jnp.exp2(s - m_new)
    m_ref[...] = m_new
    l_ref[...] = alpha * l_ref[...] + jnp.sum(pr, axis=-1, keepdims=True)
    prb = pr.astype(BF16)
    acc = alpha * acc_ref[...]
    for g in range(n_group):
        vt = vbuf[slot, g].reshape(d_att, page).astype(BF16)
        acc = acc + lax.dot_general(prb[:, g * page:(g + 1) * page], vt, nt_dims, preferred_element_type=F32)
    acc_ref[...] = acc

    @pl.when(js == n_steps - 1)
    def _():
        o = acc * (1.0 / l_ref[...])
        sub = lax.broadcasted_iota(jnp.int32, (N_HEADS, d_att), 0)
        lane = lax.broadcasted_iota(jnp.int32, (N_HEADS, d_att), 1)
        own = (lane // HEAD_DIM) == sub
        rows = [jnp.sum(jnp.where(own, o[i * N_HEADS:(i + 1) * N_HEADS], 0.0), axis=0, keepdims=True)
                for i in range(t_new)]
        o_ref[0] = jnp.concatenate(rows, axis=0).astype(o_ref.dtype)


def _decode(page_table, q, kn, vn, cqcol, cqrow, us, cache_k, cache_v, cache_lf, n_group):
    bd, t_new, d_att = q.shape
    n_pages = page_table.shape[1]
    page = cache_k.shape[3]
    n_rows = t_new * N_HEADS
    n_steps = n_pages // n_group
    per_b = lambda shape: pl.BlockSpec((1,) + shape, lambda t, pt: (t // n_steps,) + (0,) * len(shape))
    hbm = pl.BlockSpec(memory_space=pl.ANY)
    grid_spec = pltpu.PrefetchScalarGridSpec(
        num_scalar_prefetch=1,
        grid=(bd * n_steps,),
        in_specs=[per_b((t_new, d_att)), per_b(kn.shape[1:]), per_b(vn.shape[1:]), per_b((n_rows, 1)),
                  per_b(cqrow.shape[1:]), pl.BlockSpec(us.shape, lambda t, pt: (0, 0)), hbm, hbm, hbm],
        out_specs=per_b((t_new, d_att)),
        scratch_shapes=[pltpu.VMEM((2, n_group, N_HEADS, HEAD_DIM, page), F32),
                        pltpu.VMEM((2, n_group, N_HEADS, HEAD_DIM, page), F32),
                        pltpu.VMEM((2, n_group, N_HEADS, page), F32),
                        pltpu.SemaphoreType.DMA((2, 3)),
                        pltpu.VMEM((n_rows, d_att), BF16), pltpu.VMEM((n_rows, 1), F32),
                        pltpu.VMEM((n_rows, 1), F32), pltpu.VMEM((n_rows, d_att), F32),
                        pltpu.VMEM((N_HEADS, page), F32)],
    )
    return pl.pallas_call(
        functools.partial(_decode_kernel, n_group=n_group, n_steps=n_steps),
        grid_spec=grid_spec,
        out_shape=jax.ShapeDtypeStruct((bd, t_new, d_att), BF16),
        compiler_params=_params(1),
        name="decode",
    )(page_table, q, kn, vn, cqcol, cqrow, us, cache_k, cache_v, cache_lf)


def _conv_kernel(u_ref, state_ref, w_ref, b_ref, lg_ref, lb_ref, o_ref, buf_ref, rot_ref, y_ref):
    tm = u_ref.shape[1]
    buf_ref[0:CONV_HALO, :] = state_ref[0]
    buf_ref[CONV_HALO:, :] = u_ref[0]
    _causal_conv(buf_ref, rot_ref, w_ref, b_ref, y_ref, t0=0, tm=tm, rows=tm)
    o_ref[0] = _ln_swish(y_ref[...], lg_ref[...], lb_ref[...])


def _conv(u, state, w, b, lg, lb):
    bsz, tm, d_conv = u.shape
    assert state.shape == (bsz, CONV_HALO, d_conv)
    tile = pl.BlockSpec((1, tm, d_conv), lambda i: (i, 0, 0))
    vec = _const_spec((1, d_conv))
    return pl.pallas_call(
        _conv_kernel,
        grid=(bsz,),
        in_specs=[tile, pl.BlockSpec((1, CONV_HALO, d_conv), lambda i: (i, 0, 0)), _const_spec(w.shape),
                  vec, vec, vec],
        out_specs=tile,
        out_shape=jax.ShapeDtypeStruct((bsz, tm, d_conv), F32),
        scratch_shapes=_conv_scratch(tm, d_conv),
        compiler_params=_params(1),
        name="conv",
    )(u, state, w, b, lg, lb)


def _outproj_kernel(x_ref, att_ref, cv_ref, wa_ref, wc_ref, g_ref, o_ref):
    m = jnp.dot(att_ref[...].astype(BF16), wa_ref[...], preferred_element_type=F32)
    m = m + jnp.dot(cv_ref[...].astype(BF16), wc_ref[...], preferred_element_type=F32)
    o_ref[...] = x_ref[...] + _rms(m, g_ref[...])


def _outproj(x, att, cv, wa, wc, g, tm):
    m, d = x.shape
    tile = lambda width: pl.BlockSpec((tm, width), lambda i: (i, 0))
    return pl.pallas_call(
        _outproj_kernel,
        grid=(m // tm,),
        in_specs=[tile(d), tile(att.shape[1]), tile(cv.shape[1]), _const_spec(wa.shape), _const_spec(wc.shape),
                  _const_spec((1, d))],
        out_specs=tile(d),
        out_shape=jax.ShapeDtypeStruct((m, d), F32),
        compiler_params=_params(1),
        name="outproj",
    )(x, att, cv, wa, wc, g)


def _pool_kernel(x_ref, halo_ref, g2_ref, g3_ref, w_ref, sc_ref, o_ref, tail_ref, buf_ref, ta_ref, tb_ref,
                 *, pos0, halo_is_state):
    tm = x_ref.shape[1]
    d = x_ref.shape[2]
    grp = d // len(POOL_WINDOWS)
    n = tm + POOL_HALO
    j = pl.program_id(1)
    x = x_ref[0]
    h = _rms(x, g2_ref[...])
    if halo_is_state:
        halo = halo_ref[0]
    else:
        halo = jnp.where(j == 0, 0.0, _rms(halo_ref[0], g2_ref[...]))
    buf_ref[0:POOL_HALO, :] = halo
    buf_ref[POOL_HALO:, :] = h

    @pl.when(j == pl.num_programs(1) - 1)
    def _():
        tail_ref[0] = h[tm - tail_ref.shape[1]:, :]

    ta_ref[8:n, :] = buf_ref[8:n, :] + buf_ref[7:n - 1, :]
    tb_ref[16:n, grp:] = ta_ref[16:n, grp:] + ta_ref[14:n - 2, grp:]
    ta_ref[24:n, 2 * grp:] = tb_ref[24:n, 2 * grp:] + tb_ref[20:n - 4, 2 * grp:]
    s16 = ta_ref[32:n, 3 * grp:] + ta_ref[24:n - 8, 3 * grp:]
    totals = (ta_ref[POOL_HALO:n, 0:grp], tb_ref[POOL_HALO:n, grp:2 * grp],
              ta_ref[POOL_HALO:n, 2 * grp:3 * grp], s16)

    pos = pos0 + j * tm + lax.broadcasted_iota(jnp.int32, (tm, 1), 0)
    outs = []
    for gi, win in enumerate(POOL_WINDOWS):
        cnt = jnp.minimum(pos + 1, win).astype(F32)
        mixed = totals[gi] / cnt - h[:, gi * grp:(gi + 1) * grp]
        outs.append(jnp.dot(mixed.astype(BF16), w_ref[gi], preferred_element_type=F32))
    mix = jnp.concatenate(outs, axis=-1) * sc_ref[...]
    o_ref[0] = x + _rms(mix, g3_ref[...])


def _pool(x, state, g2, g3, w, scale, tm, pos0):
    b, s, d = x.shape
    per_halo = tm // POOL_HALO
    n_tail = min(tm, POOL_TAIL)
    tile = pl.BlockSpec((1, tm, d), lambda i, j: (i, j, 0))
    if state is None:
        halo = pl.BlockSpec((1, POOL_HALO, d), lambda i, j: (i, jnp.maximum(j * per_halo - 1, 0), 0))
    else:
        assert s == tm and state.shape == (b, POOL_HALO, d)
        halo = pl.BlockSpec((1, POOL_HALO, d), lambda i, j: (i, 0, 0))
    vec = _const_spec((1, d))
    buf = pltpu.VMEM((tm + POOL_HALO, d), F32)
    return pl.pallas_call(
        functools.partial(_pool_kernel, pos0=pos0, halo_is_state=state is not None),
        grid=(b, s // tm),
        in_specs=[tile, halo, vec, vec, _const_spec(w.shape), vec],
        out_specs=[tile, pl.BlockSpec((1, n_tail, d), lambda i, j: (i, 0, 0))],
        out_shape=[jax.ShapeDtypeStruct((b, s, d), F32), jax.ShapeDtypeStruct((b, n_tail, d), F32)],
        scratch_shapes=[buf, buf, buf],
        compiler_params=_params(2),
        name="pool",
    )(x, x if state is None else state, g2, g3, w, scale)


def kernel(x_prompt, x_sample, cache_k, cache_v, cache_logf, state_conv, state_pool, page_table,
           norm_g, ffn_w_gate, ffn_w_up, ffn_w_down, mix_w_in, fgate_b, conv_dw_w, conv_dw_b,
           conv_ln_g, conv_ln_b, mix_w_out, pool_w, pool_scale):
    b, s, d = x_prompt.shape
    bd, t_new, _ = x_sample.shape
    depth = norm_g.shape[0]
    d_att = N_HEADS * HEAD_DIM
    d_conv = d - d_att
    page = cache_k.shape[2]
    past_len = page_table.shape[1] * page
    tm_prompt = 512
    m_sample = bd * t_new
    t_pad = 8

    xp = x_prompt.reshape(b * s, d)
    xs = x_sample.reshape(m_sample, d)
    vec = lambda a: a.reshape(1, -1).astype(F32)

    def ffn_both(xp, xs, layer, half):
        wg = ffn_w_gate[layer, half].astype(BF16)
        wu = ffn_w_up[layer, half].astype(BF16)
        wd = ffn_w_down[layer, half].astype(BF16)
        gp, gq = vec(norm_g[layer, 4 * half]), vec(norm_g[layer, 4 * half + 1])
        return (_ffn(xp, gp, gq, wg, wu, wd, 1024), _ffn(xs, gp, gq, wg, wu, wd, m_sample))

    r = jnp.arange(LANES)
    tri_incl = (r[:, None] <= r[None, :]).astype(F32)
    tri_group = tri_incl * (r[:, None] // t_new == r[None, :] // t_new).astype(F32)
    rp = jnp.arange(page)
    suffix_total = jnp.concatenate([(rp[:, None] > rp[None, :]).astype(F32), jnp.ones((page, page), F32)], axis=1)

    outs_p = {k: [] for k in ("k", "v", "lf", "conv", "pool")}
    outs_s = {k: [] for k in ("k", "v", "lf", "conv", "pool")}
    for layer in range(depth):
        g = norm_g[layer]
        xp, xs = ffn_both(xp, xs, layer, 0)
        if layer % 2 == 0:
            e = layer // 2
            w_in = mix_w_in[e]
            o = 3 * d_att + N_HEADS
            w_cat = jnp.concatenate(
                [w_in[:, :3 * d_att], w_in[:, o:], w_in[:, 3 * d_att:o],
                 jnp.zeros((d, LANES - N_HEADS), w_in.dtype)], axis=1).astype(BF16)
            bf = jnp.concatenate([fgate_b[e].astype(F32), jnp.zeros((LANES - N_HEADS,), F32)]).reshape(1, LANES)
            wa = mix_w_out[e, :d_att].astype(BF16)
            wc = mix_w_out[e, d_att:].astype(BF16)
            cw, cb = conv_dw_w[e].astype(F32), vec(conv_dw_b[e])
            lg, lb = vec(conv_ln_g[e]), vec(conv_ln_b[e])
            g2, g3 = vec(g[2]), vec(g[3])

            q, kt, vt, ktb, vb, lft, cv, u_tail, ccol, crow = _inproj(
                xp.reshape(b, s, d), g2, w_cat, bf, tri_incl, (cw, cb, lg, lb), tm_prompt, d_att, d_conv)
            att = _attn(q, ktb, vb, ccol, crow, 256)
            xp = _outproj(xp, att.reshape(b * s, d_att), cv.reshape(b * s, d_conv), wa, wc, g3, tm_prompt)
            outs_p["k"].append(kt.reshape(b, N_HEADS, HEAD_DIM, s).transpose(0, 3, 1, 2))
            outs_p["v"].append(vt.reshape(b, N_HEADS, HEAD_DIM, s).transpose(0, 3, 1, 2))
            outs_p["lf"].append(lft.transpose(0, 2, 1))
            outs_p["conv"].append(u_tail[:, CONV_HALO - (CONV_W - 1):])

            q, k, v, lft, u, ccol, crow = _inproj(
                xs.reshape(1, m_sample, d), g2, w_cat, bf, tri_group, None, m_sample, d_att, d_conv)
            lf = lft.reshape(N_HEADS, m_sample).T
            pad_rows = lambda a: jnp.pad(a.reshape(bd, t_new, d_att), ((0, 0), (0, 16 - t_new), (0, 0)))
            cqcol = ccol.reshape(bd, t_new * N_HEADS, 1)
            cqrow = jnp.pad(crow.reshape(N_HEADS, bd, t_new).transpose(1, 0, 2), ((0, 0), (0, 0), (0, 16 - t_new)))
            att = _decode(page_table, q.reshape(bd, t_new, d_att), pad_rows(k), pad_rows(v), cqcol, cqrow,
                          suffix_total, cache_k[e].transpose(0, 2, 3, 1), cache_v[e].transpose(0, 2, 3, 1),
                          cache_logf[e].astype(F32).transpose(0, 2, 1), 16)
            st = state_conv[e].astype(F32)
            cv = _conv(jnp.pad(u.reshape(bd, t_new, d_conv), ((0, 0), (0, t_pad - t_new), (0, 0))),
                       jnp.pad(st, ((0, 0), (CONV_HALO - (CONV_W - 1), 0), (0, 0))), cw, cb, lg, lb)
            xs = _outproj(xs, att.reshape(m_sample, d_att), cv[:, :t_new].reshape(m_sample, d_conv),
                          wa, wc, g3, m_sample)
            outs_s["k"].append(k.reshape(bd, t_new, N_HEADS, HEAD_DIM))
            outs_s["v"].append(v.reshape(bd, t_new, N_HEADS, HEAD_DIM))
            outs_s["lf"].append(lf.reshape(bd, t_new, N_HEADS))
            outs_s["conv"].append(jnp.concatenate([st, u.reshape(bd, t_new, d_conv)], axis=1)[:, t_new:])
        else:
            o = layer // 2
            g2, g3 = vec(g[2]), vec(g[3])
            pw = pool_w[o].astype(BF16)
            sc = vec(pool_scale[o])
            n_keep = POOL_WINDOWS[-1] - 1
            xp3, tail = _pool(xp.reshape(b, s, d), None, g2, g3, pw, sc, 256, 0)
            xp = xp3.reshape(b * s, d)
            outs_p["pool"].append(tail[:, POOL_TAIL - n_keep:])
            st = state_pool[o].astype(F32)
            xs3, hs = _pool(jnp.pad(xs.reshape(bd, t_new, d), ((0, 0), (0, t_pad - t_new), (0, 0))),
                            jnp.pad(st, ((0, 0), (POOL_HALO - n_keep, 0), (0, 0))), g2, g3, pw, sc, t_pad, past_len)
            xs = xs3[:, :t_new].reshape(m_sample, d)
            outs_s["pool"].append(jnp.concatenate([st, hs[:, :t_new]], axis=1)[:, t_new:])
        xp, xs = ffn_both(xp, xs, layer, 1)

    stack = lambda l: jnp.stack(l)
    return (xp.reshape(b, s, d), xs.reshape(bd, t_new, d),
            stack(outs_p["k"]), stack(outs_p["v"]), stack(outs_p["lf"]), stack(outs_p["conv"]),
            stack(outs_p["pool"]),
            stack(outs_s["k"]), stack(outs_s["v"]), stack(outs_s["lf"]), stack(outs_s["conv"]),
            stack(outs_s["pool"]))
```

```python
import functools

import jax
import jax.numpy as jnp
from jax import lax
from jax.experimental import pallas as pl
from jax.experimental.pallas import tpu as pltpu

F32 = jnp.float32
BF16 = jnp.bfloat16

RMS_EPS = 1e-6
LN_EPS = 1e-5
NEG_INF = -1e30
LOG2E = 1.4426950408889634

LANES = 128
HEAD_DIM = 64
N_HEADS = 8
Q_SCALE = HEAD_DIM ** -0.5 * LOG2E
CONV_W = 31
CONV_HALO = 32
POOL_WINDOWS = (2, 4, 8, 16)
POOL_HALO = 32
POOL_TAIL = 16
VMEM_LIMIT_BYTES = 56 * 1024 * 1024


def _params(n_axes):
    return pltpu.CompilerParams(dimension_semantics=("arbitrary",) * n_axes,
                                vmem_limit_bytes=VMEM_LIMIT_BYTES)


def _const_spec(shape, single_buffer=False):
    zeros = (0,) * len(shape)
    if single_buffer:
        return pl.BlockSpec(shape, lambda *_: zeros, pipeline_mode=pl.Buffered(1))
    return pl.BlockSpec(shape, lambda *_: zeros)


def _rms(x, g):
    ms = jnp.mean(x * x, axis=-1, keepdims=True)
    return x * lax.rsqrt(ms + RMS_EPS) * g


def _sigmoid(x):
    return 1.0 / (1.0 + jnp.exp(-x))


def _log_sigmoid(x):
    return jnp.minimum(x, 0.0) - jnp.log1p(jnp.exp(-jnp.abs(x)))


def _split3(x):
    hi = x.astype(BF16).astype(F32)
    r = x - hi
    mid = r.astype(BF16).astype(F32)
    return hi, mid, r - mid


def _dot3(x, w):
    hi, mid, lo = _split3(x)
    d = lambda a: jnp.dot(a, w, preferred_element_type=F32)
    return d(hi) + d(mid) + d(lo)


WEIGHT_STAGE_ROWS = 128


def _stage_weight(src_hbm, dst_ref, stage_ref, sem):
    rows = WEIGHT_STAGE_ROWS
    n = src_hbm.shape[0] // rows

    def copy(k, slot):
        return pltpu.make_async_copy(src_hbm.at[pl.ds(k * rows, rows)], stage_ref.at[slot], sem.at[slot])

    copy(0, 0).start()

    def step(k, carry):
        slot = lax.rem(k, 2)

        @pl.when(k + 1 < n)
        def _():
            copy(k + 1, 1 - slot).start()

        copy(k, slot).wait()
        dst_ref[pl.ds(pl.multiple_of(k * rows, rows), rows), :] = stage_ref[slot].astype(BF16)
        return carry

    lax.fori_loop(0, n, step, 0)


def _ffn_kernel(xp_ref, xs_ref, gpre_ref, gpost_ref, wg_hbm, wu_hbm, wd_hbm, op_ref, os_ref,
                wg_ref, wu_ref, wd_ref, a_ref, stage_in_ref, stage_out_ref, sem, *, chunk, n_prompt, which):
    i = pl.program_id(0)

    @pl.when(i == 0)
    def _():
        _stage_weight(wg_hbm.at[which], wg_ref, stage_in_ref, sem.at[0])
        _stage_weight(wu_hbm.at[which], wu_ref, stage_in_ref, sem.at[0])
        _stage_weight(wd_hbm.at[which], wd_ref, stage_out_ref, sem.at[1])

    def update(x_ref, o_ref):
        rows = x_ref.shape[0]
        x = x_ref[...]
        h = _rms(x, gpre_ref[...]).astype(BF16)
        d_ff = wg_ref.shape[1]
        for c0 in range(0, d_ff, chunk):
            sl = slice(c0, min(c0 + chunk, d_ff))
            g = jnp.dot(h, wg_ref[:, sl], preferred_element_type=F32)
            u = jnp.dot(h, wu_ref[:, sl], preferred_element_type=F32)
            a_ref[0:rows, sl] = (g * _sigmoid(g) * u).astype(BF16)
        y = jnp.dot(a_ref[0:rows, :], wd_ref[...], preferred_element_type=F32)
        o_ref[...] = x + 0.5 * _rms(y, gpost_ref[...])

    @pl.when(i < n_prompt)
    def _():
        update(xp_ref, op_ref)

    @pl.when(i == n_prompt)
    def _():
        update(xs_ref, os_ref)


def _ffn(xp, xs, g_pre, g_post, wg, wu, wd, which, tm):
    m, d = xp.shape
    d_ff = wg.shape[-1]
    n_prompt = m // tm
    tile = pl.BlockSpec((tm, d), lambda i: (jnp.minimum(i, n_prompt - 1), 0))
    hbm = pl.BlockSpec(memory_space=pl.ANY)
    return pl.pallas_call(
        functools.partial(_ffn_kernel, chunk=512, n_prompt=n_prompt, which=which),
        grid=(n_prompt + 1,),
        in_specs=[tile, _const_spec(xs.shape), _const_spec((1, d)), _const_spec((1, d)), hbm, hbm, hbm],
        out_specs=[tile, _const_spec(xs.shape)],
        out_shape=[jax.ShapeDtypeStruct((m, d), F32), jax.ShapeDtypeStruct(xs.shape, F32)],
        scratch_shapes=[pltpu.VMEM((d, d_ff), BF16), pltpu.VMEM((d, d_ff), BF16), pltpu.VMEM((d_ff, d), BF16),
                        pltpu.VMEM((tm, d_ff), BF16),
                        pltpu.VMEM((2, WEIGHT_STAGE_ROWS, d_ff), F32), pltpu.VMEM((2, WEIGHT_STAGE_ROWS, d), F32),
                        pltpu.SemaphoreType.DMA((2, 2))],
        compiler_params=_params(1),
        name="ffn",
    )(xp, xs, g_pre, g_post, wg, wu, wd)


def _ln_swish(y, g, b):
    mu = jnp.mean(y, axis=-1, keepdims=True)
    d = y - mu
    var = jnp.mean(d * d, axis=-1, keepdims=True)
    yn = d * lax.rsqrt(var + LN_EPS) * g + b
    return yn * _sigmoid(yn)


def _causal_conv(buf_ref, rot_ref, w_ref, b_ref, y_ref, *, t0, tm, rows):
    d_conv = buf_ref.shape[1]
    base = CONV_HALO - (CONV_W - 1)
    n_rot = tm + CONV_HALO - 8
    for r in range(1, 8):
        rot_ref[r - 1, t0:t0 + n_rot, :] = buf_ref[t0 + r:t0 + r + n_rot, :]
    for cb in range(d_conv // LANES):
        ls = slice(cb * LANES, (cb + 1) * LANES)
        for rb in range(tm // rows):
            acc = jnp.zeros((rows, LANES), F32)
            for tap in range(CONV_W):
                a, r = divmod(base + tap, 8)
                r0 = t0 + rb * rows + 8 * a
                src = buf_ref[r0:r0 + rows, ls] if r == 0 else rot_ref[r - 1, r0:r0 + rows, ls]
                acc = acc + w_ref[tap:tap + 1, ls] * src
            y_ref[t0 + rb * rows:t0 + (rb + 1) * rows, ls] = acc + b_ref[:, ls]


def _conv_scratch(tm, d_conv):
    return [pltpu.VMEM((tm + CONV_HALO, d_conv), F32), pltpu.VMEM((7, tm + CONV_HALO - 8, d_conv), F32),
            pltpu.VMEM((tm, d_conv), F32)]


def _inproj_kernel(x_ref, g_ref, w_ref, bf_ref, tri_ref, *refs, d_att, d_conv, prompt, sub):
    if prompt:
        (cw_ref, cb_ref, lg_ref, lb_ref,
         q_ref, kt_ref, vt_ref, ktb_ref, vb_ref, lft_ref, cv_ref, utail_ref, ccol_ref, crow_ref,
         carry_ref, buf_ref, rot_ref, y_ref) = refs
    else:
        q_ref, k_ref, v_ref, lft_ref, u_ref, ccol_ref, crow_ref, carry_ref = refs
    tm = x_ref.shape[1]
    j = pl.program_id(1)

    @pl.when(j == 0)
    def _():
        carry_ref[...] = jnp.zeros_like(carry_ref)
        if prompt:
            buf_ref[0:CONV_HALO, :] = jnp.zeros((CONV_HALO, d_conv), F32)

    if prompt:
        @pl.when(j > 0)
        def _():
            buf_ref[0:CONV_HALO, :] = buf_ref[tm:tm + CONV_HALO, :]

    carry = carry_ref[...]
    tri = tri_ref[...]
    for t0 in range(0, tm, sub):
        rs = slice(t0, t0 + sub)
        h = _rms(x_ref[0, rs, :], g_ref[...]).astype(BF16)
        z = jnp.dot(h, w_ref[...], preferred_element_type=F32)
        k = z[:, d_att:2 * d_att]
        v = z[:, 2 * d_att:3 * d_att]
        o = 3 * d_att
        a = z[:, o:o + d_conv]
        gate = z[:, o + d_conv:o + 2 * d_conv]
        fg = z[:, o + 2 * d_conv:o + 2 * d_conv + LANES]
        q_ref[0, rs, :] = (z[:, :d_att] * Q_SCALE).astype(BF16)
        u = a * _sigmoid(gate)
        if prompt:
            kt = k.T
            kt_ref[0, :, rs] = kt
            vt_ref[0, :, rs] = v.T
            ktb_ref[0, :, rs] = kt.astype(BF16)
            vb_ref[0, rs, :] = v.astype(BF16)
            buf_ref[CONV_HALO + t0:CONV_HALO + t0 + sub, :] = u
            _causal_conv(buf_ref, rot_ref, cw_ref, cb_ref, y_ref, t0=t0, tm=sub, rows=min(sub, 128))
            cv_ref[0, rs, :] = _ln_swish(y_ref[rs, :], lg_ref[...], lb_ref[...]).astype(cv_ref.dtype)
        else:
            k_ref[0, rs, :] = k
            v_ref[0, rs, :] = v
            u_ref[0, rs, :] = u
        lf = _log_sigmoid(fg + bf_ref[...])
        lft = lf.T[:N_HEADS]
        lft_ref[0, :, rs] = lft

        chunks = []
        for c in range(sub // LANES):
            cs = _dot3(lft[:, c * LANES:(c + 1) * LANES], tri) + carry
            chunks.append(cs)
            carry = jnp.broadcast_to(cs[:, LANES - 1:LANES], cs.shape)
        crow = jnp.concatenate(chunks, axis=1) * LOG2E
        crow_ref[0, :, rs] = crow
        cfull = jnp.concatenate([crow, jnp.zeros((LANES - N_HEADS, sub), F32)], axis=0)
        ccol_ref[0, rs, :] = cfull.T[:, :N_HEADS]
    carry_ref[...] = carry

    if prompt:
        @pl.when(j == pl.num_programs(1) - 1)
        def _():
            utail_ref[0] = buf_ref[tm:tm + CONV_HALO, :]


def _inproj(x, g, w, bf, tri, conv, tm, d_att, d_conv):
    b, s, d = x.shape
    prompt = conv is not None
    tok = lambda width: pl.BlockSpec((1, tm, width), lambda i, j: (i, j, 0))
    tr = lambda rows: pl.BlockSpec((1, rows, tm), lambda i, j: (i, 0, j))
    shp = lambda width, dt: jax.ShapeDtypeStruct((b, s, width), dt)
    sht = lambda rows, dt: jax.ShapeDtypeStruct((b, rows, s), dt)
    in_specs = [tok(d), _const_spec((1, d)), _const_spec(w.shape, True), _const_spec((1, LANES)),
                _const_spec((LANES, LANES))]
    scratch = [pltpu.VMEM((N_HEADS, LANES), F32)]
    if prompt:
        in_specs += [_const_spec(conv[0].shape)] + [_const_spec((1, d_conv))] * 3
        out_specs = [tok(d_att), tr(d_att), tr(d_att), tr(d_att), tok(d_att), tr(N_HEADS), tok(d_conv),
                     pl.BlockSpec((1, CONV_HALO, d_conv), lambda i, j: (i, 0, 0))]
        out_shape = [shp(d_att, BF16), sht(d_att, F32), sht(d_att, F32), sht(d_att, BF16), shp(d_att, BF16),
                     sht(N_HEADS, F32), shp(d_conv, BF16), jax.ShapeDtypeStruct((b, CONV_HALO, d_conv), F32)]
        scratch += _conv_scratch(tm, d_conv)
    else:
        out_specs = [tok(d_att), tok(d_att), tok(d_att), tr(N_HEADS), tok(d_conv)]
        out_shape = [shp(d_att, BF16), shp(d_att, F32), shp(d_att, F32), sht(N_HEADS, F32), shp(d_conv, F32)]
    return pl.pallas_call(
        functools.partial(_inproj_kernel, d_att=d_att, d_conv=d_conv, prompt=prompt, sub=min(tm, 256)),
        grid=(b, s // tm),
        in_specs=in_specs,
        out_specs=out_specs + [tok(N_HEADS), tr(N_HEADS)],
        out_shape=out_shape + [shp(N_HEADS, F32), sht(N_HEADS, F32)],
        scratch_shapes=scratch,
        compiler_params=_params(2),
        name="inproj",
    )(x, g, w, bf, tri, *(conv or ()))


def _attn_kernel(q_ref, kt_ref, v_ref, ccol_ref, crow_ref, o_ref, *, tq):
    s_len = q_ref.shape[1]
    pair = pl.program_id(1)
    row = lax.broadcasted_iota(jnp.int32, (tq, tq), 0)
    col = lax.broadcasted_iota(jnp.int32, (tq, tq), 1)
    causal = col <= row
    low = lax.broadcasted_iota(jnp.int32, (tq, LANES), 1) < HEAD_DIM
    ccol = ccol_ref[0]
    head_lane = lax.broadcasted_iota(jnp.int32, ccol.shape, 1)
    cq_all, ck_all = [], []
    for hh in range(2):
        head = 2 * pair + hh
        cq_all.append(jnp.sum(jnp.where(head_lane == head, ccol, 0.0), axis=-1, keepdims=True))
        ck_all.append(crow_ref[0, pl.ds(head, 1), :])

    def scores(i, hh):
        k0 = i * tq
        q2 = q_ref[0, k0:k0 + tq, :]
        zero = jnp.zeros_like(q2)
        qh = jnp.where(low, q2, zero) if hh == 0 else jnp.where(low, zero, q2)
        s_diag = jnp.dot(qh, kt_ref[0, :, k0:k0 + tq], preferred_element_type=F32)
        s_off = jnp.dot(qh, kt_ref[0, :, :k0], preferred_element_type=F32) if i > 0 else None
        return s_diag, s_off

    def finish(i, hh, s_diag, s_off):
        k0 = i * tq
        cq = cq_all[hh][k0:k0 + tq]
        ck = ck_all[hh]
        s_diag = jnp.where(causal, s_diag - ck[:, k0:k0 + tq], NEG_INF)
        m = jnp.max(s_diag, axis=-1, keepdims=True)
        if i > 0:
            s_off = s_off - ck[:, :k0]
            m = jnp.maximum(m, jnp.max(s_off, axis=-1, keepdims=True))
        shift = cq - (m + cq)
        p_diag = jnp.exp2(s_diag + shift)
        l = jnp.sum(p_diag, axis=-1, keepdims=True)
        pv = jnp.dot(p_diag.astype(BF16), v_ref[0, k0:k0 + tq, :], preferred_element_type=F32)
        if i > 0:
            p_off = jnp.exp2(s_off + shift)
            l = l + jnp.sum(p_off, axis=-1, keepdims=True)
            pv = pv + jnp.dot(p_off.astype(BF16), v_ref[0, :k0, :], preferred_element_type=F32)
        return pv * (1.0 / l)

    blocks = [(i, hh) for i in range(s_len // tq) for hh in range(2)]
    nxt = scores(*blocks[0])
    halves = []
    for n, (i, hh) in enumerate(blocks):
        cur = nxt
        if n + 1 < len(blocks):
            nxt = scores(*blocks[n + 1])
        halves.append(finish(i, hh, *cur))
        if hh == 1:
            o_ref[0, i * tq:(i + 1) * tq, :] = jnp.where(low, halves[0], halves[1]).astype(BF16)
            halves = []


def _attn(q, ktb, vb, ccol, crow, tq):
    b, s, d_att = q.shape
    cols = pl.BlockSpec((1, s, LANES), lambda i, j: (i, 0, j))
    return pl.pallas_call(
        functools.partial(_attn_kernel, tq=tq),
        grid=(b, d_att // LANES),
        in_specs=[cols, pl.BlockSpec((1, LANES, s), lambda i, j: (i, j, 0)), cols,
                  pl.BlockSpec((1, s, N_HEADS), lambda i, j: (i, 0, 0)),
                  pl.BlockSpec((1, N_HEADS, s), lambda i, j: (i, 0, 0))],
        out_specs=cols,
        out_shape=jax.ShapeDtypeStruct((b, s, d_att), BF16),
        compiler_params=_params(2),
        name="attn",
    )(q, ktb, vb, ccol, crow)


def _decode_kernel(pt_ref, q_ref, kn_ref, vn_ref, cqcol_ref, cqrow_ref, us_ref, kc_ref, vc_ref, lc_ref, o_ref,
                   kbuf, vbuf, lbuf, sem, qbd_ref, m_ref, l_ref, acc_ref, carry_ref, *, n_group, n_steps):
    t = pl.program_id(0)
    js = lax.rem(t, n_steps)
    slot = lax.rem(t, 2)
    n_pages = pt_ref.shape[1]
    t_new = q_ref.shape[1]
    d_att = q_ref.shape[2]
    page = kbuf.shape[-1]
    n_rows = t_new * N_HEADS
    nt_dims = (((1,), (1,)), ((), ()))
    cqcol = cqcol_ref[0]

    def page_copies(step, to_slot, pages):
        out = []
        for g in range(n_group):
            p = pages(step, g)
            out.append(pltpu.make_async_copy(kc_ref.at[p], kbuf.at[to_slot, g], sem.at[to_slot, 0]))
            out.append(pltpu.make_async_copy(vc_ref.at[p], vbuf.at[to_slot, g], sem.at[to_slot, 1]))
            out.append(pltpu.make_async_copy(lc_ref.at[p], lbuf.at[to_slot, g], sem.at[to_slot, 2]))
        return out

    def table_page(step, g):
        return pt_ref[lax.div(step, n_steps), n_pages - 1 - (lax.rem(step, n_steps) * n_group + g)]

    @pl.when(t == 0)
    def _():
        for c in page_copies(t, slot, table_page):
            c.start()

    @pl.when(t + 1 < pl.num_programs(0))
    def _():
        for c in page_copies(t + 1, 1 - slot, table_page):
            c.start()

    @pl.when(js == 0)
    def _():
        q = q_ref[0].astype(F32)
        sub = lax.broadcasted_iota(jnp.int32, (N_HEADS, d_att), 0)
        lane = lax.broadcasted_iota(jnp.int32, (N_HEADS, d_att), 1)
        own = (lane // HEAD_DIM) == sub
        rows = [jnp.where(own, jnp.broadcast_to(q[i:i + 1, :], (N_HEADS, d_att)), 0.0) for i in range(t_new)]
        qbd = jnp.concatenate(rows, axis=0).astype(BF16)
        qbd_ref[...] = qbd
        s = lax.dot_general(qbd, kn_ref[0].astype(BF16), nt_dims, preferred_element_type=F32)
        cqk = jnp.concatenate([cqrow_ref[0]] * t_new, axis=0)
        r = lax.broadcasted_iota(jnp.int32, s.shape, 0)
        c = lax.broadcasted_iota(jnp.int32, s.shape, 1)
        s = jnp.where(c <= r // N_HEADS, s + (cqcol - cqk), NEG_INF)
        m = jnp.max(s, axis=-1, keepdims=True)
        pr = jnp.exp2(s - m)
        m_ref[...] = m
        l_ref[...] = jnp.sum(pr, axis=-1, keepdims=True)
        acc_ref[...] = jnp.dot(pr.astype(BF16), vn_ref[0].astype(BF16), preferred_element_type=F32)
        carry_ref[...] = jnp.zeros_like(carry_ref)

    for c in page_copies(t, slot, lambda step, g: 0):
        c.wait()

    qbd = qbd_ref[...]
    carry = carry_ref[...]
    us = us_ref[...]
    s_parts = []
    for g in range(n_group):
        r = _dot3(lbuf[slot, g], us)
        suffix = (r[:, :page] + carry) * LOG2E
        carry = carry + r[:, page:]
        bias = jnp.concatenate([suffix] * t_new, axis=0) + cqcol
        kt = kbuf[slot, g].reshape(d_att, page).astype(BF16)
        s_parts.append(jnp.dot(qbd, kt, preferred_element_type=F32) + bias)
    carry_ref[...] = carry
    s = jnp.concatenate(s_parts, axis=1)
    m = m_ref[...]
    m_new = jnp.maximum(m, jnp.max(s, axis=-1, keepdims=True))
    alpha = jnp.exp2(m - m_new)
    pr = jnp.exp2(s - m_new)
    m_ref[...] = m_new
    l_ref[...] = alpha * l_ref[...] + jnp.sum(pr, axis=-1, keepdims=True)
    prb = pr.astype(BF16)
    acc = alpha * acc_ref[...]
    for g in range(n_group):
        vt = vbuf[slot, g].reshape(d_att, page).astype(BF16)
        acc = acc + lax.dot_general(prb[:, g * page:(g + 1) * page], vt, nt_dims, preferred_element_type=F32)
    acc_ref[...] = acc

    @pl.when(js == n_steps - 1)
    def _():
        o = acc * (1.0 / l_ref[...])
        sub = lax.broadcasted_iota(jnp.int32, (N_HEADS, d_att), 0)
        lane = lax.broadcasted_iota(jnp.int32, (N_HEADS, d_att), 1)
        own = (lane // HEAD_DIM) == sub
        rows = [jnp.sum(jnp.where(own, o[i * N_HEADS:(i + 1) * N_HEADS], 0.0), axis=0, keepdims=True)
                for i in range(t_new)]
        o_ref[0] = jnp.concatenate(rows, axis=0).astype(o_ref.dtype)


def _decode(page_table, q, kn, vn, cqcol, cqrow, us, cache_k, cache_v, cache_lf, n_group):
    bd, t_new, d_att = q.shape
    n_pages = page_table.shape[1]
    page = cache_k.shape[3]
    n_rows = t_new * N_HEADS
    n_steps = n_pages // n_group
    per_b = lambda shape: pl.BlockSpec((1,) + shape, lambda t, pt: (t // n_steps,) + (0,) * len(shape))
    hbm = pl.BlockSpec(memory_space=pl.ANY)
    grid_spec = pltpu.PrefetchScalarGridSpec(
        num_scalar_prefetch=1,
        grid=(bd * n_steps,),
        in_specs=[per_b((t_new, d_att)), per_b(kn.shape[1:]), per_b(vn.shape[1:]), per_b((n_rows, 1)),
                  per_b(cqrow.shape[1:]), pl.BlockSpec(us.shape, lambda t, pt: (0, 0)), hbm, hbm, hbm],
        out_specs=per_b((t_new, d_att)),
        scratch_shapes=[pltpu.VMEM((2, n_group, N_HEADS, HEAD_DIM, page), F32),
                        pltpu.VMEM((2, n_group, N_HEADS, HEAD_DIM, page), F32),
                        pltpu.VMEM((2, n_group, N_HEADS, page), F32),
                        pltpu.SemaphoreType.DMA((2, 3)),
                        pltpu.VMEM((n_rows, d_att), BF16), pltpu.VMEM((n_rows, 1), F32),
                        pltpu.VMEM((n_rows, 1), F32), pltpu.VMEM((n_rows, d_att), F32),
                        pltpu.VMEM((N_HEADS, page), F32)],
    )
    return pl.pallas_call(
        functools.partial(_decode_kernel, n_group=n_group, n_steps=n_steps),
        grid_spec=grid_spec,
        out_shape=jax.ShapeDtypeStruct((bd, t_new, d_att), BF16),
        compiler_params=_params(1),
        name="decode",
    )(page_table, q, kn, vn, cqcol, cqrow, us, cache_k, cache_v, cache_lf)


def _conv_kernel(u_ref, state_ref, w_ref, b_ref, lg_ref, lb_ref, o_ref, buf_ref, rot_ref, y_ref):
    tm = u_ref.shape[1]
    buf_ref[0:CONV_HALO, :] = state_ref[0]
    buf_ref[CONV_HALO:, :] = u_ref[0]
    _causal_conv(buf_ref, rot_ref, w_ref, b_ref, y_ref, t0=0, tm=tm, rows=tm)
    o_ref[0] = _ln_swish(y_ref[...], lg_ref[...], lb_ref[...])


def _conv(u, state, w, b, lg, lb):
    bsz, tm, d_conv = u.shape
    assert state.shape == (bsz, CONV_HALO, d_conv)
    tile = pl.BlockSpec((1, tm, d_conv), lambda i: (i, 0, 0))
    vec = _const_spec((1, d_conv))
    return pl.pallas_call(
        _conv_kernel,
        grid=(bsz,),
        in_specs=[tile, pl.BlockSpec((1, CONV_HALO, d_conv), lambda i: (i, 0, 0)), _const_spec(w.shape),
                  vec, vec, vec],
        out_specs=tile,
        out_shape=jax.ShapeDtypeStruct((bsz, tm, d_conv), F32),
        scratch_shapes=_conv_scratch(tm, d_conv),
        compiler_params=_params(1),
        name="conv",
    )(u, state, w, b, lg, lb)


def _outproj_kernel(x_ref, att_ref, cv_ref, wa_ref, wc_ref, g_ref, o_ref):
    m = jnp.dot(att_ref[...].astype(BF16), wa_ref[...], preferred_element_type=F32)
    m = m + jnp.dot(cv_ref[...].astype(BF16), wc_ref[...], preferred_element_type=F32)
    o_ref[...] = x_ref[...] + _rms(m, g_ref[...])


def _outproj(x, att, cv, wa, wc, g, tm):
    m, d = x.shape
    tile = lambda width: pl.BlockSpec((tm, width), lambda i: (i, 0))
    return pl.pallas_call(
        _outproj_kernel,
        grid=(m // tm,),
        in_specs=[tile(d), tile(att.shape[1]), tile(cv.shape[1]), _const_spec(wa.shape), _const_spec(wc.shape),
                  _const_spec((1, d))],
        out_specs=tile(d),
        out_shape=jax.ShapeDtypeStruct((m, d), F32),
        compiler_params=_params(1),
        name="outproj",
    )(x, att, cv, wa, wc, g)


def _pool_kernel(x_ref, halo_ref, g2_ref, g3_ref, w_ref, sc_ref, o_ref, tail_ref, buf_ref, ta_ref, tb_ref,
                 *, pos0, halo_is_state):
    tm = x_ref.shape[1]
    d = x_ref.shape[2]
    grp = d // len(POOL_WINDOWS)
    n = tm + POOL_HALO
    j = pl.program_id(1)
    x = x_ref[0]
    h = _rms(x, g2_ref[...])
    if halo_is_state:
        halo = halo_ref[0]
    else:
        halo = jnp.where(j == 0, 0.0, _rms(halo_ref[0], g2_ref[...]))
    buf_ref[0:POOL_HALO, :] = halo
    buf_ref[POOL_HALO:, :] = h

    @pl.when(j == pl.num_programs(1) - 1)
    def _():
        tail_ref[0] = h[tm - tail_ref.shape[1]:, :]

    ta_ref[8:n, :] = buf_ref[8:n, :] + buf_ref[7:n - 1, :]
    tb_ref[16:n, grp:] = ta_ref[16:n, grp:] + ta_ref[14:n - 2, grp:]
    ta_ref[24:n, 2 * grp:] = tb_ref[24:n, 2 * grp:] + tb_ref[20:n - 4, 2 * grp:]
    s16 = ta_ref[32:n, 3 * grp:] + ta_ref[24:n - 8, 3 * grp:]
    totals = (ta_ref[POOL_HALO:n, 0:grp], tb_ref[POOL_HALO:n, grp:2 * grp],
              ta_ref[POOL_HALO:n, 2 * grp:3 * grp], s16)

    pos = pos0 + j * tm + lax.broadcasted_iota(jnp.int32, (tm, 1), 0)
    outs = []
    for gi, win in enumerate(POOL_WINDOWS):
        cnt = jnp.minimum(pos + 1, win).astype(F32)
        mixed = totals[gi] / cnt - h[:, gi * grp:(gi + 1) * grp]
        outs.append(jnp.dot(mixed.astype(BF16), w_ref[gi], preferred_element_type=F32))
    mix = jnp.concatenate(outs, axis=-1) * sc_ref[...]
    o_ref[0] = x + _rms(mix, g3_ref[...])


def _pool(x, state, g2, g3, w, scale, tm, pos0):
    b, s, d = x.shape
    per_halo = tm // POOL_HALO
    n_tail = min(tm, POOL_TAIL)
    tile = pl.BlockSpec((1, tm, d), lambda i, j: (i, j, 0))
    if state is None:
        halo = pl.BlockSpec((1, POOL_HALO, d), lambda i, j: (i, jnp.maximum(j * per_halo - 1, 0), 0))
    else:
        assert s == tm and state.shape == (b, POOL_HALO, d)
        halo = pl.BlockSpec((1, POOL_HALO, d), lambda i, j: (i, 0, 0))
    vec = _const_spec((1, d))
    buf = pltpu.VMEM((tm + POOL_HALO, d), F32)
    return pl.pallas_call(
        functools.partial(_pool_kernel, pos0=pos0, halo_is_state=state is not None),
        grid=(b, s // tm),
        in_specs=[tile, halo, vec, vec, _const_spec(w.shape), vec],
        out_specs=[tile, pl.BlockSpec((1, n_tail, d), lambda i, j: (i, 0, 0))],
        out_shape=[jax.ShapeDtypeStruct((b, s, d), F32), jax.ShapeDtypeStruct((b, n_tail, d), F32)],
        scratch_shapes=[buf, buf, buf],
        compiler_params=_params(2),
        name="pool",
    )(x, x if state is None else state, g2, g3, w, scale)


def kernel(x_prompt, x_sample, cache_k, cache_v, cache_logf, state_conv, state_pool, page_table,
           norm_g, ffn_w_gate, ffn_w_up, ffn_w_down, mix_w_in, fgate_b, conv_dw_w, conv_dw_b,
           conv_ln_g, conv_ln_b, mix_w_out, pool_w, pool_scale):
    b, s, d = x_prompt.shape
    bd, t_new, _ = x_sample.shape
    depth = norm_g.shape[0]
    d_att = N_HEADS * HEAD_DIM
    d_conv = d - d_att
    page = cache_k.shape[2]
    past_len = page_table.shape[1] * page
    tm_prompt = 512
    m_sample = bd * t_new
    t_pad = 8

    xp = x_prompt.reshape(b * s, d)
    xs = x_sample.reshape(m_sample, d)
    vec = lambda a: a.reshape(1, -1).astype(F32)

    def ffn_both(xp, xs, layer, half):
        gp, gq = vec(norm_g[layer, 4 * half]), vec(norm_g[layer, 4 * half + 1])
        return _ffn(xp, xs, gp, gq, ffn_w_gate.astype(F32), ffn_w_up.astype(F32), ffn_w_down.astype(F32),
                    (layer, half), 1024)

    r = jnp.arange(LANES)
    tri_incl = (r[:, None] <= r[None, :]).astype(F32)
    tri_group = tri_incl * (r[:, None] // t_new == r[None, :] // t_new).astype(F32)
    rp = jnp.arange(page)
    suffix_total = jnp.concatenate([(rp[:, None] > rp[None, :]).astype(F32), jnp.ones((page, page), F32)], axis=1)

    outs_p = {k: [] for k in ("k", "v", "lf", "conv", "pool")}
    outs_s = {k: [] for k in ("k", "v", "lf", "conv", "pool")}
    for layer in range(depth):
        g = norm_g[layer]
        xp, xs = ffn_both(xp, xs, layer, 0)
        if layer % 2 == 0:
            e = layer // 2
            w_in = mix_w_in[e]
            o = 3 * d_att + N_HEADS
            w_cat = jnp.concatenate(
                [w_in[:, :3 * d_att], w_in[:, o:], w_in[:, 3 * d_att:o],
                 jnp.zeros((d, LANES - N_HEADS), w_in.dtype)], axis=1).astype(BF16)
            bf = jnp.concatenate([fgate_b[e].astype(F32), jnp.zeros((LANES - N_HEADS,), F32)]).reshape(1, LANES)
            wa = mix_w_out[e, :d_att].astype(BF16)
            wc = mix_w_out[e, d_att:].astype(BF16)
            cw, cb = conv_dw_w[e].astype(F32), vec(conv_dw_b[e])
            lg, lb = vec(conv_ln_g[e]), vec(conv_ln_b[e])
            g2, g3 = vec(g[2]), vec(g[3])

            q, kt, vt, ktb, vb, lft, cv, u_tail, ccol, crow = _inproj(
                xp.reshape(b, s, d), g2, w_cat, bf, tri_incl, (cw, cb, lg, lb), tm_prompt, d_att, d_conv)
            att = _attn(q, ktb, vb, ccol, crow, 256)
            xp = _outproj(xp, att.reshape(b * s, d_att), cv.reshape(b * s, d_conv), wa, wc, g3, tm_prompt)
            outs_p["k"].append(kt.reshape(b, N_HEADS, HEAD_DIM, s).transpose(0, 3, 1, 2))
            outs_p["v"].append(vt.reshape(b, N_HEADS, HEAD_DIM, s).transpose(0, 3, 1, 2))
            outs_p["lf"].append(lft.transpose(0, 2, 1))
            outs_p["conv"].append(u_tail[:, CONV_HALO - (CONV_W - 1):])

            q, k, v, lft, u, ccol, crow = _inproj(
                xs.reshape(1, m_sample, d), g2, w_cat, bf, tri_group, None, m_sample, d_att, d_conv)
            lf = lft.reshape(N_HEADS, m_sample).T
            pad_rows = lambda a: jnp.pad(a.reshape(bd, t_new, d_att), ((0, 0), (0, 16 - t_new), (0, 0)))
            cqcol = ccol.reshape(bd, t_new * N_HEADS, 1)
            cqrow = jnp.pad(crow.reshape(N_HEADS, bd, t_new).transpose(1, 0, 2), ((0, 0), (0, 0), (0, 16 - t_new)))
            att = _decode(page_table, q.reshape(bd, t_new, d_att), pad_rows(k), pad_rows(v), cqcol, cqrow,
                          suffix_total, cache_k[e].transpose(0, 2, 3, 1), cache_v[e].transpose(0, 2, 3, 1),
                          cache_logf[e].astype(F32).transpose(0, 2, 1), 16)
            st = state_conv[e].astype(F32)
            cv = _conv(jnp.pad(u.reshape(bd, t_new, d_conv), ((0, 0), (0, t_pad - t_new), (0, 0))),
                       jnp.pad(st, ((0, 0), (CONV_HALO - (CONV_W - 1), 0), (0, 0))), cw, cb, lg, lb)
            xs = _outproj(xs, att.reshape(m_sample, d_att), cv[:, :t_new].reshape(m_sample, d_conv),
                          wa, wc, g3, m_sample)
            outs_s["k"].append(k.reshape(bd, t_new, N_HEADS, HEAD_DIM))
            outs_s["v"].append(v.reshape(bd, t_new, N_HEADS, HEAD_DIM))
            outs_s["lf"].append(lf.reshape(bd, t_new, N_HEADS))
            outs_s["conv"].append(jnp.concatenate([st, u.reshape(bd, t_new, d_conv)], axis=1)[:, t_new:])
        else:
            o = layer // 2
            g2, g3 = vec(g[2]), vec(g[3])
            pw = pool_w[o].astype(BF16)
            sc = vec(pool_scale[o])
            n_keep = POOL_WINDOWS[-1] - 1
            xp3, tail = _pool(xp.reshape(b, s, d), None, g2, g3, pw, sc, 512, 0)
            xp = xp3.reshape(b * s, d)
            outs_p["pool"].append(tail[:, POOL_TAIL - n_keep:])
            st = state_pool[o].astype(F32)
            xs3, hs = _pool(jnp.pad(xs.reshape(bd, t_new, d), ((0, 0), (0, t_pad - t_new), (0, 0))),
                            jnp.pad(st, ((0, 0), (POOL_HALO - n_keep, 0), (0, 0))), g2, g3, pw, sc, t_pad, past_len)
            xs = xs3[:, :t_new].reshape(m_sample, d)
            outs_s["pool"].append(jnp.concatenate([st, hs[:, :t_new]], axis=1)[:, t_new:])
        xp, xs = ffn_both(xp, xs, layer, 1)

    stack = lambda l: jnp.stack(l)
    return (xp.reshape(b, s, d), xs.reshape(bd, t_new, d),
            stack(outs_p["k"]), stack(outs_p["v"]), stack(outs_p["lf"]), stack(outs_p["conv"]),
            stack(outs_p["pool"]),
            stack(outs_s["k"]), stack(outs_s["v"]), stack(outs_s["lf"]), stack(outs_s["conv"]),
            stack(outs_s["pool"]))
```

```python
import functools

import jax
import jax.numpy as jnp
from jax import lax
from jax.experimental import pallas as pl
from jax.experimental.pallas import tpu as pltpu

F32 = jnp.float32
BF16 = jnp.bfloat16

RMS_EPS = 1e-6
LN_EPS = 1e-5
NEG_INF = -1e30
LOG2E = 1.4426950408889634

LANES = 128
HEAD_DIM = 64
N_HEADS = 8
Q_SCALE = HEAD_DIM ** -0.5 * LOG2E
CONV_W = 31
CONV_HALO = 32
POOL_WINDOWS = (2, 4, 8, 16)
POOL_HALO = 32
POOL_TAIL = 16
VMEM_LIMIT_BYTES = 56 * 1024 * 1024
FFN_VMEM_LIMIT_BYTES = 61 * 1024 * 1024


def _params(n_axes, vmem_limit_bytes=VMEM_LIMIT_BYTES):
    return pltpu.CompilerParams(dimension_semantics=("arbitrary",) * n_axes,
                                vmem_limit_bytes=vmem_limit_bytes)


def _const_spec(shape, single_buffer=False):
    zeros = (0,) * len(shape)
    if single_buffer:
        return pl.BlockSpec(shape, lambda *_: zeros, pipeline_mode=pl.Buffered(1))
    return pl.BlockSpec(shape, lambda *_: zeros)


def _rms(x, g):
    ms = jnp.mean(x * x, axis=-1, keepdims=True)
    return x * lax.rsqrt(ms + RMS_EPS) * g


def _sigmoid(x):
    return 1.0 / (1.0 + jnp.exp(-x))


def _log_sigmoid(x):
    return jnp.minimum(x, 0.0) - jnp.log1p(jnp.exp(-jnp.abs(x)))


def _split3(x):
    hi = x.astype(BF16).astype(F32)
    r = x - hi
    mid = r.astype(BF16).astype(F32)
    return hi, mid, r - mid


def _dot3(x, w):
    hi, mid, lo = _split3(x)
    d = lambda a: jnp.dot(a, w, preferred_element_type=F32)
    return d(hi) + d(mid) + d(lo)


WEIGHT_STAGE_ROWS = 64
WEIGHT_STAGE_SLOTS = 4
WEIGHT_STAGE_AHEAD = 3


def _stage_weights(jobs):
    rows = WEIGHT_STAGE_ROWS
    chunks = []
    used = {}
    for src, dst, stage, sems in jobs:
        for k in range(src.shape[0] // rows):
            slot = used.get(id(stage), 0) % WEIGHT_STAGE_SLOTS
            used[id(stage)] = used.get(id(stage), 0) + 1
            copy = pltpu.make_async_copy(src.at[pl.ds(k * rows, rows)], stage.at[slot], sems.at[slot])
            chunks.append((copy, dst, k, stage, slot))
    for copy, *_ in chunks[:WEIGHT_STAGE_AHEAD]:
        copy.start()
    for n, (copy, dst, k, stage, slot) in enumerate(chunks):
        copy.wait()
        dst[k * rows:(k + 1) * rows, :] = stage[slot].astype(BF16)
        if n + WEIGHT_STAGE_AHEAD < len(chunks):
            chunks[n + WEIGHT_STAGE_AHEAD][0].start()


def _ffn_kernel(*refs, chunk, n_prompt, which, mix):
    xp_ref, xs_ref, gpre_ref, gpost_ref, wg_hbm, wu_hbm, wd_hbm = refs[:7]
    refs = refs[7:]
    if mix:
        attp_ref, cvp_ref, atts_ref, cvs_ref, wa_ref, wc_ref, gmix_ref = refs[:7]
        refs = refs[7:]
    op_ref, os_ref, wg_ref, wu_ref, wd_ref, a_ref, stage_in_ref, stage_out_ref, sem = refs
    i = pl.program_id(0)

    @pl.when(i == 0)
    def _():
        _stage_weights([(wg_hbm.at[which], wg_ref, stage_in_ref, sem.at[0]),
                        (wu_hbm.at[which], wu_ref, stage_in_ref, sem.at[0]),
                        (wd_hbm.at[which], wd_ref, stage_out_ref, sem.at[1])])

    def update(x_ref, o_ref, att_ref=None, cv_ref=None):
        rows = x_ref.shape[0]
        x = x_ref[...]
        if mix:
            mo = jnp.dot(att_ref[...].astype(BF16), wa_ref[...], preferred_element_type=F32)
            mo = mo + jnp.dot(cv_ref[...].astype(BF16), wc_ref[...], preferred_element_type=F32)
            x = x + _rms(mo, gmix_ref[...])
        h = _rms(x, gpre_ref[...]).astype(BF16)
        d_ff = wg_ref.shape[1]
        for c0 in range(0, d_ff, chunk):
            sl = slice(c0, min(c0 + chunk, d_ff))
            g = jnp.dot(h, wg_ref[:, sl], preferred_element_type=F32)
            u = jnp.dot(h, wu_ref[:, sl], preferred_element_type=F32)
            a_ref[0:rows, sl] = (g * _sigmoid(g) * u).astype(BF16)
        y = jnp.dot(a_ref[0:rows, :], wd_ref[...], preferred_element_type=F32)
        o_ref[...] = x + 0.5 * _rms(y, gpost_ref[...])

    @pl.when(i < n_prompt)
    def _():
        update(xp_ref, op_ref, *((attp_ref, cvp_ref) if mix else ()))

    @pl.when(i == n_prompt)
    def _():
        update(xs_ref, os_ref, *((atts_ref, cvs_ref) if mix else ()))


def _ffn(xp, xs, g_pre, g_post, wg, wu, wd, which, tm, mix=None):
    m, d = xp.shape
    d_ff = wg.shape[-1]
    n_prompt = m // tm
    tile = lambda width: pl.BlockSpec((tm, width), lambda i: (jnp.minimum(i, n_prompt - 1), 0))
    hbm = pl.BlockSpec(memory_space=pl.ANY)
    in_specs = [tile(d), _const_spec(xs.shape), _const_spec((1, d)), _const_spec((1, d)), hbm, hbm, hbm]
    if mix is not None:
        att_p, cv_p, att_s, cv_s, w_att, w_cv, _ = mix
        in_specs += [tile(att_p.shape[1]), tile(cv_p.shape[1]), _const_spec(att_s.shape), _const_spec(cv_s.shape),
                     _const_spec(w_att.shape, True), _const_spec(w_cv.shape, True), _const_spec((1, d))]
    return pl.pallas_call(
        functools.partial(_ffn_kernel, chunk=512, n_prompt=n_prompt, which=which, mix=mix is not None),
        grid=(n_prompt + 1,),
        in_specs=in_specs,
        out_specs=[tile(d), _const_spec(xs.shape)],
        out_shape=[jax.ShapeDtypeStruct((m, d), F32), jax.ShapeDtypeStruct(xs.shape, F32)],
        scratch_shapes=[pltpu.VMEM((d, d_ff), BF16), pltpu.VMEM((d, d_ff), BF16), pltpu.VMEM((d_ff, d), BF16),
                        pltpu.VMEM((tm, d_ff), BF16),
                        pltpu.VMEM((WEIGHT_STAGE_SLOTS, WEIGHT_STAGE_ROWS, d_ff), F32),
                        pltpu.VMEM((WEIGHT_STAGE_SLOTS, WEIGHT_STAGE_ROWS, d), F32),
                        pltpu.SemaphoreType.DMA((2, WEIGHT_STAGE_SLOTS))],
        compiler_params=_params(1, FFN_VMEM_LIMIT_BYTES),
        name="ffn",
    )(xp, xs, g_pre, g_post, wg, wu, wd, *(mix or ()))


def _ln_swish(y, g, b):
    mu = jnp.mean(y, axis=-1, keepdims=True)
    d = y - mu
    var = jnp.mean(d * d, axis=-1, keepdims=True)
    yn = d * lax.rsqrt(var + LN_EPS) * g + b
    return yn * _sigmoid(yn)


def _causal_conv(buf_ref, rot_ref, w_ref, b_ref, y_ref, *, t0, tm, rows):
    d_conv = buf_ref.shape[1]
    base = CONV_HALO - (CONV_W - 1)
    n_rot = tm + CONV_HALO - 8
    for r in range(1, 8):
        rot_ref[r - 1, t0:t0 + n_rot, :] = buf_ref[t0 + r:t0 + r + n_rot, :]
    for cb in range(d_conv // LANES):
        ls = slice(cb * LANES, (cb + 1) * LANES)
        for rb in range(tm // rows):
            acc = jnp.zeros((rows, LANES), F32)
            for tap in range(CONV_W):
                a, r = divmod(base + tap, 8)
                r0 = t0 + rb * rows + 8 * a
                src = buf_ref[r0:r0 + rows, ls] if r == 0 else rot_ref[r - 1, r0:r0 + rows, ls]
                acc = acc + w_ref[tap:tap + 1, ls] * src
            y_ref[t0 + rb * rows:t0 + (rb + 1) * rows, ls] = acc + b_ref[:, ls]


def _conv_scratch(tm, d_conv):
    return [pltpu.VMEM((tm + CONV_HALO, d_conv), F32), pltpu.VMEM((7, tm + CONV_HALO - 8, d_conv), F32),
            pltpu.VMEM((tm, d_conv), F32)]


def _inproj_kernel(x_ref, g_ref, w_ref, bf_ref, tri_ref, *refs, d_att, d_conv, prompt, sub):
    if prompt:
        (cw_ref, cb_ref, lg_ref, lb_ref,
         q_ref, kt_ref, vt_ref, ktb_ref, vb_ref, lft_ref, cv_ref, utail_ref, ccol_ref, crow_ref,
         carry_ref, buf_ref, rot_ref, y_ref) = refs
    else:
        q_ref, k_ref, v_ref, lft_ref, u_ref, ccol_ref, crow_ref, carry_ref = refs
    tm = x_ref.shape[1]
    j = pl.program_id(1)

    @pl.when(j == 0)
    def _():
        carry_ref[...] = jnp.zeros_like(carry_ref)
        if prompt:
            buf_ref[0:CONV_HALO, :] = jnp.zeros((CONV_HALO, d_conv), F32)

    if prompt:
        @pl.when(j > 0)
        def _():
            buf_ref[0:CONV_HALO, :] = buf_ref[tm:tm + CONV_HALO, :]

    carry = carry_ref[...]
    tri = tri_ref[...]
    for t0 in range(0, tm, sub):
        rs = slice(t0, t0 + sub)
        h = _rms(x_ref[0, rs, :], g_ref[...]).astype(BF16)
        z = jnp.dot(h, w_ref[...], preferred_element_type=F32)
        k = z[:, d_att:2 * d_att]
        v = z[:, 2 * d_att:3 * d_att]
        o = 3 * d_att
        a = z[:, o:o + d_conv]
        gate = z[:, o + d_conv:o + 2 * d_conv]
        fg = z[:, o + 2 * d_conv:o + 2 * d_conv + LANES]
        q_ref[0, rs, :] = (z[:, :d_att] * Q_SCALE).astype(BF16)
        u = a * _sigmoid(gate)
        if prompt:
            kt = k.T
            kt_ref[0, :, rs] = kt
            vt_ref[0, :, rs] = v.T
            ktb_ref[0, :, rs] = kt.astype(BF16)
            vb_ref[0, rs, :] = v.astype(BF16)
            buf_ref[CONV_HALO + t0:CONV_HALO + t0 + sub, :] = u
            _causal_conv(buf_ref, rot_ref, cw_ref, cb_ref, y_ref, t0=t0, tm=sub, rows=min(sub, 128))
            cv_ref[0, rs, :] = _ln_swish(y_ref[rs, :], lg_ref[...], lb_ref[...]).astype(cv_ref.dtype)
        else:
            k_ref[0, rs, :] = k
            v_ref[0, rs, :] = v
            u_ref[0, rs, :] = u
        lf = _log_sigmoid(fg + bf_ref[...])
        lft = lf.T[:N_HEADS]
        lft_ref[0, :, rs] = lft

        chunks = []
        for c in range(sub // LANES):
            cs = _dot3(lft[:, c * LANES:(c + 1) * LANES], tri) + carry
            chunks.append(cs)
            carry = jnp.broadcast_to(cs[:, LANES - 1:LANES], cs.shape)
        crow = jnp.concatenate(chunks, axis=1) * LOG2E
        crow_ref[0, :, rs] = crow
        cfull = jnp.concatenate([crow, jnp.zeros((LANES - N_HEADS, sub), F32)], axis=0)
        ccol_ref[0, rs, :] = cfull.T[:, :N_HEADS]
    carry_ref[...] = carry

    if prompt:
        @pl.when(j == pl.num_programs(1) - 1)
        def _():
            utail_ref[0] = buf_ref[tm:tm + CONV_HALO, :]


def _inproj(x, g, w, bf, tri, conv, tm, d_att, d_conv):
    b, s, d = x.shape
    prompt = conv is not None
    tok = lambda width: pl.BlockSpec((1, tm, width), lambda i, j: (i, j, 0))
    tr = lambda rows: pl.BlockSpec((1, rows, tm), lambda i, j: (i, 0, j))
    shp = lambda width, dt: jax.ShapeDtypeStruct((b, s, width), dt)
    sht = lambda rows, dt: jax.ShapeDtypeStruct((b, rows, s), dt)
    in_specs = [tok(d), _const_spec((1, d)), _const_spec(w.shape, True), _const_spec((1, LANES)),
                _const_spec((LANES, LANES))]
    scratch = [pltpu.VMEM((N_HEADS, LANES), F32)]
    if prompt:
        in_specs += [_const_spec(conv[0].shape)] + [_const_spec((1, d_conv))] * 3
        out_specs = [tok(d_att), tr(d_att), tr(d_att), tr(d_att), tok(d_att), tr(N_HEADS), tok(d_conv),
                     pl.BlockSpec((1, CONV_HALO, d_conv), lambda i, j: (i, 0, 0))]
        out_shape = [shp(d_att, BF16), sht(d_att, F32), sht(d_att, F32), sht(d_att, BF16), shp(d_att, BF16),
                     sht(N_HEADS, F32), shp(d_conv, BF16), jax.ShapeDtypeStruct((b, CONV_HALO, d_conv), F32)]
        scratch += _conv_scratch(tm, d_conv)
    else:
        out_specs = [tok(d_att), tok(d_att), tok(d_att), tr(N_HEADS), tok(d_conv)]
        out_shape = [shp(d_att, BF16), shp(d_att, F32), shp(d_att, F32), sht(N_HEADS, F32), shp(d_conv, F32)]
    return pl.pallas_call(
        functools.partial(_inproj_kernel, d_att=d_att, d_conv=d_conv, prompt=prompt, sub=min(tm, 256)),
        grid=(b, s // tm),
        in_specs=in_specs,
        out_specs=out_specs + [tok(N_HEADS), tr(N_HEADS)],
        out_shape=out_shape + [shp(N_HEADS, F32), sht(N_HEADS, F32)],
        scratch_shapes=scratch,
        compiler_params=_params(2),
        name="inproj",
    )(x, g, w, bf, tri, *(conv or ()))


def _attn_kernel(q_ref, kt_ref, v_ref, ccol_ref, crow_ref, o_ref, *, tq):
    s_len = q_ref.shape[1]
    pair = pl.program_id(1)
    row = lax.broadcasted_iota(jnp.int32, (tq, tq), 0)
    col = lax.broadcasted_iota(jnp.int32, (tq, tq), 1)
    causal = col <= row
    low = lax.broadcasted_iota(jnp.int32, (tq, LANES), 1) < HEAD_DIM
    ccol = ccol_ref[0]
    head_lane = lax.broadcasted_iota(jnp.int32, ccol.shape, 1)
    cq_all, ck_all = [], []
    for hh in range(2):
        head = 2 * pair + hh
        cq_all.append(jnp.sum(jnp.where(head_lane == head, ccol, 0.0), axis=-1, keepdims=True))
        ck_all.append(crow_ref[0, pl.ds(head, 1), :])

    def scores(i, hh):
        k0 = i * tq
        q2 = q_ref[0, k0:k0 + tq, :]
        zero = jnp.zeros_like(q2)
        qh = jnp.where(low, q2, zero) if hh == 0 else jnp.where(low, zero, q2)
        s_diag = jnp.dot(qh, kt_ref[0, :, k0:k0 + tq], preferred_element_type=F32)
        s_off = jnp.dot(qh, kt_ref[0, :, :k0], preferred_element_type=F32) if i > 0 else None
        return s_diag, s_off

    def finish(i, hh, s_diag, s_off):
        k0 = i * tq
        cq = cq_all[hh][k0:k0 + tq]
        ck = ck_all[hh]
        s_diag = jnp.where(causal, s_diag - ck[:, k0:k0 + tq], NEG_INF)
        m = jnp.max(s_diag, axis=-1, keepdims=True)
        if i > 0:
            s_off = s_off - ck[:, :k0]
            m = jnp.maximum(m, jnp.max(s_off, axis=-1, keepdims=True))
        shift = cq - (m + cq)
        p_diag = jnp.exp2(s_diag + shift)
        l = jnp.sum(p_diag, axis=-1, keepdims=True)
        pv = jnp.dot(p_diag.astype(BF16), v_ref[0, k0:k0 + tq, :], preferred_element_type=F32)
        if i > 0:
            p_off = jnp.exp2(s_off + shift)
            l = l + jnp.sum(p_off, axis=-1, keepdims=True)
            pv = pv + jnp.dot(p_off.astype(BF16), v_ref[0, :k0, :], preferred_element_type=F32)
        return pv * (1.0 / l)

    blocks = [(i, hh) for i in range(s_len // tq) for hh in range(2)]
    nxt = scores(*blocks[0])
    halves = []
    for n, (i, hh) in enumerate(blocks):
        cur = nxt
        if n + 1 < len(blocks):
            nxt = scores(*blocks[n + 1])
        halves.append(finish(i, hh, *cur))
        if hh == 1:
            o_ref[0, i * tq:(i + 1) * tq, :] = jnp.where(low, halves[0], halves[1]).astype(BF16)
            halves = []


def _attn(q, ktb, vb, ccol, crow, tq):
    b, s, d_att = q.shape
    cols = pl.BlockSpec((1, s, LANES), lambda i, j: (i, 0, j))
    return pl.pallas_call(
        functools.partial(_attn_kernel, tq=tq),
        grid=(b, d_att // LANES),
        in_specs=[cols, pl.BlockSpec((1, LANES, s), lambda i, j: (i, j, 0)), cols,
                  pl.BlockSpec((1, s, N_HEADS), lambda i, j: (i, 0, 0)),
                  pl.BlockSpec((1, N_HEADS, s), lambda i, j: (i, 0, 0))],
        out_specs=cols,
        out_shape=jax.ShapeDtypeStruct((b, s, d_att), BF16),
        compiler_params=_params(2),
        name="attn",
    )(q, ktb, vb, ccol, crow)


def _decode_kernel(pt_ref, q_ref, kn_ref, vn_ref, cqcol_ref, cqrow_ref, us_ref, kc_ref, vc_ref, lc_ref, o_ref,
                   kbuf, vbuf, lbuf, sem, qbd_ref, m_ref, l_ref, acc_ref, carry_ref, *, n_group, n_steps):
    t = pl.program_id(0)
    js = lax.rem(t, n_steps)
    slot = lax.rem(t, 2)
    n_pages = pt_ref.shape[1]
    t_new = q_ref.shape[1]
    d_att = q_ref.shape[2]
    page = kbuf.shape[-1]
    n_rows = t_new * N_HEADS
    nt_dims = (((1,), (1,)), ((), ()))
    cqcol = cqcol_ref[0]

    def page_copies(step, to_slot, pages):
        out = []
        for g in range(n_group):
            p = pages(step, g)
            out.append(pltpu.make_async_copy(kc_ref.at[p], kbuf.at[to_slot, g], sem.at[to_slot, 0]))
            out.append(pltpu.make_async_copy(vc_ref.at[p], vbuf.at[to_slot, g], sem.at[to_slot, 1]))
            out.append(pltpu.make_async_copy(lc_ref.at[p], lbuf.at[to_slot, g], sem.at[to_slot, 2]))
        return out

    def table_page(step, g):
        return pt_ref[lax.div(step, n_steps), n_pages - 1 - (lax.rem(step, n_steps) * n_group + g)]

    @pl.when(t == 0)
    def _():
        for c in page_copies(t, slot, table_page):
            c.start()

    @pl.when(t + 1 < pl.num_programs(0))
    def _():
        for c in page_copies(t + 1, 1 - slot, table_page):
            c.start()

    @pl.when(js == 0)
    def _():
        q = q_ref[0].astype(F32)
        sub = lax.broadcasted_iota(jnp.int32, (N_HEADS, d_att), 0)
        lane = lax.broadcasted_iota(jnp.int32, (N_HEADS, d_att), 1)
        own = (lane // HEAD_DIM) == sub
        rows = [jnp.where(own, jnp.broadcast_to(q[i:i + 1, :], (N_HEADS, d_att)), 0.0) for i in range(t_new)]
        qbd = jnp.concatenate(rows, axis=0).astype(BF16)
        qbd_ref[...] = qbd
        s = lax.dot_general(qbd, kn_ref[0].astype(BF16), nt_dims, preferred_element_type=F32)
        cqk = jnp.concatenate([cqrow_ref[0]] * t_new, axis=0)
        r = lax.broadcasted_iota(jnp.int32, s.shape, 0)
        c = lax.broadcasted_iota(jnp.int32, s.shape, 1)
        s = jnp.where(c <= r // N_HEADS, s + (cqcol - cqk), NEG_INF)
        m = jnp.max(s, axis=-1, keepdims=True)
        pr = jnp.exp2(s - m)
        m_ref[...] = m
        l_ref[...] = jnp.sum(pr, axis=-1, keepdims=True)
        acc_ref[...] = jnp.dot(pr.astype(BF16), vn_ref[0].astype(BF16), preferred_element_type=F32)
        carry_ref[...] = jnp.zeros_like(carry_ref)

    for c in page_copies(t, slot, lambda step, g: 0):
        c.wait()

    qbd = qbd_ref[...]
    carry = carry_ref[...]
    us = us_ref[...]
    s_parts = []
    for g in range(n_group):
        r = _dot3(lbuf[slot, g], us)
        suffix = (r[:, :page] + carry) * LOG2E
        carry = carry + r[:, page:]
        bias = jnp.concatenate([suffix] * t_new, axis=0) + cqcol
        kt = kbuf[slot, g].reshape(d_att, page).astype(BF16)
        s_parts.append(jnp.dot(qbd, kt, preferred_element_type=F32) + bias)
    carry_ref[...] = carry
    s = jnp.concatenate(s_parts, axis=1)
    m = m_ref[...]
    m_new = jnp.maximum(m, jnp.max(s, axis=-1, keepdims=True))
    alpha = jnp.exp2(m - m_new)
    pr = jnp.exp2(s - m_new)
    m_ref[...] = m_new
    l_ref[...] = alpha * l_ref[...] + jnp.sum(pr, axis=-1, keepdims=True)
    prb = pr.astype(BF16)
    acc = alpha * acc_ref[...]
    for g in range(n_group):
        vt = vbuf[slot, g].reshape(d_att, page).astype(BF16)
        acc = acc + lax.dot_general(prb[:, g * page:(g + 1) * page], vt, nt_dims, preferred_element_type=F32)
    acc_ref[...] = acc

    @pl.when(js == n_steps - 1)
    def _():
        o = acc * (1.0 / l_ref[...])
        sub = lax.broadcasted_iota(jnp.int32, (N_HEADS, d_att), 0)
        lane = lax.broadcasted_iota(jnp.int32, (N_HEADS, d_att), 1)
        own = (lane // HEAD_DIM) == sub
        rows = [jnp.sum(jnp.where(own, o[i * N_HEADS:(i + 1) * N_HEADS], 0.0), axis=0, keepdims=True)
                for i in range(t_new)]
        o_ref[0] = jnp.concatenate(rows, axis=0).astype(o_ref.dtype)


def _decode(page_table, q, kn, vn, cqcol, cqrow, us, cache_k, cache_v, cache_lf, n_group):
    bd, t_new, d_att = q.shape
    n_pages = page_table.shape[1]
    page = cache_k.shape[3]
    n_rows = t_new * N_HEADS
    n_steps = n_pages // n_group
    per_b = lambda shape: pl.BlockSpec((1,) + shape, lambda t, pt: (t // n_steps,) + (0,) * len(shape))
    hbm = pl.BlockSpec(memory_space=pl.ANY)
    grid_spec = pltpu.PrefetchScalarGridSpec(
        num_scalar_prefetch=1,
        grid=(bd * n_steps,),
        in_specs=[per_b((t_new, d_att)), per_b(kn.shape[1:]), per_b(vn.shape[1:]), per_b((n_rows, 1)),
                  per_b(cqrow.shape[1:]), pl.BlockSpec(us.shape, lambda t, pt: (0, 0)), hbm, hbm, hbm],
        out_specs=per_b((t_new, d_att)),
        scratch_shapes=[pltpu.VMEM((2, n_group, N_HEADS, HEAD_DIM, page), F32),
                        pltpu.VMEM((2, n_group, N_HEADS, HEAD_DIM, page), F32),
                        pltpu.VMEM((2, n_group, N_HEADS, page), F32),
                        pltpu.SemaphoreType.DMA((2, 3)),
                        pltpu.VMEM((n_rows, d_att), BF16), pltpu.VMEM((n_rows, 1), F32),
                        pltpu.VMEM((n_rows, 1), F32), pltpu.VMEM((n_rows, d_att), F32),
                        pltpu.VMEM((N_HEADS, page), F32)],
    )
    return pl.pallas_call(
        functools.partial(_decode_kernel, n_group=n_group, n_steps=n_steps),
        grid_spec=grid_spec,
        out_shape=jax.ShapeDtypeStruct((bd, t_new, d_att), BF16),
        compiler_params=_params(1),
        name="decode",
    )(page_table, q, kn, vn, cqcol, cqrow, us, cache_k, cache_v, cache_lf)


def _conv_kernel(u_ref, state_ref, w_ref, b_ref, lg_ref, lb_ref, o_ref, buf_ref, rot_ref, y_ref):
    tm = u_ref.shape[1]
    for n in range(u_ref.shape[0]):
        buf_ref[0:CONV_HALO, :] = state_ref[n]
        buf_ref[CONV_HALO:, :] = u_ref[n]
        _causal_conv(buf_ref, rot_ref, w_ref, b_ref, y_ref, t0=0, tm=tm, rows=tm)
        o_ref[n] = _ln_swish(y_ref[...], lg_ref[...], lb_ref[...])


def _conv(u, state, w, b, lg, lb, per_step):
    bsz, tm, d_conv = u.shape
    assert state.shape == (bsz, CONV_HALO, d_conv)
    tile = pl.BlockSpec((per_step, tm, d_conv), lambda i: (i, 0, 0))
    vec = _const_spec((1, d_conv))
    return pl.pallas_call(
        _conv_kernel,
        grid=(bsz // per_step,),
        in_specs=[tile, pl.BlockSpec((per_step, CONV_HALO, d_conv), lambda i: (i, 0, 0)), _const_spec(w.shape),
                  vec, vec, vec],
        out_specs=tile,
        out_shape=jax.ShapeDtypeStruct((bsz, tm, d_conv), F32),
        scratch_shapes=_conv_scratch(tm, d_conv),
        compiler_params=_params(1),
        name="conv",
    )(u, state, w, b, lg, lb)


def _pool_kernel(x_ref, halo_ref, g2_ref, g3_ref, w_ref, sc_ref, o_ref, tail_ref, buf_ref, ta_ref, tb_ref,
                 *, pos0, halo_is_state):
    tm = x_ref.shape[1]
    d = x_ref.shape[2]
    grp = d // len(POOL_WINDOWS)
    n = tm + POOL_HALO
    j = pl.program_id(1)
    pos = pos0 + j * tm + lax.broadcasted_iota(jnp.int32, (tm, 1), 0)
    for bi in range(x_ref.shape[0]):
        x = x_ref[bi]
        h = _rms(x, g2_ref[...])
        if halo_is_state:
            halo = halo_ref[bi]
        else:
            halo = jnp.where(j == 0, 0.0, _rms(halo_ref[bi], g2_ref[...]))
        buf_ref[0:POOL_HALO, :] = halo
        buf_ref[POOL_HALO:, :] = h
        tail_ref[bi] = h[tm - tail_ref.shape[1]:, :]

        ta_ref[8:n, :] = buf_ref[8:n, :] + buf_ref[7:n - 1, :]
        tb_ref[16:n, grp:] = ta_ref[16:n, grp:] + ta_ref[14:n - 2, grp:]
        ta_ref[24:n, 2 * grp:] = tb_ref[24:n, 2 * grp:] + tb_ref[20:n - 4, 2 * grp:]
        s16 = ta_ref[32:n, 3 * grp:] + ta_ref[24:n - 8, 3 * grp:]
        totals = (ta_ref[POOL_HALO:n, 0:grp], tb_ref[POOL_HALO:n, grp:2 * grp],
                  ta_ref[POOL_HALO:n, 2 * grp:3 * grp], s16)

        outs = []
        for gi, win in enumerate(POOL_WINDOWS):
            cnt = jnp.minimum(pos + 1, win).astype(F32)
            mixed = totals[gi] / cnt - h[:, gi * grp:(gi + 1) * grp]
            outs.append(jnp.dot(mixed.astype(BF16), w_ref[gi], preferred_element_type=F32))
        mix = jnp.concatenate(outs, axis=-1) * sc_ref[...]
        o_ref[bi] = x + _rms(mix, g3_ref[...])


def _pool(x, state, g2, g3, w, scale, tm, pos0, per_step=1):
    b, s, d = x.shape
    per_halo = tm // POOL_HALO
    n_tail = min(tm, POOL_TAIL)
    tile = pl.BlockSpec((per_step, tm, d), lambda i, j: (i, j, 0))
    if state is None:
        halo = pl.BlockSpec((per_step, POOL_HALO, d), lambda i, j: (i, jnp.maximum(j * per_halo - 1, 0), 0))
    else:
        assert s == tm and state.shape == (b, POOL_HALO, d)
        halo = pl.BlockSpec((per_step, POOL_HALO, d), lambda i, j: (i, 0, 0))
    vec = _const_spec((1, d))
    buf = pltpu.VMEM((tm + POOL_HALO, d), F32)
    return pl.pallas_call(
        functools.partial(_pool_kernel, pos0=pos0, halo_is_state=state is not None),
        grid=(b // per_step, s // tm),
        in_specs=[tile, halo, vec, vec, _const_spec(w.shape), vec],
        out_specs=[tile, pl.BlockSpec((per_step, n_tail, d), lambda i, j: (i, 0, 0))],
        out_shape=[jax.ShapeDtypeStruct((b, s, d), F32), jax.ShapeDtypeStruct((b, n_tail, d), F32)],
        scratch_shapes=[buf, buf, buf],
        compiler_params=_params(2),
        name="pool",
    )(x, x if state is None else state, g2, g3, w, scale)


def kernel(x_prompt, x_sample, cache_k, cache_v, cache_logf, state_conv, state_pool, page_table,
           norm_g, ffn_w_gate, ffn_w_up, ffn_w_down, mix_w_in, fgate_b, conv_dw_w, conv_dw_b,
           conv_ln_g, conv_ln_b, mix_w_out, pool_w, pool_scale):
    b, s, d = x_prompt.shape
    bd, t_new, _ = x_sample.shape
    depth = norm_g.shape[0]
    d_att = N_HEADS * HEAD_DIM
    d_conv = d - d_att
    page = cache_k.shape[2]
    past_len = page_table.shape[1] * page
    tm_prompt = 512
    m_sample = bd * t_new
    t_pad = 8

    xp = x_prompt.reshape(b * s, d)
    xs = x_sample.reshape(m_sample, d)
    vec = lambda a: a.reshape(1, -1).astype(F32)

    def ffn_both(xp, xs, layer, half, mix=None):
        gp, gq = vec(norm_g[layer, 4 * half]), vec(norm_g[layer, 4 * half + 1])
        return _ffn(xp, xs, gp, gq, ffn_w_gate.astype(F32), ffn_w_up.astype(F32), ffn_w_down.astype(F32),
                    (layer, half), 1024, mix)

    r = jnp.arange(LANES)
    tri_incl = (r[:, None] <= r[None, :]).astype(F32)
    tri_group = tri_incl * (r[:, None] // t_new == r[None, :] // t_new).astype(F32)
    rp = jnp.arange(page)
    suffix_total = jnp.concatenate([(rp[:, None] > rp[None, :]).astype(F32), jnp.ones((page, page), F32)], axis=1)

    outs_p = {k: [] for k in ("k", "v", "lf", "conv", "pool")}
    outs_s = {k: [] for k in ("k", "v", "lf", "conv", "pool")}
    for layer in range(depth):
        g = norm_g[layer]
        xp, xs = ffn_both(xp, xs, layer, 0)
        mix = None
        if layer % 2 == 0:
            e = layer // 2
            w_in = mix_w_in[e]
            o = 3 * d_att + N_HEADS
            w_cat = jnp.concatenate(
                [w_in[:, :3 * d_att], w_in[:, o:], w_in[:, 3 * d_att:o],
                 jnp.zeros((d, LANES - N_HEADS), w_in.dtype)], axis=1).astype(BF16)
            bf = jnp.concatenate([fgate_b[e].astype(F32), jnp.zeros((LANES - N_HEADS,), F32)]).reshape(1, LANES)
            wa = mix_w_out[e, :d_att].astype(BF16)
            wc = mix_w_out[e, d_att:].astype(BF16)
            cw, cb = conv_dw_w[e].astype(F32), vec(conv_dw_b[e])
            lg, lb = vec(conv_ln_g[e]), vec(conv_ln_b[e])
            g2, g3 = vec(g[2]), vec(g[3])

            q, kt, vt, ktb, vb, lft, cv, u_tail, ccol, crow = _inproj(
                xp.reshape(b, s, d), g2, w_cat, bf, tri_incl, (cw, cb, lg, lb), tm_prompt, d_att, d_conv)
            att_p = _attn(q, ktb, vb, ccol, crow, 256).reshape(b * s, d_att)
            cv_p = cv.reshape(b * s, d_conv)
            outs_p["k"].append(kt.reshape(b, N_HEADS, HEAD_DIM, s).transpose(0, 3, 1, 2))
            outs_p["v"].append(vt.reshape(b, N_HEADS, HEAD_DIM, s).transpose(0, 3, 1, 2))
            outs_p["lf"].append(lft.transpose(0, 2, 1))
            outs_p["conv"].append(u_tail[:, CONV_HALO - (CONV_W - 1):])

            q, k, v, lft, u, ccol, crow = _inproj(
                xs.reshape(1, m_sample, d), g2, w_cat, bf, tri_group, None, m_sample, d_att, d_conv)
            lf = lft.reshape(N_HEADS, m_sample).T
            pad_rows = lambda a: jnp.pad(a.reshape(bd, t_new, d_att), ((0, 0), (0, 16 - t_new), (0, 0)))
            cqcol = ccol.reshape(bd, t_new * N_HEADS, 1)
            cqrow = jnp.pad(crow.reshape(N_HEADS, bd, t_new).transpose(1, 0, 2), ((0, 0), (0, 0), (0, 16 - t_new)))
            att = _decode(page_table, q.reshape(bd, t_new, d_att), pad_rows(k), pad_rows(v), cqcol, cqrow,
                          suffix_total, cache_k[e].transpose(0, 2, 3, 1), cache_v[e].transpose(0, 2, 3, 1),
                          cache_logf[e].astype(F32).transpose(0, 2, 1), 16)
            st = state_conv[e].astype(F32)
            cv = _conv(jnp.pad(u.reshape(bd, t_new, d_conv), ((0, 0), (0, t_pad - t_new), (0, 0))),
                       jnp.pad(st, ((0, 0), (CONV_HALO - (CONV_W - 1), 0), (0, 0))), cw, cb, lg, lb, 8)
            mix = (att_p, cv_p, att.reshape(m_sample, d_att), cv[:, :t_new].reshape(m_sample, d_conv), wa, wc, g3)
            outs_s["k"].append(k.reshape(bd, t_new, N_HEADS, HEAD_DIM))
            outs_s["v"].append(v.reshape(bd, t_new, N_HEADS, HEAD_DIM))
            outs_s["lf"].append(lf.reshape(bd, t_new, N_HEADS))
            outs_s["conv"].append(jnp.concatenate([st, u.reshape(bd, t_new, d_conv)], axis=1)[:, t_new:])
        else:
            o = layer // 2
            g2, g3 = vec(g[2]), vec(g[3])
            pw = pool_w[o].astype(BF16)
            sc = vec(pool_scale[o])
            n_keep = POOL_WINDOWS[-1] - 1
            xp3, tail = _pool(xp.reshape(b, s, d), None, g2, g3, pw, sc, 512, 0)
            xp = xp3.reshape(b * s, d)
            outs_p["pool"].append(tail[:, POOL_TAIL - n_keep:])
            st = state_pool[o].astype(F32)
            xs3, hs = _pool(jnp.pad(xs.reshape(bd, t_new, d), ((0, 0), (0, t_pad - t_new), (0, 0))),
                            jnp.pad(st, ((0, 0), (POOL_HALO - n_keep, 0), (0, 0))), g2, g3, pw, sc, t_pad, past_len, 8)
            xs = xs3[:, :t_new].reshape(m_sample, d)
            outs_s["pool"].append(jnp.concatenate([st, hs[:, :t_new]], axis=1)[:, t_new:])
        xp, xs = ffn_both(xp, xs, layer, 1, mix)

    stack = lambda l: jnp.stack(l)
    return (xp.reshape(b, s, d), xs.reshape(bd, t_new, d),
            stack(outs_p["k"]), stack(outs_p["v"]), stack(outs_p["lf"]), stack(outs_p["conv"]),
            stack(outs_p["pool"]),
            stack(outs_s["k"]), stack(outs_s["v"]), stack(outs_s["lf"]), stack(outs_s["conv"]),
            stack(outs_s["pool"]))
```

```python
import functools

import jax
import jax.numpy as jnp
from jax import lax
from jax.experimental import pallas as pl
from jax.experimental.pallas import tpu as pltpu

F32 = jnp.float32
BF16 = jnp.bfloat16

RMS_EPS = 1e-6
LN_EPS = 1e-5
NEG_INF = -1e30
LOG2E = 1.4426950408889634

LANES = 128
HEAD_DIM = 64
N_HEADS = 8
Q_SCALE = HEAD_DIM ** -0.5 * LOG2E
CONV_W = 31
CONV_HALO = 32
POOL_WINDOWS = (2, 4, 8, 16)
POOL_HALO = 32
POOL_TAIL = 16
VMEM_LIMIT_BYTES = 56 * 1024 * 1024
FFN_VMEM_LIMIT_BYTES = 61 * 1024 * 1024


def _params(n_axes, vmem_limit_bytes=VMEM_LIMIT_BYTES):
    return pltpu.CompilerParams(dimension_semantics=("arbitrary",) * n_axes,
                                vmem_limit_bytes=vmem_limit_bytes)


def _const_spec(shape, single_buffer=False):
    zeros = (0,) * len(shape)
    if single_buffer:
        return pl.BlockSpec(shape, lambda *_: zeros, pipeline_mode=pl.Buffered(1))
    return pl.BlockSpec(shape, lambda *_: zeros)


def _rms(x, g):
    ms = jnp.mean(x * x, axis=-1, keepdims=True)
    return x * lax.rsqrt(ms + RMS_EPS) * g


def _sigmoid(x):
    return 1.0 / (1.0 + jnp.exp(-x))


def _log_sigmoid(x):
    return jnp.minimum(x, 0.0) - jnp.log1p(jnp.exp(-jnp.abs(x)))


def _split3(x):
    hi = x.astype(BF16).astype(F32)
    r = x - hi
    mid = r.astype(BF16).astype(F32)
    return hi, mid, r - mid


def _dot3(x, w):
    hi, mid, lo = _split3(x)
    d = lambda a: jnp.dot(a, w, preferred_element_type=F32)
    return d(hi) + d(mid) + d(lo)


def _convert_slice_copies(i, slot, src_hbm, which, dst_hbm, stage_in, stage_out, sem_in, sem_out, n_slices):
    ins, outs = [], []
    for k in range(3):
        rows = dst_hbm[k].shape[0] // n_slices
        ins.append(pltpu.make_async_copy(src_hbm[k].at[which[0], which[1], pl.ds(i * rows, rows)],
                                         stage_in[k].at[slot], sem_in.at[slot, k]))
        outs.append(pltpu.make_async_copy(stage_out[k].at[slot], dst_hbm[k].at[pl.ds(i * rows, rows)],
                                          sem_out.at[slot, k]))
    return ins, outs


def _ffn_kernel(*refs, chunk, n_prompt, mix, convert):
    xp_ref, xs_ref, gpre_ref, gpost_ref, wg_ref, wu_ref, wd_ref = refs[:7]
    refs = refs[7:]
    if mix:
        attp_ref, cvp_ref, atts_ref, cvs_ref, wa_ref, wc_ref, gmix_ref = refs[:7]
        refs = refs[7:]
    if convert:
        src_hbm, refs = refs[:3], refs[3:]
    op_ref, os_ref = refs[:2]
    refs = refs[2:]
    if convert:
        dst_hbm, refs = refs[:3], refs[3:]
    a_ref = refs[0]
    if convert:
        stage_in, stage_out, (sem_in, sem_out) = refs[1:4], refs[4:7], refs[7:9]
    i = pl.program_id(0)

    def slice_copies(step):
        return _convert_slice_copies(step, lax.rem(step, 2), src_hbm, convert, dst_hbm, stage_in, stage_out,
                                     sem_in, sem_out, n_prompt)

    def convert_slice():
        @pl.when(i == 0)
        def _():
            for c in slice_copies(i)[0]:
                c.start()

        @pl.when(i + 1 < n_prompt)
        def _():
            for c in slice_copies(i + 1)[0]:
                c.start()

        ins, outs = slice_copies(i)
        for c in ins:
            c.wait()

        @pl.when(i >= 2)
        def _():
            for c in slice_copies(i - 2)[1]:
                c.wait()

        slot = lax.rem(i, 2)
        for k in range(3):
            stage_out[k][slot] = stage_in[k][slot].astype(BF16)
        for c in outs:
            c.start()

    def update(x_ref, o_ref, att_ref=None, cv_ref=None):
        rows = x_ref.shape[0]
        x = x_ref[...]
        if mix:
            mo = jnp.dot(att_ref[...].astype(BF16), wa_ref[...], preferred_element_type=F32)
            mo = mo + jnp.dot(cv_ref[...].astype(BF16), wc_ref[...], preferred_element_type=F32)
            x = x + _rms(mo, gmix_ref[...])
        h = _rms(x, gpre_ref[...]).astype(BF16)
        d_ff = wg_ref.shape[1]
        for c0 in range(0, d_ff, chunk):
            sl = slice(c0, min(c0 + chunk, d_ff))
            g = jnp.dot(h, wg_ref[:, sl], preferred_element_type=F32)
            u = jnp.dot(h, wu_ref[:, sl], preferred_element_type=F32)
            a_ref[0:rows, sl] = (g * _sigmoid(g) * u).astype(BF16)
        y = jnp.dot(a_ref[0:rows, :], wd_ref[...], preferred_element_type=F32)
        o_ref[...] = x + 0.5 * _rms(y, gpost_ref[...])

    @pl.when(i < n_prompt)
    def _():
        if convert:
            convert_slice()
        update(xp_ref, op_ref, *((attp_ref, cvp_ref) if mix else ()))

    @pl.when(i == n_prompt)
    def _():
        if convert:
            for step in (n_prompt - 2, n_prompt - 1):
                for c in slice_copies(step)[1]:
                    c.wait()
        update(xs_ref, os_ref, *((atts_ref, cvs_ref) if mix else ()))


def _ffn(xp, xs, g_pre, g_post, wg, wu, wd, tm, mix=None, convert=None):
    m, d = xp.shape
    d_ff = wg.shape[1]
    n_prompt = m // tm
    tile = lambda width: pl.BlockSpec((tm, width), lambda i: (jnp.minimum(i, n_prompt - 1), 0))
    hbm = pl.BlockSpec(memory_space=pl.ANY)
    in_specs = [tile(d), _const_spec(xs.shape), _const_spec((1, d)), _const_spec((1, d)),
                _const_spec(wg.shape, True), _const_spec(wu.shape, True), _const_spec(wd.shape, True)]
    out_specs = [tile(d), _const_spec(xs.shape)]
    out_shape = [jax.ShapeDtypeStruct((m, d), F32), jax.ShapeDtypeStruct(xs.shape, F32)]
    scratch = [pltpu.VMEM((tm, d_ff), BF16)]
    operands = [xp, xs, g_pre, g_post, wg, wu, wd]
    if mix is not None:
        att_p, cv_p, att_s, cv_s, w_att, w_cv, _ = mix
        in_specs += [tile(att_p.shape[1]), tile(cv_p.shape[1]), _const_spec(att_s.shape), _const_spec(cv_s.shape),
                     _const_spec(w_att.shape, True), _const_spec(w_cv.shape, True), _const_spec((1, d))]
        operands += list(mix)
    if convert is not None:
        stacks, which = convert
        in_specs += [hbm] * 3
        operands += list(stacks)
        out_specs += [hbm] * 3
        shapes = [(d, d_ff), (d, d_ff), (d_ff, d)]
        out_shape += [jax.ShapeDtypeStruct(sh, BF16) for sh in shapes]
        scratch += [pltpu.VMEM((2, r // n_prompt, c), F32) for r, c in shapes]
        scratch += [pltpu.VMEM((2, r // n_prompt, c), BF16) for r, c in shapes]
        scratch += [pltpu.SemaphoreType.DMA((2, 3)), pltpu.SemaphoreType.DMA((2, 3))]
    return pl.pallas_call(
        functools.partial(_ffn_kernel, chunk=512, n_prompt=n_prompt, mix=mix is not None,
                          convert=None if convert is None else convert[1]),
        grid=(n_prompt + 1,),
        in_specs=in_specs,
        out_specs=out_specs,
        out_shape=out_shape,
        scratch_shapes=scratch,
        compiler_params=_params(1, FFN_VMEM_LIMIT_BYTES),
        name="ffn",
    )(*operands)


def _ln_swish(y, g, b):
    mu = jnp.mean(y, axis=-1, keepdims=True)
    d = y - mu
    var = jnp.mean(d * d, axis=-1, keepdims=True)
    yn = d * lax.rsqrt(var + LN_EPS) * g + b
    return yn * _sigmoid(yn)


def _causal_conv(buf_ref, rot_ref, w_ref, b_ref, y_ref, *, t0, tm, rows):
    d_conv = buf_ref.shape[1]
    base = CONV_HALO - (CONV_W - 1)
    n_rot = tm + CONV_HALO - 8
    for r in range(1, 8):
        rot_ref[r - 1, t0:t0 + n_rot, :] = buf_ref[t0 + r:t0 + r + n_rot, :]
    for cb in range(d_conv // LANES):
        ls = slice(cb * LANES, (cb + 1) * LANES)
        for rb in range(tm // rows):
            acc = jnp.zeros((rows, LANES), F32)
            for tap in range(CONV_W):
                a, r = divmod(base + tap, 8)
                r0 = t0 + rb * rows + 8 * a
                src = buf_ref[r0:r0 + rows, ls] if r == 0 else rot_ref[r - 1, r0:r0 + rows, ls]
                acc = acc + w_ref[tap:tap + 1, ls] * src
            y_ref[t0 + rb * rows:t0 + (rb + 1) * rows, ls] = acc + b_ref[:, ls]


def _conv_scratch(tm, d_conv):
    return [pltpu.VMEM((tm + CONV_HALO, d_conv), F32), pltpu.VMEM((7, tm + CONV_HALO - 8, d_conv), F32),
            pltpu.VMEM((tm, d_conv), F32)]


def _inproj_kernel(x_ref, g_ref, w_ref, bf_ref, tri_ref, *refs, d_att, d_conv, prompt, sub):
    if prompt:
        (cw_ref, cb_ref, lg_ref, lb_ref,
         q_ref, kt_ref, vt_ref, ktb_ref, vb_ref, lft_ref, cv_ref, utail_ref, ccol_ref, crow_ref,
         carry_ref, buf_ref, rot_ref, y_ref) = refs
    else:
        q_ref, k_ref, v_ref, lft_ref, u_ref, ccol_ref, crow_ref, carry_ref = refs
    tm = x_ref.shape[1]
    j = pl.program_id(1)

    @pl.when(j == 0)
    def _():
        carry_ref[...] = jnp.zeros_like(carry_ref)
        if prompt:
            buf_ref[0:CONV_HALO, :] = jnp.zeros((CONV_HALO, d_conv), F32)

    if prompt:
        @pl.when(j > 0)
        def _():
            buf_ref[0:CONV_HALO, :] = buf_ref[tm:tm + CONV_HALO, :]

    carry = carry_ref[...]
    tri = tri_ref[...]
    for t0 in range(0, tm, sub):
        rs = slice(t0, t0 + sub)
        h = _rms(x_ref[0, rs, :], g_ref[...]).astype(BF16)
        z = jnp.dot(h, w_ref[...], preferred_element_type=F32)
        k = z[:, d_att:2 * d_att]
        v = z[:, 2 * d_att:3 * d_att]
        o = 3 * d_att
        a = z[:, o:o + d_conv]
        gate = z[:, o + d_conv:o + 2 * d_conv]
        fg = z[:, o + 2 * d_conv:o + 2 * d_conv + LANES]
        q_ref[0, rs, :] = (z[:, :d_att] * Q_SCALE).astype(BF16)
        u = a * _sigmoid(gate)
        if prompt:
            kt = k.T
            kt_ref[0, :, rs] = kt
            vt_ref[0, :, rs] = v.T
            ktb_ref[0, :, rs] = kt.astype(BF16)
            vb_ref[0, rs, :] = v.astype(BF16)
            buf_ref[CONV_HALO + t0:CONV_HALO + t0 + sub, :] = u
            _causal_conv(buf_ref, rot_ref, cw_ref, cb_ref, y_ref, t0=t0, tm=sub, rows=min(sub, 128))
            cv_ref[0, rs, :] = _ln_swish(y_ref[rs, :], lg_ref[...], lb_ref[...]).astype(cv_ref.dtype)
        else:
            k_ref[0, rs, :] = k
            v_ref[0, rs, :] = v
            u_ref[0, rs, :] = u
        lf = _log_sigmoid(fg + bf_ref[...])
        lft = lf.T[:N_HEADS]
        lft_ref[0, :, rs] = lft

        chunks = []
        for c in range(sub // LANES):
            cs = _dot3(lft[:, c * LANES:(c + 1) * LANES], tri) + carry
            chunks.append(cs)
            carry = jnp.broadcast_to(cs[:, LANES - 1:LANES], cs.shape)
        crow = jnp.concatenate(chunks, axis=1) * LOG2E
        crow_ref[0, :, rs] = crow
        cfull = jnp.concatenate([crow, jnp.zeros((LANES - N_HEADS, sub), F32)], axis=0)
        ccol_ref[0, rs, :] = cfull.T[:, :N_HEADS]
    carry_ref[...] = carry

    if prompt:
        @pl.when(j == pl.num_programs(1) - 1)
        def _():
            utail_ref[0] = buf_ref[tm:tm + CONV_HALO, :]


def _inproj(x, g, w, bf, tri, conv, tm, d_att, d_conv):
    b, s, d = x.shape
    prompt = conv is not None
    tok = lambda width: pl.BlockSpec((1, tm, width), lambda i, j: (i, j, 0))
    tr = lambda rows: pl.BlockSpec((1, rows, tm), lambda i, j: (i, 0, j))
    shp = lambda width, dt: jax.ShapeDtypeStruct((b, s, width), dt)
    sht = lambda rows, dt: jax.ShapeDtypeStruct((b, rows, s), dt)
    in_specs = [tok(d), _const_spec((1, d)), _const_spec(w.shape, True), _const_spec((1, LANES)),
                _const_spec((LANES, LANES))]
    scratch = [pltpu.VMEM((N_HEADS, LANES), F32)]
    if prompt:
        in_specs += [_const_spec(conv[0].shape)] + [_const_spec((1, d_conv))] * 3
        out_specs = [tok(d_att), tr(d_att), tr(d_att), tr(d_att), tok(d_att), tr(N_HEADS), tok(d_conv),
                     pl.BlockSpec((1, CONV_HALO, d_conv), lambda i, j: (i, 0, 0))]
        out_shape = [shp(d_att, BF16), sht(d_att, F32), sht(d_att, F32), sht(d_att, BF16), shp(d_att, BF16),
                     sht(N_HEADS, F32), shp(d_conv, BF16), jax.ShapeDtypeStruct((b, CONV_HALO, d_conv), F32)]
        scratch += _conv_scratch(tm, d_conv)
    else:
        out_specs = [tok(d_att), tok(d_att), tok(d_att), tr(N_HEADS), tok(d_conv)]
        out_shape = [shp(d_att, BF16), shp(d_att, F32), shp(d_att, F32), sht(N_HEADS, F32), shp(d_conv, F32)]
    return pl.pallas_call(
        functools.partial(_inproj_kernel, d_att=d_att, d_conv=d_conv, prompt=prompt, sub=min(tm, 256)),
        grid=(b, s // tm),
        in_specs=in_specs,
        out_specs=out_specs + [tok(N_HEADS), tr(N_HEADS)],
        out_shape=out_shape + [shp(N_HEADS, F32), sht(N_HEADS, F32)],
        scratch_shapes=scratch,
        compiler_params=_params(2),
        name="inproj",
    )(x, g, w, bf, tri, *(conv or ()))


def _attn_kernel(q_ref, kt_ref, v_ref, ccol_ref, crow_ref, o_ref, *, tq):
    s_len = q_ref.shape[1]
    pair = pl.program_id(1)
    row = lax.broadcasted_iota(jnp.int32, (tq, tq), 0)
    col = lax.broadcasted_iota(jnp.int32, (tq, tq), 1)
    causal = col <= row
    low = lax.broadcasted_iota(jnp.int32, (tq, LANES), 1) < HEAD_DIM
    ccol = ccol_ref[0]
    head_lane = lax.broadcasted_iota(jnp.int32, ccol.shape, 1)
    cq_all, ck_all = [], []
    for hh in range(2):
        head = 2 * pair + hh
        cq_all.append(jnp.sum(jnp.where(head_lane == head, ccol, 0.0), axis=-1, keepdims=True))
        ck_all.append(crow_ref[0, pl.ds(head, 1), :])

    def scores(i, hh):
        k0 = i * tq
        q2 = q_ref[0, k0:k0 + tq, :]
        zero = jnp.zeros_like(q2)
        qh = jnp.where(low, q2, zero) if hh == 0 else jnp.where(low, zero, q2)
        s_diag = jnp.dot(qh, kt_ref[0, :, k0:k0 + tq], preferred_element_type=F32)
        s_off = jnp.dot(qh, kt_ref[0, :, :k0], preferred_element_type=F32) if i > 0 else None
        return s_diag, s_off

    def finish(i, hh, s_diag, s_off):
        k0 = i * tq
        cq = cq_all[hh][k0:k0 + tq]
        ck = ck_all[hh]
        s_diag = jnp.where(causal, s_diag - ck[:, k0:k0 + tq], NEG_INF)
        m = jnp.max(s_diag, axis=-1, keepdims=True)
        if i > 0:
            s_off = s_off - ck[:, :k0]
            m = jnp.maximum(m, jnp.max(s_off, axis=-1, keepdims=True))
        shift = cq - (m + cq)
        p_diag = jnp.exp2(s_diag + shift)
        l = jnp.sum(p_diag, axis=-1, keepdims=True)
        pv = jnp.dot(p_diag.astype(BF16), v_ref[0, k0:k0 + tq, :], preferred_element_type=F32)
        if i > 0:
            p_off = jnp.exp2(s_off + shift)
            l = l + jnp.sum(p_off, axis=-1, keepdims=True)
            pv = pv + jnp.dot(p_off.astype(BF16), v_ref[0, :k0, :], preferred_element_type=F32)
        return pv * (1.0 / l)

    blocks = [(i, hh) for i in range(s_len // tq) for hh in range(2)]
    nxt = scores(*blocks[0])
    halves = []
    for n, (i, hh) in enumerate(blocks):
        cur = nxt
        if n + 1 < len(blocks):
            nxt = scores(*blocks[n + 1])
        halves.append(finish(i, hh, *cur))
        if hh == 1:
            o_ref[0, i * tq:(i + 1) * tq, :] = jnp.where(low, halves[0], halves[1]).astype(BF16)
            halves = []


def _attn(q, ktb, vb, ccol, crow, tq):
    b, s, d_att = q.shape
    cols = pl.BlockSpec((1, s, LANES), lambda i, j: (i, 0, j))
    return pl.pallas_call(
        functools.partial(_attn_kernel, tq=tq),
        grid=(b, d_att // LANES),
        in_specs=[cols, pl.BlockSpec((1, LANES, s), lambda i, j: (i, j, 0)), cols,
                  pl.BlockSpec((1, s, N_HEADS), lambda i, j: (i, 0, 0)),
                  pl.BlockSpec((1, N_HEADS, s), lambda i, j: (i, 0, 0))],
        out_specs=cols,
        out_shape=jax.ShapeDtypeStruct((b, s, d_att), BF16),
        compiler_params=_params(2),
        name="attn",
    )(q, ktb, vb, ccol, crow)


def _decode_kernel(pt_ref, q_ref, kn_ref, vn_ref, cqcol_ref, cqrow_ref, us_ref, kc_ref, vc_ref, lc_ref, o_ref,
                   kbuf, vbuf, lbuf, sem, qbd_ref, m_ref, l_ref, acc_ref, carry_ref, *, n_group, n_steps):
    t = pl.program_id(0)
    js = lax.rem(t, n_steps)
    slot = lax.rem(t, 2)
    n_pages = pt_ref.shape[1]
    t_new = q_ref.shape[1]
    d_att = q_ref.shape[2]
    page = kbuf.shape[-1]
    n_rows = t_new * N_HEADS
    nt_dims = (((1,), (1,)), ((), ()))
    cqcol = cqcol_ref[0]

    def page_copies(step, to_slot, pages):
        out = []
        for g in range(n_group):
            p = pages(step, g)
            out.append(pltpu.make_async_copy(kc_ref.at[p], kbuf.at[to_slot, g], sem.at[to_slot, 0]))
            out.append(pltpu.make_async_copy(vc_ref.at[p], vbuf.at[to_slot, g], sem.at[to_slot, 1]))
            out.append(pltpu.make_async_copy(lc_ref.at[p], lbuf.at[to_slot, g], sem.at[to_slot, 2]))
        return out

    def table_page(step, g):
        return pt_ref[lax.div(step, n_steps), n_pages - 1 - (lax.rem(step, n_steps) * n_group + g)]

    @pl.when(t == 0)
    def _():
        for c in page_copies(t, slot, table_page):
            c.start()

    @pl.when(t + 1 < pl.num_programs(0))
    def _():
        for c in page_copies(t + 1, 1 - slot, table_page):
            c.start()

    @pl.when(js == 0)
    def _():
        q = q_ref[0].astype(F32)
        sub = lax.broadcasted_iota(jnp.int32, (N_HEADS, d_att), 0)
        lane = lax.broadcasted_iota(jnp.int32, (N_HEADS, d_att), 1)
        own = (lane // HEAD_DIM) == sub
        rows = [jnp.where(own, jnp.broadcast_to(q[i:i + 1, :], (N_HEADS, d_att)), 0.0) for i in range(t_new)]
        qbd = jnp.concatenate(rows, axis=0).astype(BF16)
        qbd_ref[...] = qbd
        s = lax.dot_general(qbd, kn_ref[0].astype(BF16), nt_dims, preferred_element_type=F32)
        cqk = jnp.concatenate([cqrow_ref[0]] * t_new, axis=0)
        r = lax.broadcasted_iota(jnp.int32, s.shape, 0)
        c = lax.broadcasted_iota(jnp.int32, s.shape, 1)
        s = jnp.where(c <= r // N_HEADS, s + (cqcol - cqk), NEG_INF)
        m = jnp.max(s, axis=-1, keepdims=True)
        pr = jnp.exp2(s - m)
        m_ref[...] = m
        l_ref[...] = jnp.sum(pr, axis=-1, keepdims=True)
        acc_ref[...] = jnp.dot(pr.astype(BF16), vn_ref[0].astype(BF16), preferred_element_type=F32)
        carry_ref[...] = jnp.zeros_like(carry_ref)

    for c in page_copies(t, slot, lambda step, g: 0):
        c.wait()

    qbd = qbd_ref[...]
    carry = carry_ref[...]
    us = us_ref[...]
    s_parts = []
    for g in range(n_group):
        r = _dot3(lbuf[slot, g], us)
        suffix = (r[:, :page] + carry) * LOG2E
        carry = carry + r[:, page:]
        bias = jnp.concatenate([suffix] * t_new, axis=0) + cqcol
        kt = kbuf[slot, g].reshape(d_att, page).astype(BF16)
        s_parts.append(jnp.dot(qbd, kt, preferred_element_type=F32) + bias)
    carry_ref[...] = carry
    s = jnp.concatenate(s_parts, axis=1)
    m = m_ref[...]
    m_new = jnp.maximum(m, jnp.max(s, axis=-1, keepdims=True))
    alpha = jnp.exp2(m - m_new)
    pr = jnp.exp2(s - m_new)
    m_ref[...] = m_new
    l_ref[...] = alpha * l_ref[...] + jnp.sum(pr, axis=-1, keepdims=True)
    prb = pr.astype(BF16)
    acc = alpha * acc_ref[...]
    for g in range(n_group):
        vt = vbuf[slot, g].reshape(d_att, page).astype(BF16)
        acc = acc + lax.dot_general(prb[:, g * page:(g + 1) * page], vt, nt_dims, preferred_element_type=F32)
    acc_ref[...] = acc

    @pl.when(js == n_steps - 1)
    def _():
        o = acc * (1.0 / l_ref[...])
        sub = lax.broadcasted_iota(jnp.int32, (N_HEADS, d_att), 0)
        lane = lax.broadcasted_iota(jnp.int32, (N_HEADS, d_att), 1)
        own = (lane // HEAD_DIM) == sub
        rows = [jnp.sum(jnp.where(own, o[i * N_HEADS:(i + 1) * N_HEADS], 0.0), axis=0, keepdims=True)
                for i in range(t_new)]
        o_ref[0] = jnp.concatenate(rows, axis=0).astype(o_ref.dtype)


def _decode(page_table, q, kn, vn, cqcol, cqrow, us, cache_k, cache_v, cache_lf, n_group):
    bd, t_new, d_att = q.shape
    n_pages = page_table.shape[1]
    page = cache_k.shape[3]
    n_rows = t_new * N_HEADS
    n_steps = n_pages // n_group
    per_b = lambda shape: pl.BlockSpec((1,) + shape, lambda t, pt: (t // n_steps,) + (0,) * len(shape))
    hbm = pl.BlockSpec(memory_space=pl.ANY)
    grid_spec = pltpu.PrefetchScalarGridSpec(
        num_scalar_prefetch=1,
        grid=(bd * n_steps,),
        in_specs=[per_b((t_new, d_att)), per_b(kn.shape[1:]), per_b(vn.shape[1:]), per_b((n_rows, 1)),
                  per_b(cqrow.shape[1:]), pl.BlockSpec(us.shape, lambda t, pt: (0, 0)), hbm, hbm, hbm],
        out_specs=per_b((t_new, d_att)),
        scratch_shapes=[pltpu.VMEM((2, n_group, N_HEADS, HEAD_DIM, page), F32),
                        pltpu.VMEM((2, n_group, N_HEADS, HEAD_DIM, page), F32),
                        pltpu.VMEM((2, n_group, N_HEADS, page), F32),
                        pltpu.SemaphoreType.DMA((2, 3)),
                        pltpu.VMEM((n_rows, d_att), BF16), pltpu.VMEM((n_rows, 1), F32),
                        pltpu.VMEM((n_rows, 1), F32), pltpu.VMEM((n_rows, d_att), F32),
                        pltpu.VMEM((N_HEADS, page), F32)],
    )
    return pl.pallas_call(
        functools.partial(_decode_kernel, n_group=n_group, n_steps=n_steps),
        grid_spec=grid_spec,
        out_shape=jax.ShapeDtypeStruct((bd, t_new, d_att), BF16),
        compiler_params=_params(1),
        name="decode",
    )(page_table, q, kn, vn, cqcol, cqrow, us, cache_k, cache_v, cache_lf)


def _conv_kernel(u_ref, state_ref, w_ref, b_ref, lg_ref, lb_ref, o_ref, buf_ref, rot_ref, y_ref):
    tm = u_ref.shape[1]
    for n in range(u_ref.shape[0]):
        buf_ref[0:CONV_HALO, :] = state_ref[n]
        buf_ref[CONV_HALO:, :] = u_ref[n]
        _causal_conv(buf_ref, rot_ref, w_ref, b_ref, y_ref, t0=0, tm=tm, rows=tm)
        o_ref[n] = _ln_swish(y_ref[...], lg_ref[...], lb_ref[...])


def _conv(u, state, w, b, lg, lb, per_step):
    bsz, tm, d_conv = u.shape
    assert state.shape == (bsz, CONV_HALO, d_conv)
    tile = pl.BlockSpec((per_step, tm, d_conv), lambda i: (i, 0, 0))
    vec = _const_spec((1, d_conv))
    return pl.pallas_call(
        _conv_kernel,
        grid=(bsz // per_step,),
        in_specs=[tile, pl.BlockSpec((per_step, CONV_HALO, d_conv), lambda i: (i, 0, 0)), _const_spec(w.shape),
                  vec, vec, vec],
        out_specs=tile,
        out_shape=jax.ShapeDtypeStruct((bsz, tm, d_conv), F32),
        scratch_shapes=_conv_scratch(tm, d_conv),
        compiler_params=_params(1),
        name="conv",
    )(u, state, w, b, lg, lb)


def _pool_kernel(x_ref, halo_ref, g2_ref, g3_ref, w_ref, sc_ref, o_ref, tail_ref, buf_ref, ta_ref, tb_ref,
                 *, pos0, halo_is_state):
    tm = x_ref.shape[1]
    d = x_ref.shape[2]
    grp = d // len(POOL_WINDOWS)
    n = tm + POOL_HALO
    j = pl.program_id(1)
    pos = pos0 + j * tm + lax.broadcasted_iota(jnp.int32, (tm, 1), 0)
    for bi in range(x_ref.shape[0]):
        x = x_ref[bi]
        h = _rms(x, g2_ref[...])
        if halo_is_state:
            halo = halo_ref[bi]
        else:
            halo = jnp.where(j == 0, 0.0, _rms(halo_ref[bi], g2_ref[...]))
        buf_ref[0:POOL_HALO, :] = halo
        buf_ref[POOL_HALO:, :] = h
        tail_ref[bi] = h[tm - tail_ref.shape[1]:, :]

        ta_ref[8:n, :] = buf_ref[8:n, :] + buf_ref[7:n - 1, :]
        tb_ref[16:n, grp:] = ta_ref[16:n, grp:] + ta_ref[14:n - 2, grp:]
        ta_ref[24:n, 2 * grp:] = tb_ref[24:n, 2 * grp:] + tb_ref[20:n - 4, 2 * grp:]
        s16 = ta_ref[32:n, 3 * grp:] + ta_ref[24:n - 8, 3 * grp:]
        totals = (ta_ref[POOL_HALO:n, 0:grp], tb_ref[POOL_HALO:n, grp:2 * grp],
                  ta_ref[POOL_HALO:n, 2 * grp:3 * grp], s16)

        outs = []
        for gi, win in enumerate(POOL_WINDOWS):
            cnt = jnp.minimum(pos + 1, win).astype(F32)
            mixed = totals[gi] / cnt - h[:, gi * grp:(gi + 1) * grp]
            outs.append(jnp.dot(mixed.astype(BF16), w_ref[gi], preferred_element_type=F32))
        mix = jnp.concatenate(outs, axis=-1) * sc_ref[...]
        o_ref[bi] = x + _rms(mix, g3_ref[...])


def _pool(x, state, g2, g3, w, scale, tm, pos0, per_step=1):
    b, s, d = x.shape
    per_halo = tm // POOL_HALO
    n_tail = min(tm, POOL_TAIL)
    tile = pl.BlockSpec((per_step, tm, d), lambda i, j: (i, j, 0))
    if state is None:
        halo = pl.BlockSpec((per_step, POOL_HALO, d), lambda i, j: (i, jnp.maximum(j * per_halo - 1, 0), 0))
    else:
        assert s == tm and state.shape == (b, POOL_HALO, d)
        halo = pl.BlockSpec((per_step, POOL_HALO, d), lambda i, j: (i, 0, 0))
    vec = _const_spec((1, d))
    buf = pltpu.VMEM((tm + POOL_HALO, d), F32)
    return pl.pallas_call(
        functools.partial(_pool_kernel, pos0=pos0, halo_is_state=state is not None),
        grid=(b // per_step, s // tm),
        in_specs=[tile, halo, vec, vec, _const_spec(w.shape), vec],
        out_specs=[tile, pl.BlockSpec((per_step, n_tail, d), lambda i, j: (i, 0, 0))],
        out_shape=[jax.ShapeDtypeStruct((b, s, d), F32), jax.ShapeDtypeStruct((b, n_tail, d), F32)],
        scratch_shapes=[buf, buf, buf],
        compiler_params=_params(2),
        name="pool",
    )(x, x if state is None else state, g2, g3, w, scale)


def kernel(x_prompt, x_sample, cache_k, cache_v, cache_logf, state_conv, state_pool, page_table,
           norm_g, ffn_w_gate, ffn_w_up, ffn_w_down, mix_w_in, fgate_b, conv_dw_w, conv_dw_b,
           conv_ln_g, conv_ln_b, mix_w_out, pool_w, pool_scale):
    b, s, d = x_prompt.shape
    bd, t_new, _ = x_sample.shape
    depth = norm_g.shape[0]
    d_att = N_HEADS * HEAD_DIM
    d_conv = d - d_att
    page = cache_k.shape[2]
    past_len = page_table.shape[1] * page
    tm_prompt = 512
    m_sample = bd * t_new
    t_pad = 8

    xp = x_prompt.reshape(b * s, d)
    xs = x_sample.reshape(m_sample, d)
    vec = lambda a: a.reshape(1, -1).astype(F32)

    stacks = (ffn_w_gate.astype(F32), ffn_w_up.astype(F32), ffn_w_down.astype(F32))
    halves = [(layer, half) for layer in range(depth) for half in range(2)]
    ffn_weights = [tuple(w[0, 0].astype(BF16) for w in stacks)]

    def ffn_both(xp, xs, layer, half, mix=None):
        gp, gq = vec(norm_g[layer, 4 * half]), vec(norm_g[layer, 4 * half + 1])
        n = halves.index((layer, half))
        convert = (stacks, halves[n + 1]) if n + 1 < len(halves) else None
        xp, xs, *converted = _ffn(xp, xs, gp, gq, *ffn_weights[n], 1024, mix, convert)
        ffn_weights.append(tuple(converted))
        return xp, xs

    r = jnp.arange(LANES)
    tri_incl = (r[:, None] <= r[None, :]).astype(F32)
    tri_group = tri_incl * (r[:, None] // t_new == r[None, :] // t_new).astype(F32)
    rp = jnp.arange(page)
    suffix_total = jnp.concatenate([(rp[:, None] > rp[None, :]).astype(F32), jnp.ones((page, page), F32)], axis=1)

    outs_p = {k: [] for k in ("k", "v", "lf", "conv", "pool")}
    outs_s = {k: [] for k in ("k", "v", "lf", "conv", "pool")}
    for layer in range(depth):
        g = norm_g[layer]
        xp, xs = ffn_both(xp, xs, layer, 0)
        mix = None
        if layer % 2 == 0:
            e = layer // 2
            w_in = mix_w_in[e]
            o = 3 * d_att + N_HEADS
            w_cat = jnp.concatenate(
                [w_in[:, :3 * d_att], w_in[:, o:], w_in[:, 3 * d_att:o],
                 jnp.zeros((d, LANES - N_HEADS), w_in.dtype)], axis=1).astype(BF16)
            bf = jnp.concatenate([fgate_b[e].astype(F32), jnp.zeros((LANES - N_HEADS,), F32)]).reshape(1, LANES)
            wa = mix_w_out[e, :d_att].astype(BF16)
            wc = mix_w_out[e, d_att:].astype(BF16)
            cw, cb = conv_dw_w[e].astype(F32), vec(conv_dw_b[e])
            lg, lb = vec(conv_ln_g[e]), vec(conv_ln_b[e])
            g2, g3 = vec(g[2]), vec(g[3])

            q, kt, vt, ktb, vb, lft, cv, u_tail, ccol, crow = _inproj(
                xp.reshape(b, s, d), g2, w_cat, bf, tri_incl, (cw, cb, lg, lb), tm_prompt, d_att, d_conv)
            att_p = _attn(q, ktb, vb, ccol, crow, 256).reshape(b * s, d_att)
            cv_p = cv.reshape(b * s, d_conv)
            outs_p["k"].append(kt.reshape(b, N_HEADS, HEAD_DIM, s).transpose(0, 3, 1, 2))
            outs_p["v"].append(vt.reshape(b, N_HEADS, HEAD_DIM, s).transpose(0, 3, 1, 2))
            outs_p["lf"].append(lft.transpose(0, 2, 1))
            outs_p["conv"].append(u_tail[:, CONV_HALO - (CONV_W - 1):])

            q, k, v, lft, u, ccol, crow = _inproj(
                xs.reshape(1, m_sample, d), g2, w_cat, bf, tri_group, None, m_sample, d_att, d_conv)
            lf = lft.reshape(N_HEADS, m_sample).T
            pad_rows = lambda a: jnp.pad(a.reshape(bd, t_new, d_att), ((0, 0), (0, 16 - t_new), (0, 0)))
            cqcol = ccol.reshape(bd, t_new * N_HEADS, 1)
            cqrow = jnp.pad(crow.reshape(N_HEADS, bd, t_new).transpose(1, 0, 2), ((0, 0), (0, 0), (0, 16 - t_new)))
            att = _decode(page_table, q.reshape(bd, t_new, d_att), pad_rows(k), pad_rows(v), cqcol, cqrow,
                          suffix_total, cache_k[e].transpose(0, 2, 3, 1), cache_v[e].transpose(0, 2, 3, 1),
                          cache_logf[e].astype(F32).transpose(0, 2, 1), 16)
            st = state_conv[e].astype(F32)
            cv = _conv(jnp.pad(u.reshape(bd, t_new, d_conv), ((0, 0), (0, t_pad - t_new), (0, 0))),
                       jnp.pad(st, ((0, 0), (CONV_HALO - (CONV_W - 1), 0), (0, 0))), cw, cb, lg, lb, 8)
            mix = (att_p, cv_p, att.reshape(m_sample, d_att), cv[:, :t_new].reshape(m_sample, d_conv), wa, wc, g3)
            outs_s["k"].append(k.reshape(bd, t_new, N_HEADS, HEAD_DIM))
            outs_s["v"].append(v.reshape(bd, t_new, N_HEADS, HEAD_DIM))
            outs_s["lf"].append(lf.reshape(bd, t_new, N_HEADS))
            outs_s["conv"].append(jnp.concatenate([st, u.reshape(bd, t_new, d_conv)], axis=1)[:, t_new:])
        else:
            o = layer // 2
            g2, g3 = vec(g[2]), vec(g[3])
            pw = pool_w[o].astype(BF16)
            sc = vec(pool_scale[o])
            n_keep = POOL_WINDOWS[-1] - 1
            xp3, tail = _pool(xp.reshape(b, s, d), None, g2, g3, pw, sc, 512, 0)
            xp = xp3.reshape(b * s, d)
            outs_p["pool"].append(tail[:, POOL_TAIL - n_keep:])
            st = state_pool[o].astype(F32)
            xs3, hs = _pool(jnp.pad(xs.reshape(bd, t_new, d), ((0, 0), (0, t_pad - t_new), (0, 0))),
                            jnp.pad(st, ((0, 0), (POOL_HALO - n_keep, 0), (0, 0))), g2, g3, pw, sc, t_pad, past_len, 8)
            xs = xs3[:, :t_new].reshape(m_sample, d)
            outs_s["pool"].append(jnp.concatenate([st, hs[:, :t_new]], axis=1)[:, t_new:])
        xp, xs = ffn_both(xp, xs, layer, 1, mix)

    stack = lambda l: jnp.stack(l)
    return (xp.reshape(b, s, d), xs.reshape(bd, t_new, d),
            stack(outs_p["k"]), stack(outs_p["v"]), stack(outs_p["lf"]), stack(outs_p["conv"]),
            stack(outs_p["pool"]),
            stack(outs_s["k"]), stack(outs_s["v"]), stack(outs_s["lf"]), stack(outs_s["conv"]),
            stack(outs_s["pool"]))
```

```python
import functools

import jax
import jax.numpy as jnp
from jax import lax
from jax.experimental import pallas as pl
from jax.experimental.pallas import tpu as pltpu

F32 = jnp.float32
BF16 = jnp.bfloat16

RMS_EPS = 1e-6
LN_EPS = 1e-5
NEG_INF = -1e30
LOG2E = 1.4426950408889634

LANES = 128
HEAD_DIM = 64
N_HEADS = 8
Q_SCALE = HEAD_DIM ** -0.5 * LOG2E
CONV_W = 31
CONV_HALO = 32
POOL_WINDOWS = (2, 4, 8, 16)
POOL_HALO = 32
POOL_TAIL = 16
POOL_BLOCK = 256
VMEM_LIMIT_BYTES = 56 * 1024 * 1024
FFN_VMEM_LIMIT_BYTES = 61 * 1024 * 1024


def _params(n_axes, vmem_limit_bytes=VMEM_LIMIT_BYTES):
    return pltpu.CompilerParams(dimension_semantics=("arbitrary",) * n_axes,
                                vmem_limit_bytes=vmem_limit_bytes)


def _const_spec(shape, single_buffer=False):
    zeros = (0,) * len(shape)
    if single_buffer:
        return pl.BlockSpec(shape, lambda *_: zeros, pipeline_mode=pl.Buffered(1))
    return pl.BlockSpec(shape, lambda *_: zeros)


def _rms(x, g):
    ms = jnp.mean(x * x, axis=-1, keepdims=True)
    return x * lax.rsqrt(ms + RMS_EPS) * g


def _sigmoid(x):
    return 1.0 / (1.0 + jnp.exp(-x))


def _log_sigmoid(x):
    return jnp.minimum(x, 0.0) - jnp.log1p(jnp.exp(-jnp.abs(x)))


def _split3(x):
    hi = x.astype(BF16).astype(F32)
    r = x - hi
    mid = r.astype(BF16).astype(F32)
    return hi, mid, r - mid


def _dot3(x, w):
    hi, mid, lo = _split3(x)
    d = lambda a: jnp.dot(a, w, preferred_element_type=F32)
    return d(hi) + d(mid) + d(lo)


def _pool_mix(h, halo, pos, w_ref, sc_ref, buf_ref, ta_ref, tb_ref):
    rows, d = h.shape
    grp = d // len(POOL_WINDOWS)
    n = rows + POOL_HALO
    buf_ref[0:POOL_HALO, :] = halo
    buf_ref[POOL_HALO:n, :] = h
    ta_ref[8:n, :] = buf_ref[8:n, :] + buf_ref[7:n - 1, :]
    tb_ref[16:n, grp:] = ta_ref[16:n, grp:] + ta_ref[14:n - 2, grp:]
    ta_ref[24:n, 2 * grp:] = tb_ref[24:n, 2 * grp:] + tb_ref[20:n - 4, 2 * grp:]
    s16 = ta_ref[32:n, 3 * grp:] + ta_ref[24:n - 8, 3 * grp:]
    totals = (ta_ref[POOL_HALO:n, 0:grp], tb_ref[POOL_HALO:n, grp:2 * grp],
              ta_ref[POOL_HALO:n, 2 * grp:3 * grp], s16)
    outs = []
    for gi, win in enumerate(POOL_WINDOWS):
        cnt = jnp.minimum(pos + 1, win).astype(F32)
        mixed = totals[gi] / cnt - h[:, gi * grp:(gi + 1) * grp]
        outs.append(jnp.dot(mixed.astype(BF16), w_ref[gi], preferred_element_type=F32))
    return jnp.concatenate(outs, axis=-1) * sc_ref[...]


def _convert_slice_copies(i, slot, src_hbm, which, dst_hbm, stage_in, stage_out, sem_in, sem_out, n_slices):
    ins, outs = [], []
    for k in range(3):
        rows = dst_hbm[k].shape[0] // n_slices
        ins.append(pltpu.make_async_copy(src_hbm[k].at[which[0], which[1], pl.ds(i * rows, rows)],
                                         stage_in[k].at[slot], sem_in.at[slot, k]))
        outs.append(pltpu.make_async_copy(stage_out[k].at[slot], dst_hbm[k].at[pl.ds(i * rows, rows)],
                                          sem_out.at[slot, k]))
    return ins, outs


def _ffn_kernel(*refs, chunk, n_prompt, mix, pool, convert):
    xp_ref, xs_ref, gpre_ref, gpost_ref, wg_ref, wu_ref, wd_ref = refs[:7]
    refs = refs[7:]
    if mix:
        attp_ref, cvp_ref, atts_ref, cvs_ref, wa_ref, wc_ref, gmix_ref = refs[:7]
        refs = refs[7:]
    if pool:
        halo_ref, g2p_ref, g3p_ref, pw_ref, psc_ref = refs[:5]
        refs = refs[5:]
    if convert:
        src_hbm, refs = refs[:3], refs[3:]
    op_ref, os_ref = refs[:2]
    refs = refs[2:]
    if pool:
        tail_ref, refs = refs[0], refs[1:]
    if convert:
        dst_hbm, refs = refs[:3], refs[3:]
    a_ref, refs = refs[0], refs[1:]
    if pool:
        pool_bufs, refs = refs[:3], refs[3:]
    if convert:
        stage_in, stage_out, (sem_in, sem_out) = refs[0:3], refs[3:6], refs[6:8]
    i = pl.program_id(0)

    def slice_copies(step):
        return _convert_slice_copies(step, lax.rem(step, 2), src_hbm, convert, dst_hbm, stage_in, stage_out,
                                     sem_in, sem_out, n_prompt)

    def convert_slice():
        @pl.when(i == 0)
        def _():
            for c in slice_copies(i)[0]:
                c.start()

        @pl.when(i + 1 < n_prompt)
        def _():
            for c in slice_copies(i + 1)[0]:
                c.start()

        ins, outs = slice_copies(i)
        for c in ins:
            c.wait()

        @pl.when(i >= 2)
        def _():
            for c in slice_copies(i - 2)[1]:
                c.wait()

        slot = lax.rem(i, 2)
        for k in range(3):
            stage_out[k][slot] = stage_in[k][slot].astype(BF16)
        for c in outs:
            c.start()

    def pooled(x):
        rows = x.shape[0]
        tile_in_seq = lax.rem(i, pool)
        halo = jnp.where(tile_in_seq == 0, 0.0, _rms(halo_ref[...], g2p_ref[...]))
        out = []
        for r0 in range(0, rows, POOL_BLOCK):
            xb = x[r0:r0 + POOL_BLOCK]
            h = _rms(xb, g2p_ref[...])
            pos = tile_in_seq * rows + r0 + lax.broadcasted_iota(jnp.int32, (POOL_BLOCK, 1), 0)
            out.append(xb + _rms(_pool_mix(h, halo, pos, pw_ref, psc_ref, *pool_bufs), g3p_ref[...]))
            halo = h[POOL_BLOCK - POOL_HALO:]
        tail_ref[0] = h[POOL_BLOCK - POOL_TAIL:]
        return jnp.concatenate(out, axis=0)

    def update(x_ref, o_ref, att_ref=None, cv_ref=None, with_pool=False):
        rows = x_ref.shape[0]
        x = x_ref[...]
        if mix:
            mo = jnp.dot(att_ref[...].astype(BF16), wa_ref[...], preferred_element_type=F32)
            mo = mo + jnp.dot(cv_ref[...].astype(BF16), wc_ref[...], preferred_element_type=F32)
            x = x + _rms(mo, gmix_ref[...])
        if with_pool:
            x = pooled(x)
        h = _rms(x, gpre_ref[...]).astype(BF16)
        d_ff = wg_ref.shape[1]
        for c0 in range(0, d_ff, chunk):
            sl = slice(c0, min(c0 + chunk, d_ff))
            g = jnp.dot(h, wg_ref[:, sl], preferred_element_type=F32)
            u = jnp.dot(h, wu_ref[:, sl], preferred_element_type=F32)
            a_ref[0:rows, sl] = (g * _sigmoid(g) * u).astype(BF16)
        y = jnp.dot(a_ref[0:rows, :], wd_ref[...], preferred_element_type=F32)
        o_ref[...] = x + 0.5 * _rms(y, gpost_ref[...])

    @pl.when(i < n_prompt)
    def _():
        if convert:
            convert_slice()
        update(xp_ref, op_ref, *((attp_ref, cvp_ref) if mix else ()), with_pool=bool(pool))

    @pl.when(i == n_prompt)
    def _():
        if convert:
            for step in (n_prompt - 2, n_prompt - 1):
                for c in slice_copies(step)[1]:
                    c.wait()
        update(xs_ref, os_ref, *((atts_ref, cvs_ref) if mix else ()))


def _ffn(xp, xs, g_pre, g_post, wg, wu, wd, tm, mix=None, pool=None, convert=None):
    m, d = xp.shape
    d_ff = wg.shape[1]
    n_prompt = m // tm
    tile = lambda width: pl.BlockSpec((tm, width), lambda i: (jnp.minimum(i, n_prompt - 1), 0))
    hbm = pl.BlockSpec(memory_space=pl.ANY)
    in_specs = [tile(d), _const_spec(xs.shape), _const_spec((1, d)), _const_spec((1, d)),
                _const_spec(wg.shape, True), _const_spec(wu.shape, True), _const_spec(wd.shape, True)]
    out_specs = [tile(d), _const_spec(xs.shape)]
    out_shape = [jax.ShapeDtypeStruct((m, d), F32), jax.ShapeDtypeStruct(xs.shape, F32)]
    scratch = [pltpu.VMEM((tm, d_ff), BF16)]
    operands = [xp, xs, g_pre, g_post, wg, wu, wd]
    if mix is not None:
        att_p, cv_p, att_s, cv_s, w_att, w_cv, _ = mix
        in_specs += [tile(att_p.shape[1]), tile(cv_p.shape[1]), _const_spec(att_s.shape), _const_spec(cv_s.shape),
                     _const_spec(w_att.shape, True), _const_spec(w_cv.shape, True), _const_spec((1, d))]
        operands += list(mix)
    tiles_per_seq = None
    if pool is not None:
        seq_len, g2p, g3p, pw, psc = pool
        tiles_per_seq = seq_len // tm
        per_halo = tm // POOL_HALO
        in_specs += [pl.BlockSpec((POOL_HALO, d),
                                  lambda i: (jnp.maximum(jnp.minimum(i, n_prompt - 1) * per_halo - 1, 0), 0)),
                     _const_spec((1, d)), _const_spec((1, d)), _const_spec(pw.shape, True), _const_spec((1, d))]
        operands += [xp, g2p, g3p, pw, psc]
        out_specs += [pl.BlockSpec((1, POOL_TAIL, d),
                                   lambda i: (jnp.minimum(i, n_prompt - 1) // tiles_per_seq, 0, 0))]
        out_shape += [jax.ShapeDtypeStruct((m // seq_len, POOL_TAIL, d), F32)]
        scratch += [pltpu.VMEM((POOL_BLOCK + POOL_HALO, d), F32)] * 3
    if convert is not None:
        stacks, which = convert
        in_specs += [hbm] * 3
        operands += list(stacks)
        out_specs += [hbm] * 3
        shapes = [(d, d_ff), (d, d_ff), (d_ff, d)]
        out_shape += [jax.ShapeDtypeStruct(sh, BF16) for sh in shapes]
        scratch += [pltpu.VMEM((2, r // n_prompt, c), F32) for r, c in shapes]
        scratch += [pltpu.VMEM((2, r // n_prompt, c), BF16) for r, c in shapes]
        scratch += [pltpu.SemaphoreType.DMA((2, 3)), pltpu.SemaphoreType.DMA((2, 3))]
    return pl.pallas_call(
        functools.partial(_ffn_kernel, chunk=256, n_prompt=n_prompt, mix=mix is not None, pool=tiles_per_seq,
                          convert=None if convert is None else convert[1]),
        grid=(n_prompt + 1,),
        in_specs=in_specs,
        out_specs=out_specs,
        out_shape=out_shape,
        scratch_shapes=scratch,
        compiler_params=_params(1, FFN_VMEM_LIMIT_BYTES),
        name="ffn",
    )(*operands)


def _ln_swish(y, g, b):
    mu = jnp.mean(y, axis=-1, keepdims=True)
    d = y - mu
    var = jnp.mean(d * d, axis=-1, keepdims=True)
    yn = d * lax.rsqrt(var + LN_EPS) * g + b
    return yn * _sigmoid(yn)


def _causal_conv(buf_ref, rot_ref, w_ref, b_ref, y_ref, *, t0, tm, rows):
    d_conv = buf_ref.shape[1]
    base = CONV_HALO - (CONV_W - 1)
    n_rot = tm + CONV_HALO - 8
    for r in range(1, 8):
        rot_ref[r - 1, t0:t0 + n_rot, :] = buf_ref[t0 + r:t0 + r + n_rot, :]
    for cb in range(d_conv // LANES):
        ls = slice(cb * LANES, (cb + 1) * LANES)
        for rb in range(tm // rows):
            acc = jnp.zeros((rows, LANES), F32)
            for tap in range(CONV_W):
                a, r = divmod(base + tap, 8)
                r0 = t0 + rb * rows + 8 * a
                src = buf_ref[r0:r0 + rows, ls] if r == 0 else rot_ref[r - 1, r0:r0 + rows, ls]
                acc = acc + w_ref[tap:tap + 1, ls] * src
            y_ref[t0 + rb * rows:t0 + (rb + 1) * rows, ls] = acc + b_ref[:, ls]


def _conv_scratch(tm, d_conv):
    return [pltpu.VMEM((tm + CONV_HALO, d_conv), F32), pltpu.VMEM((7, tm + CONV_HALO - 8, d_conv), F32),
            pltpu.VMEM((tm, d_conv), F32)]


def _inproj_kernel(x_ref, g_ref, w_ref, bf_ref, tri_ref, *refs, d_att, d_conv, prompt, sub):
    if prompt:
        (cw_ref, cb_ref, lg_ref, lb_ref,
         q_ref, kt_ref, vt_ref, ktb_ref, vb_ref, lft_ref, cv_ref, utail_ref, ccol_ref, crow_ref,
         carry_ref, buf_ref, rot_ref, y_ref) = refs
    else:
        q_ref, k_ref, v_ref, lft_ref, u_ref, ccol_ref, crow_ref, carry_ref = refs
    tm = x_ref.shape[1]
    j = pl.program_id(1)

    @pl.when(j == 0)
    def _():
        carry_ref[...] = jnp.zeros_like(carry_ref)
        if prompt:
            buf_ref[0:CONV_HALO, :] = jnp.zeros((CONV_HALO, d_conv), F32)

    if prompt:
        @pl.when(j > 0)
        def _():
            buf_ref[0:CONV_HALO, :] = buf_ref[tm:tm + CONV_HALO, :]

    carry = carry_ref[...]
    tri = tri_ref[...]
    for t0 in range(0, tm, sub):
        rs = slice(t0, t0 + sub)
        h = _rms(x_ref[0, rs, :], g_ref[...]).astype(BF16)
        z = jnp.dot(h, w_ref[...], preferred_element_type=F32)
        k = z[:, d_att:2 * d_att]
        v = z[:, 2 * d_att:3 * d_att]
        o = 3 * d_att
        a = z[:, o:o + d_conv]
        gate = z[:, o + d_conv:o + 2 * d_conv]
        fg = z[:, o + 2 * d_conv:o + 2 * d_conv + LANES]
        q_ref[0, rs, :] = (z[:, :d_att] * Q_SCALE).astype(BF16)
        u = a * _sigmoid(gate)
        if prompt:
            kt = k.T
            kt_ref[0, :, rs] = kt
            vt_ref[0, :, rs] = v.T
            ktb_ref[0, :, rs] = kt.astype(BF16)
            vb_ref[0, rs, :] = v.astype(BF16)
            buf_ref[CONV_HALO + t0:CONV_HALO + t0 + sub, :] = u
            _causal_conv(buf_ref, rot_ref, cw_ref, cb_ref, y_ref, t0=t0, tm=sub, rows=min(sub, 128))
            cv_ref[0, rs, :] = _ln_swish(y_ref[rs, :], lg_ref[...], lb_ref[...]).astype(cv_ref.dtype)
        else:
            k_ref[0, rs, :] = k
            v_ref[0, rs, :] = v
            u_ref[0, rs, :] = u
        lf = _log_sigmoid(fg + bf_ref[...])
        lft = lf.T[:N_HEADS]
        lft_ref[0, :, rs] = lft

        chunks = []
        for c in range(sub // LANES):
            cs = _dot3(lft[:, c * LANES:(c + 1) * LANES], tri) + carry
            chunks.append(cs)
            carry = jnp.broadcast_to(cs[:, LANES - 1:LANES], cs.shape)
        crow = jnp.concatenate(chunks, axis=1) * LOG2E
        crow_ref[0, :, rs] = crow
        cfull = jnp.concatenate([crow, jnp.zeros((LANES - N_HEADS, sub), F32)], axis=0)
        ccol_ref[0, rs, :] = cfull.T[:, :N_HEADS]
    carry_ref[...] = carry

    if prompt:
        @pl.when(j == pl.num_programs(1) - 1)
        def _():
            utail_ref[0] = buf_ref[tm:tm + CONV_HALO, :]


def _inproj(x, g, w, bf, tri, conv, tm, d_att, d_conv):
    b, s, d = x.shape
    prompt = conv is not None
    tok = lambda width: pl.BlockSpec((1, tm, width), lambda i, j: (i, j, 0))
    tr = lambda rows: pl.BlockSpec((1, rows, tm), lambda i, j: (i, 0, j))
    shp = lambda width, dt: jax.ShapeDtypeStruct((b, s, width), dt)
    sht = lambda rows, dt: jax.ShapeDtypeStruct((b, rows, s), dt)
    in_specs = [tok(d), _const_spec((1, d)), _const_spec(w.shape, True), _const_spec((1, LANES)),
                _const_spec((LANES, LANES))]
    scratch = [pltpu.VMEM((N_HEADS, LANES), F32)]
    if prompt:
        in_specs += [_const_spec(conv[0].shape)] + [_const_spec((1, d_conv))] * 3
        out_specs = [tok(d_att), tr(d_att), tr(d_att), tr(d_att), tok(d_att), tr(N_HEADS), tok(d_conv),
                     pl.BlockSpec((1, CONV_HALO, d_conv), lambda i, j: (i, 0, 0))]
        out_shape = [shp(d_att, BF16), sht(d_att, F32), sht(d_att, F32), sht(d_att, BF16), shp(d_att, BF16),
                     sht(N_HEADS, F32), shp(d_conv, BF16), jax.ShapeDtypeStruct((b, CONV_HALO, d_conv), F32)]
        scratch += _conv_scratch(tm, d_conv)
    else:
        out_specs = [tok(d_att), tok(d_att), tok(d_att), tr(N_HEADS), tok(d_conv)]
        out_shape = [shp(d_att, BF16), shp(d_att, F32), shp(d_att, F32), sht(N_HEADS, F32), shp(d_conv, F32)]
    return pl.pallas_call(
        functools.partial(_inproj_kernel, d_att=d_att, d_conv=d_conv, prompt=prompt, sub=min(tm, 256)),
        grid=(b, s // tm),
        in_specs=in_specs,
        out_specs=out_specs + [tok(N_HEADS), tr(N_HEADS)],
        out_shape=out_shape + [shp(N_HEADS, F32), sht(N_HEADS, F32)],
        scratch_shapes=scratch,
        compiler_params=_params(2),
        name="inproj",
    )(x, g, w, bf, tri, *(conv or ()))


def _attn_kernel(q_ref, kt_ref, v_ref, ccol_ref, crow_ref, o_ref, *, tq):
    s_len = q_ref.shape[1]
    pair = pl.program_id(1)
    row = lax.broadcasted_iota(jnp.int32, (tq, tq), 0)
    col = lax.broadcasted_iota(jnp.int32, (tq, tq), 1)
    causal = col <= row
    low = lax.broadcasted_iota(jnp.int32, (tq, LANES), 1) < HEAD_DIM
    ccol = ccol_ref[0]
    head_lane = lax.broadcasted_iota(jnp.int32, ccol.shape, 1)
    cq_all, ck_all = [], []
    for hh in range(2):
        head = 2 * pair + hh
        cq_all.append(jnp.sum(jnp.where(head_lane == head, ccol, 0.0), axis=-1, keepdims=True))
        ck_all.append(crow_ref[0, pl.ds(head, 1), :])

    def scores(i, hh):
        k0 = i * tq
        q2 = q_ref[0, k0:k0 + tq, :]
        zero = jnp.zeros_like(q2)
        qh = jnp.where(low, q2, zero) if hh == 0 else jnp.where(low, zero, q2)
        s_diag = jnp.dot(qh, kt_ref[0, :, k0:k0 + tq], preferred_element_type=F32)
        s_off = jnp.dot(qh, kt_ref[0, :, :k0], preferred_element_type=F32) if i > 0 else None
        return s_diag, s_off

    def finish(i, hh, s_diag, s_off):
        k0 = i * tq
        cq = cq_all[hh][k0:k0 + tq]
        ck = ck_all[hh]
        s_diag = jnp.where(causal, s_diag - ck[:, k0:k0 + tq], NEG_INF)
        m = jnp.max(s_diag, axis=-1, keepdims=True)
        if i > 0:
            s_off = s_off - ck[:, :k0]
            m = jnp.maximum(m, jnp.max(s_off, axis=-1, keepdims=True))
        shift = cq - (m + cq)
        p_diag = jnp.exp2(s_diag + shift)
        l = jnp.sum(p_diag, axis=-1, keepdims=True)
        pv = jnp.dot(p_diag.astype(BF16), v_ref[0, k0:k0 + tq, :], preferred_element_type=F32)
        if i > 0:
            p_off = jnp.exp2(s_off + shift)
            l = l + jnp.sum(p_off, axis=-1, keepdims=True)
            pv = pv + jnp.dot(p_off.astype(BF16), v_ref[0, :k0, :], preferred_element_type=F32)
        return pv * (1.0 / l)

    blocks = [(i, hh) for i in range(s_len // tq) for hh in range(2)]
    nxt = scores(*blocks[0])
    halves = []
    for n, (i, hh) in enumerate(blocks):
        cur = nxt
        if n + 1 < len(blocks):
            nxt = scores(*blocks[n + 1])
        halves.append(finish(i, hh, *cur))
        if hh == 1:
            o_ref[0, i * tq:(i + 1) * tq, :] = jnp.where(low, halves[0], halves[1]).astype(BF16)
            halves = []


def _attn(q, ktb, vb, ccol, crow, tq):
    b, s, d_att = q.shape
    cols = pl.BlockSpec((1, s, LANES), lambda i, j: (i, 0, j))
    return pl.pallas_call(
        functools.partial(_attn_kernel, tq=tq),
        grid=(b, d_att // LANES),
        in_specs=[cols, pl.BlockSpec((1, LANES, s), lambda i, j: (i, j, 0)), cols,
                  pl.BlockSpec((1, s, N_HEADS), lambda i, j: (i, 0, 0)),
                  pl.BlockSpec((1, N_HEADS, s), lambda i, j: (i, 0, 0))],
        out_specs=cols,
        out_shape=jax.ShapeDtypeStruct((b, s, d_att), BF16),
        compiler_params=_params(2),
        name="attn",
    )(q, ktb, vb, ccol, crow)


def _decode_kernel(pt_ref, q_ref, kn_ref, vn_ref, cqcol_ref, cqrow_ref, us_ref, kc_ref, vc_ref, lc_ref, o_ref,
                   kbuf, vbuf, lbuf, sem, qbd_ref, m_ref, l_ref, acc_ref, carry_ref, *, n_group, n_steps):
    t = pl.program_id(0)
    js = lax.rem(t, n_steps)
    slot = lax.rem(t, 2)
    n_pages = pt_ref.shape[1]
    t_new = q_ref.shape[1]
    d_att = q_ref.shape[2]
    page = kbuf.shape[-1]
    n_rows = t_new * N_HEADS
    nt_dims = (((1,), (1,)), ((), ()))
    cqcol = cqcol_ref[0]

    def page_copies(step, to_slot, pages):
        out = []
        for g in range(n_group):
            p = pages(step, g)
            out.append(pltpu.make_async_copy(kc_ref.at[p], kbuf.at[to_slot, g], sem.at[to_slot, 0]))
            out.append(pltpu.make_async_copy(vc_ref.at[p], vbuf.at[to_slot, g], sem.at[to_slot, 1]))
            out.append(pltpu.make_async_copy(lc_ref.at[p], lbuf.at[to_slot, g], sem.at[to_slot, 2]))
        return out

    def table_page(step, g):
        return pt_ref[lax.div(step, n_steps), n_pages - 1 - (lax.rem(step, n_steps) * n_group + g)]

    @pl.when(t == 0)
    def _():
        for c in page_copies(t, slot, table_page):
            c.start()

    @pl.when(t + 1 < pl.num_programs(0))
    def _():
        for c in page_copies(t + 1, 1 - slot, table_page):
            c.start()

    @pl.when(js == 0)
    def _():
        q = q_ref[0].astype(F32)
        sub = lax.broadcasted_iota(jnp.int32, (N_HEADS, d_att), 0)
        lane = lax.broadcasted_iota(jnp.int32, (N_HEADS, d_att), 1)
        own = (lane // HEAD_DIM) == sub
        rows = [jnp.where(own, jnp.broadcast_to(q[i:i + 1, :], (N_HEADS, d_att)), 0.0) for i in range(t_new)]
        qbd = jnp.concatenate(rows, axis=0).astype(BF16)
        qbd_ref[...] = qbd
        s = lax.dot_general(qbd, kn_ref[0].astype(BF16), nt_dims, preferred_element_type=F32)
        cqk = jnp.concatenate([cqrow_ref[0]] * t_new, axis=0)
        r = lax.broadcasted_iota(jnp.int32, s.shape, 0)
        c = lax.broadcasted_iota(jnp.int32, s.shape, 1)
        s = jnp.where(c <= r // N_HEADS, s + (cqcol - cqk), NEG_INF)
        m = jnp.max(s, axis=-1, keepdims=True)
        pr = jnp.exp2(s - m)
        m_ref[...] = m
        l_ref[...] = jnp.sum(pr, axis=-1, keepdims=True)
        acc_ref[...] = jnp.dot(pr.astype(BF16), vn_ref[0].astype(BF16), preferred_element_type=F32)
        carry_ref[...] = jnp.zeros_like(carry_ref)

    for c in page_copies(t, slot, lambda step, g: 0):
        c.wait()

    qbd = qbd_ref[...]
    carry = carry_ref[...]
    us = us_ref[...]
    s_parts = []
    r_all = _dot3(lbuf[slot].reshape(n_group * N_HEADS, page), us)
    for g in range(n_group):
        r = r_all[g * N_HEADS:(g + 1) * N_HEADS]
        suffix = (r[:, :page] + carry) * LOG2E
        carry = carry + r[:, page:]
        bias = jnp.concatenate([suffix] * t_new, axis=0) + cqcol
        kt = kbuf[slot, g].reshape(d_att, page).astype(BF16)
        s_parts.append(jnp.dot(qbd, kt, preferred_element_type=F32) + bias)
    carry_ref[...] = carry
    s = jnp.concatenate(s_parts, axis=1)
    m = m_ref[...]
    m_new = jnp.maximum(m, jnp.max(s, axis=-1, keepdims=True))
    alpha = jnp.exp2(m - m_new)
    pr = jnp.exp2(s - m_new)
    m_ref[...] = m_new
    l_ref[...] = alpha * l_ref[...] + jnp.sum(pr, axis=-1, keepdims=True)
    prb = pr.astype(BF16)
    acc = alpha * acc_ref[...]
    for g in range(n_group):
        vt = vbuf[slot, g].reshape(d_att, page).astype(BF16)
        acc = acc + lax.dot_general(prb[:, g * page:(g + 1) * page], vt, nt_dims, preferred_element_type=F32)
    acc_ref[...] = acc

    @pl.when(js == n_steps - 1)
    def _():
        o = acc * (1.0 / l_ref[...])
        sub = lax.broadcasted_iota(jnp.int32, (N_HEADS, d_att), 0)
        lane = lax.broadcasted_iota(jnp.int32, (N_HEADS, d_att), 1)
        own = (lane // HEAD_DIM) == sub
        rows = [jnp.sum(jnp.where(own, o[i * N_HEADS:(i + 1) * N_HEADS], 0.0), axis=0, keepdims=True)
                for i in range(t_new)]
        o_ref[0] = jnp.concatenate(rows, axis=0).astype(o_ref.dtype)


def _decode(page_table, q, kn, vn, cqcol, cqrow, us, cache_k, cache_v, cache_lf, n_group):
    bd, t_new, d_att = q.shape
    n_pages = page_table.shape[1]
    page = cache_k.shape[3]
    n_rows = t_new * N_HEADS
    n_steps = n_pages // n_group
    per_b = lambda shape: pl.BlockSpec((1,) + shape, lambda t, pt: (t // n_steps,) + (0,) * len(shape))
    hbm = pl.BlockSpec(memory_space=pl.ANY)
    grid_spec = pltpu.PrefetchScalarGridSpec(
        num_scalar_prefetch=1,
        grid=(bd * n_steps,),
        in_specs=[per_b((t_new, d_att)), per_b(kn.shape[1:]), per_b(vn.shape[1:]), per_b((n_rows, 1)),
                  per_b(cqrow.shape[1:]), pl.BlockSpec(us.shape, lambda t, pt: (0, 0)), hbm, hbm, hbm],
        out_specs=per_b((t_new, d_att)),
        scratch_shapes=[pltpu.VMEM((2, n_group, N_HEADS, HEAD_DIM, page), F32),
                        pltpu.VMEM((2, n_group, N_HEADS, HEAD_DIM, page), F32),
                        pltpu.VMEM((2, n_group, N_HEADS, page), F32),
                        pltpu.SemaphoreType.DMA((2, 3)),
                        pltpu.VMEM((n_rows, d_att), BF16), pltpu.VMEM((n_rows, 1), F32),
                        pltpu.VMEM((n_rows, 1), F32), pltpu.VMEM((n_rows, d_att), F32),
                        pltpu.VMEM((N_HEADS, page), F32)],
    )
    return pl.pallas_call(
        functools.partial(_decode_kernel, n_group=n_group, n_steps=n_steps),
        grid_spec=grid_spec,
        out_shape=jax.ShapeDtypeStruct((bd, t_new, d_att), BF16),
        compiler_params=_params(1),
        name="decode",
    )(page_table, q, kn, vn, cqcol, cqrow, us, cache_k, cache_v, cache_lf)


def _conv_kernel(u_ref, state_ref, w_ref, b_ref, lg_ref, lb_ref, o_ref, buf_ref, rot_ref, y_ref):
    tm = u_ref.shape[1]
    for n in range(u_ref.shape[0]):
        buf_ref[0:CONV_HALO, :] = state_ref[n]
        buf_ref[CONV_HALO:, :] = u_ref[n]
        _causal_conv(buf_ref, rot_ref, w_ref, b_ref, y_ref, t0=0, tm=tm, rows=tm)
        o_ref[n] = _ln_swish(y_ref[...], lg_ref[...], lb_ref[...])


def _conv(u, state, w, b, lg, lb, per_step):
    bsz, tm, d_conv = u.shape
    assert state.shape == (bsz, CONV_HALO, d_conv)
    tile = pl.BlockSpec((per_step, tm, d_conv), lambda i: (i, 0, 0))
    vec = _const_spec((1, d_conv))
    return pl.pallas_call(
        _conv_kernel,
        grid=(bsz // per_step,),
        in_specs=[tile, pl.BlockSpec((per_step, CONV_HALO, d_conv), lambda i: (i, 0, 0)), _const_spec(w.shape),
                  vec, vec, vec],
        out_specs=tile,
        out_shape=jax.ShapeDtypeStruct((bsz, tm, d_conv), F32),
        scratch_shapes=_conv_scratch(tm, d_conv),
        compiler_params=_params(1),
        name="conv",
    )(u, state, w, b, lg, lb)


def _pool_kernel(x_ref, state_ref, g2_ref, g3_ref, w_ref, sc_ref, o_ref, h_ref, buf_ref, ta_ref, tb_ref, *, pos0):
    tm = x_ref.shape[1]
    pos = pos0 + lax.broadcasted_iota(jnp.int32, (tm, 1), 0)
    for bi in range(x_ref.shape[0]):
        x = x_ref[bi]
        h = _rms(x, g2_ref[...])
        h_ref[bi] = h
        mix = _pool_mix(h, state_ref[bi], pos, w_ref, sc_ref, buf_ref, ta_ref, tb_ref)
        o_ref[bi] = x + _rms(mix, g3_ref[...])


def _pool(x, state, g2, g3, w, scale, pos0, per_step):
    b, tm, d = x.shape
    assert state.shape == (b, POOL_HALO, d)
    tile = pl.BlockSpec((per_step, tm, d), lambda i: (i, 0, 0))
    vec = _const_spec((1, d))
    buf = pltpu.VMEM((tm + POOL_HALO, d), F32)
    return pl.pallas_call(
        functools.partial(_pool_kernel, pos0=pos0),
        grid=(b // per_step,),
        in_specs=[tile, pl.BlockSpec((per_step, POOL_HALO, d), lambda i: (i, 0, 0)), vec, vec,
                  _const_spec(w.shape), vec],
        out_specs=[tile, tile],
        out_shape=[jax.ShapeDtypeStruct((b, tm, d), F32), jax.ShapeDtypeStruct((b, tm, d), F32)],
        scratch_shapes=[buf, buf, buf],
        compiler_params=_params(1),
        name="pool",
    )(x, state, g2, g3, w, scale)


def kernel(x_prompt, x_sample, cache_k, cache_v, cache_logf, state_conv, state_pool, page_table,
           norm_g, ffn_w_gate, ffn_w_up, ffn_w_down, mix_w_in, fgate_b, conv_dw_w, conv_dw_b,
           conv_ln_g, conv_ln_b, mix_w_out, pool_w, pool_scale):
    b, s, d = x_prompt.shape
    bd, t_new, _ = x_sample.shape
    depth = norm_g.shape[0]
    d_att = N_HEADS * HEAD_DIM
    d_conv = d - d_att
    page = cache_k.shape[2]
    past_len = page_table.shape[1] * page
    tm_prompt = 512
    m_sample = bd * t_new
    t_pad = 8

    xp = x_prompt.reshape(b * s, d)
    xs = x_sample.reshape(m_sample, d)
    vec = lambda a: a.reshape(1, -1).astype(F32)

    stacks = (ffn_w_gate.astype(F32), ffn_w_up.astype(F32), ffn_w_down.astype(F32))
    halves = [(layer, half) for layer in range(depth) for half in range(2)]
    ffn_weights = [tuple(w[0, 0].astype(BF16) for w in stacks)]

    def ffn_both(xp, xs, layer, half, mix=None, pool=None):
        gp, gq = vec(norm_g[layer, 4 * half]), vec(norm_g[layer, 4 * half + 1])
        n = halves.index((layer, half))
        convert = (stacks, halves[n + 1]) if n + 1 < len(halves) else None
        xp, xs, *rest = _ffn(xp, xs, gp, gq, *ffn_weights[n], 1024, mix, pool, convert)
        tail = rest.pop(0) if pool is not None else None
        ffn_weights.append(tuple(rest))
        return xp, xs, tail

    r = jnp.arange(LANES)
    tri_incl = (r[:, None] <= r[None, :]).astype(F32)
    tri_group = tri_incl * (r[:, None] // t_new == r[None, :] // t_new).astype(F32)
    rp = jnp.arange(page)
    suffix_total = jnp.concatenate([(rp[:, None] > rp[None, :]).astype(F32), jnp.ones((page, page), F32)], axis=1)

    outs_p = {k: [] for k in ("k", "v", "lf", "conv", "pool")}
    outs_s = {k: [] for k in ("k", "v", "lf", "conv", "pool")}
    for layer in range(depth):
        g = norm_g[layer]
        xp, xs, _ = ffn_both(xp, xs, layer, 0)
        mix = pool = None
        if layer % 2 == 0:
            e = layer // 2
            w_in = mix_w_in[e]
            o = 3 * d_att + N_HEADS
            w_cat = jnp.concatenate(
                [w_in[:, :3 * d_att], w_in[:, o:], w_in[:, 3 * d_att:o],
                 jnp.zeros((d, LANES - N_HEADS), w_in.dtype)], axis=1).astype(BF16)
            bf = jnp.concatenate([fgate_b[e].astype(F32), jnp.zeros((LANES - N_HEADS,), F32)]).reshape(1, LANES)
            wa = mix_w_out[e, :d_att].astype(BF16)
            wc = mix_w_out[e, d_att:].astype(BF16)
            cw, cb = conv_dw_w[e].astype(F32), vec(conv_dw_b[e])
            lg, lb = vec(conv_ln_g[e]), vec(conv_ln_b[e])
            g2, g3 = vec(g[2]), vec(g[3])

            q, kt, vt, ktb, vb, lft, cv, u_tail, ccol, crow = _inproj(
                xp.reshape(b, s, d), g2, w_cat, bf, tri_incl, (cw, cb, lg, lb), tm_prompt, d_att, d_conv)
            att_p = _attn(q, ktb, vb, ccol, crow, 256).reshape(b * s, d_att)
            cv_p = cv.reshape(b * s, d_conv)
            outs_p["k"].append(kt.reshape(b, N_HEADS, HEAD_DIM, s).transpose(0, 3, 1, 2))
            outs_p["v"].append(vt.reshape(b, N_HEADS, HEAD_DIM, s).transpose(0, 3, 1, 2))
            outs_p["lf"].append(lft.transpose(0, 2, 1))
            outs_p["conv"].append(u_tail[:, CONV_HALO - (CONV_W - 1):])

            q, k, v, lft, u, ccol, crow = _inproj(
                xs.reshape(1, m_sample, d), g2, w_cat, bf, tri_group, None, m_sample, d_att, d_conv)
            lf = lft.reshape(N_HEADS, m_sample).T
            pad_rows = lambda a: jnp.pad(a.reshape(bd, t_new, d_att), ((0, 0), (0, 16 - t_new), (0, 0)))
            cqcol = ccol.reshape(bd, t_new * N_HEADS, 1)
            cqrow = jnp.pad(crow.reshape(N_HEADS, bd, t_new).transpose(1, 0, 2), ((0, 0), (0, 0), (0, 16 - t_new)))
            att = _decode(page_table, q.reshape(bd, t_new, d_att), pad_rows(k), pad_rows(v), cqcol, cqrow,
                          suffix_total, cache_k[e].transpose(0, 2, 3, 1), cache_v[e].transpose(0, 2, 3, 1),
                          cache_logf[e].astype(F32).transpose(0, 2, 1), 32)
            st = state_conv[e].astype(F32)
            cv = _conv(jnp.pad(u.reshape(bd, t_new, d_conv), ((0, 0), (0, t_pad - t_new), (0, 0))),
                       jnp.pad(st, ((0, 0), (CONV_HALO - (CONV_W - 1), 0), (0, 0))), cw, cb, lg, lb, 8)
            mix = (att_p, cv_p, att.reshape(m_sample, d_att), cv[:, :t_new].reshape(m_sample, d_conv), wa, wc, g3)
            outs_s["k"].append(k.reshape(bd, t_new, N_HEADS, HEAD_DIM))
            outs_s["v"].append(v.reshape(bd, t_new, N_HEADS, HEAD_DIM))
            outs_s["lf"].append(lf.reshape(bd, t_new, N_HEADS))
            outs_s["conv"].append(jnp.concatenate([st, u.reshape(bd, t_new, d_conv)], axis=1)[:, t_new:])
        else:
            o = layer // 2
            g2, g3 = vec(g[2]), vec(g[3])
            pw = pool_w[o].astype(BF16)
            sc = vec(pool_scale[o])
            n_keep = POOL_WINDOWS[-1] - 1
            pool = (s, g2, g3, pw, sc)
            st = state_pool[o].astype(F32)
            xs3, hs = _pool(jnp.pad(xs.reshape(bd, t_new, d), ((0, 0), (0, t_pad - t_new), (0, 0))),
                            jnp.pad(st, ((0, 0), (POOL_HALO - n_keep, 0), (0, 0))), g2, g3, pw, sc, past_len, 8)
            xs = xs3[:, :t_new].reshape(m_sample, d)
            outs_s["pool"].append(jnp.concatenate([st, hs[:, :t_new]], axis=1)[:, t_new:])
        xp, xs, tail = ffn_both(xp, xs, layer, 1, mix, pool)
        if pool is not None:
            outs_p["pool"].append(tail[:, POOL_TAIL - (POOL_WINDOWS[-1] - 1):])

    stack = lambda l: jnp.stack(l)
    return (xp.reshape(b, s, d), xs.reshape(bd, t_new, d),
            stack(outs_p["k"]), stack(outs_p["v"]), stack(outs_p["lf"]), stack(outs_p["conv"]),
            stack(outs_p["pool"]),
            stack(outs_s["k"]), stack(outs_s["v"]), stack(outs_s["lf"]), stack(outs_s["conv"]),
            stack(outs_s["pool"]))
```

```python
import functools

import jax
import jax.numpy as jnp
from jax import lax
from jax.experimental import pallas as pl
from jax.experimental.pallas import tpu as pltpu

F32 = jnp.float32
BF16 = jnp.bfloat16

RMS_EPS = 1e-6
LN_EPS = 1e-5
NEG_INF = -1e30
LOG2E = 1.4426950408889634

LANES = 128
HEAD_DIM = 64
N_HEADS = 8
Q_SCALE = HEAD_DIM ** -0.5 * LOG2E
CONV_W = 31
CONV_HALO = 32
POOL_WINDOWS = (2, 4, 8, 16)
POOL_HALO = 32
POOL_TAIL = 16
POOL_BLOCK = 256
VMEM_LIMIT_BYTES = 56 * 1024 * 1024
FFN_VMEM_LIMIT_BYTES = 61 * 1024 * 1024


def _params(n_axes, vmem_limit_bytes=VMEM_LIMIT_BYTES):
    return pltpu.CompilerParams(dimension_semantics=("arbitrary",) * n_axes,
                                vmem_limit_bytes=vmem_limit_bytes)


def _const_spec(shape, single_buffer=False):
    zeros = (0,) * len(shape)
    if single_buffer:
        return pl.BlockSpec(shape, lambda *_: zeros, pipeline_mode=pl.Buffered(1))
    return pl.BlockSpec(shape, lambda *_: zeros)


def _rms(x, g):
    ms = jnp.mean(x * x, axis=-1, keepdims=True)
    return x * lax.rsqrt(ms + RMS_EPS) * g


def _sigmoid(x):
    return 1.0 / (1.0 + jnp.exp(-x))


def _log_sigmoid(x):
    return jnp.minimum(x, 0.0) - jnp.log1p(jnp.exp(-jnp.abs(x)))


def _split3(x):
    hi = x.astype(BF16).astype(F32)
    r = x - hi
    mid = r.astype(BF16).astype(F32)
    return hi, mid, r - mid


def _dot3(x, w):
    hi, mid, lo = _split3(x)
    d = lambda a: jnp.dot(a, w, preferred_element_type=F32)
    return d(hi) + d(mid) + d(lo)


def _pool_mix(h, halo, pos, w_ref, sc_ref, buf_ref, ta_ref, tb_ref):
    rows, d = h.shape
    grp = d // len(POOL_WINDOWS)
    n = rows + POOL_HALO
    buf_ref[0:POOL_HALO, :] = halo
    buf_ref[POOL_HALO:n, :] = h
    ta_ref[8:n, :] = buf_ref[8:n, :] + buf_ref[7:n - 1, :]
    tb_ref[16:n, grp:] = ta_ref[16:n, grp:] + ta_ref[14:n - 2, grp:]
    ta_ref[24:n, 2 * grp:] = tb_ref[24:n, 2 * grp:] + tb_ref[20:n - 4, 2 * grp:]
    s16 = ta_ref[32:n, 3 * grp:] + ta_ref[24:n - 8, 3 * grp:]
    totals = (ta_ref[POOL_HALO:n, 0:grp], tb_ref[POOL_HALO:n, grp:2 * grp],
              ta_ref[POOL_HALO:n, 2 * grp:3 * grp], s16)
    outs = []
    for gi, win in enumerate(POOL_WINDOWS):
        cnt = jnp.minimum(pos + 1, win).astype(F32)
        mixed = totals[gi] / cnt - h[:, gi * grp:(gi + 1) * grp]
        outs.append(jnp.dot(mixed.astype(BF16), w_ref[gi], preferred_element_type=F32))
    return jnp.concatenate(outs, axis=-1) * sc_ref[...]


def _convert_slice_copies(i, slot, src_hbm, which, dst_hbm, stage_in, stage_out, sem_in, sem_out, n_slices):
    ins, outs = [], []
    for k in range(3):
        rows = dst_hbm[k].shape[0] // n_slices
        ins.append(pltpu.make_async_copy(src_hbm[k].at[which[0], which[1], pl.ds(i * rows, rows)],
                                         stage_in[k].at[slot], sem_in.at[slot, k]))
        outs.append(pltpu.make_async_copy(stage_out[k].at[slot], dst_hbm[k].at[pl.ds(i * rows, rows)],
                                          sem_out.at[slot, k]))
    return ins, outs


def _ffn_kernel(*refs, chunk, n_prompt, mix, pool, convert):
    xp_ref, xs_ref, gpre_ref, gpost_ref, wg_ref, wu_ref, wd_ref = refs[:7]
    refs = refs[7:]
    if mix:
        attp_ref, cvp_ref, atts_ref, cvs_ref, wa_ref, wc_ref, gmix_ref = refs[:7]
        refs = refs[7:]
    if pool:
        halo_ref, g2p_ref, g3p_ref, pw_ref, psc_ref = refs[:5]
        refs = refs[5:]
    if convert:
        src_hbm, refs = refs[:3], refs[3:]
    op_ref, os_ref = refs[:2]
    refs = refs[2:]
    if pool:
        tail_ref, refs = refs[0], refs[1:]
    if convert:
        dst_hbm, refs = refs[:3], refs[3:]
    a_ref, refs = refs[0], refs[1:]
    if pool:
        pool_bufs, refs = refs[:3], refs[3:]
    if convert:
        stage_in, stage_out, (sem_in, sem_out) = refs[0:3], refs[3:6], refs[6:8]
    i = pl.program_id(0)

    def slice_copies(step):
        return _convert_slice_copies(step, lax.rem(step, 2), src_hbm, convert, dst_hbm, stage_in, stage_out,
                                     sem_in, sem_out, n_prompt)

    def convert_slice():
        @pl.when(i == 0)
        def _():
            for c in slice_copies(i)[0]:
                c.start()

        @pl.when(i + 1 < n_prompt)
        def _():
            for c in slice_copies(i + 1)[0]:
                c.start()

        ins, outs = slice_copies(i)
        for c in ins:
            c.wait()

        @pl.when(i >= 2)
        def _():
            for c in slice_copies(i - 2)[1]:
                c.wait()

        slot = lax.rem(i, 2)
        for k in range(3):
            stage_out[k][slot] = stage_in[k][slot].astype(BF16)
        for c in outs:
            c.start()

    def pooled(x):
        rows = x.shape[0]
        tile_in_seq = lax.rem(i, pool)
        halo = jnp.where(tile_in_seq == 0, 0.0, _rms(halo_ref[...], g2p_ref[...]))
        out = []
        for r0 in range(0, rows, POOL_BLOCK):
            xb = x[r0:r0 + POOL_BLOCK]
            h = _rms(xb, g2p_ref[...])
            pos = tile_in_seq * rows + r0 + lax.broadcasted_iota(jnp.int32, (POOL_BLOCK, 1), 0)
            out.append(xb + _rms(_pool_mix(h, halo, pos, pw_ref, psc_ref, *pool_bufs), g3p_ref[...]))
            halo = h[POOL_BLOCK - POOL_HALO:]
        tail_ref[0] = h[POOL_BLOCK - POOL_TAIL:]
        return jnp.concatenate(out, axis=0)

    def update(x_ref, o_ref, att_ref=None, cv_ref=None, with_pool=False):
        rows = x_ref.shape[0]
        x = x_ref[...]
        if mix:
            mo = jnp.dot(att_ref[...].astype(BF16), wa_ref[...], preferred_element_type=F32)
            mo = mo + jnp.dot(cv_ref[...].astype(BF16), wc_ref[...], preferred_element_type=F32)
            x = x + _rms(mo, gmix_ref[...])
        if with_pool:
            x = pooled(x)
        h = _rms(x, gpre_ref[...]).astype(BF16)
        d_ff = wg_ref.shape[1]
        for c0 in range(0, d_ff, chunk):
            sl = slice(c0, min(c0 + chunk, d_ff))
            g = jnp.dot(h, wg_ref[:, sl], preferred_element_type=F32)
            u = jnp.dot(h, wu_ref[:, sl], preferred_element_type=F32)
            a_ref[0:rows, sl] = (g * _sigmoid(g) * u).astype(BF16)
        y = jnp.dot(a_ref[0:rows, :], wd_ref[...], preferred_element_type=F32)
        o_ref[...] = x + 0.5 * _rms(y, gpost_ref[...])

    @pl.when(i < n_prompt)
    def _():
        if convert:
            convert_slice()
        update(xp_ref, op_ref, *((attp_ref, cvp_ref) if mix else ()), with_pool=bool(pool))

    @pl.when(i == n_prompt)
    def _():
        if convert:
            for step in (n_prompt - 2, n_prompt - 1):
                for c in slice_copies(step)[1]:
                    c.wait()
        update(xs_ref, os_ref, *((atts_ref, cvs_ref) if mix else ()))


def _ffn(xp, xs, g_pre, g_post, wg, wu, wd, tm, mix=None, pool=None, convert=None):
    m, d = xp.shape
    d_ff = wg.shape[1]
    n_prompt = m // tm
    tile = lambda width: pl.BlockSpec((tm, width), lambda i: (jnp.minimum(i, n_prompt - 1), 0))
    hbm = pl.BlockSpec(memory_space=pl.ANY)
    in_specs = [tile(d), _const_spec(xs.shape), _const_spec((1, d)), _const_spec((1, d)),
                _const_spec(wg.shape, True), _const_spec(wu.shape, True), _const_spec(wd.shape, True)]
    out_specs = [tile(d), _const_spec(xs.shape)]
    out_shape = [jax.ShapeDtypeStruct((m, d), F32), jax.ShapeDtypeStruct(xs.shape, F32)]
    scratch = [pltpu.VMEM((tm, d_ff), BF16)]
    operands = [xp, xs, g_pre, g_post, wg, wu, wd]
    if mix is not None:
        att_p, cv_p, att_s, cv_s, w_att, w_cv, _ = mix
        in_specs += [tile(att_p.shape[1]), tile(cv_p.shape[1]), _const_spec(att_s.shape), _const_spec(cv_s.shape),
                     _const_spec(w_att.shape, True), _const_spec(w_cv.shape, True), _const_spec((1, d))]
        operands += list(mix)
    tiles_per_seq = None
    if pool is not None:
        seq_len, g2p, g3p, pw, psc = pool
        tiles_per_seq = seq_len // tm
        per_halo = tm // POOL_HALO
        in_specs += [pl.BlockSpec((POOL_HALO, d),
                                  lambda i: (jnp.maximum(jnp.minimum(i, n_prompt - 1) * per_halo - 1, 0), 0)),
                     _const_spec((1, d)), _const_spec((1, d)), _const_spec(pw.shape, True), _const_spec((1, d))]
        operands += [xp, g2p, g3p, pw, psc]
        out_specs += [pl.BlockSpec((1, POOL_TAIL, d),
                                   lambda i: (jnp.minimum(i, n_prompt - 1) // tiles_per_seq, 0, 0))]
        out_shape += [jax.ShapeDtypeStruct((m // seq_len, POOL_TAIL, d), F32)]
        scratch += [pltpu.VMEM((POOL_BLOCK + POOL_HALO, d), F32)] * 3
    if convert is not None:
        stacks, which = convert
        in_specs += [hbm] * 3
        operands += list(stacks)
        out_specs += [hbm] * 3
        shapes = [(d, d_ff), (d, d_ff), (d_ff, d)]
        out_shape += [jax.ShapeDtypeStruct(sh, BF16) for sh in shapes]
        scratch += [pltpu.VMEM((2, r // n_prompt, c), F32) for r, c in shapes]
        scratch += [pltpu.VMEM((2, r // n_prompt, c), BF16) for r, c in shapes]
        scratch += [pltpu.SemaphoreType.DMA((2, 3)), pltpu.SemaphoreType.DMA((2, 3))]
    return pl.pallas_call(
        functools.partial(_ffn_kernel, chunk=256, n_prompt=n_prompt, mix=mix is not None, pool=tiles_per_seq,
                          convert=None if convert is None else convert[1]),
        grid=(n_prompt + 1,),
        in_specs=in_specs,
        out_specs=out_specs,
        out_shape=out_shape,
        scratch_shapes=scratch,
        compiler_params=_params(1, FFN_VMEM_LIMIT_BYTES),
        name="ffn",
    )(*operands)


def _ln_swish(y, g, b):
    mu = jnp.mean(y, axis=-1, keepdims=True)
    d = y - mu
    var = jnp.mean(d * d, axis=-1, keepdims=True)
    yn = d * lax.rsqrt(var + LN_EPS) * g + b
    return yn * _sigmoid(yn)


def _causal_conv(buf_ref, rot_ref, w_ref, b_ref, y_ref, *, t0, tm, rows):
    d_conv = buf_ref.shape[1]
    base = CONV_HALO - (CONV_W - 1)
    n_rot = tm + CONV_HALO - 8
    for r in range(1, 8):
        rot_ref[r - 1, t0:t0 + n_rot, :] = buf_ref[t0 + r:t0 + r + n_rot, :]
    for cb in range(d_conv // LANES):
        ls = slice(cb * LANES, (cb + 1) * LANES)
        for rb in range(tm // rows):
            acc = jnp.zeros((rows, LANES), F32)
            for tap in range(CONV_W):
                a, r = divmod(base + tap, 8)
                r0 = t0 + rb * rows + 8 * a
                src = buf_ref[r0:r0 + rows, ls] if r == 0 else rot_ref[r - 1, r0:r0 + rows, ls]
                acc = acc + w_ref[tap:tap + 1, ls] * src
            y_ref[t0 + rb * rows:t0 + (rb + 1) * rows, ls] = acc + b_ref[:, ls]


def _conv_scratch(tm, d_conv):
    return [pltpu.VMEM((tm + CONV_HALO, d_conv), F32), pltpu.VMEM((7, tm + CONV_HALO - 8, d_conv), F32),
            pltpu.VMEM((tm, d_conv), F32)]


def _inproj_kernel(x_ref, g_ref, w_ref, bf_ref, tri_ref, *refs, d_att, d_conv, prompt, sub):
    if prompt:
        (cw_ref, cb_ref, lg_ref, lb_ref,
         q_ref, kt_ref, vt_ref, ktb_ref, vb_ref, lft_ref, cv_ref, utail_ref, ccol_ref, crow_ref,
         carry_ref, buf_ref, rot_ref, y_ref) = refs
    else:
        q_ref, k_ref, v_ref, lft_ref, u_ref, ccol_ref, crow_ref, carry_ref = refs
    tm = x_ref.shape[1]
    j = pl.program_id(1)

    @pl.when(j == 0)
    def _():
        carry_ref[...] = jnp.zeros_like(carry_ref)
        if prompt:
            buf_ref[0:CONV_HALO, :] = jnp.zeros((CONV_HALO, d_conv), F32)

    if prompt:
        @pl.when(j > 0)
        def _():
            buf_ref[0:CONV_HALO, :] = buf_ref[tm:tm + CONV_HALO, :]

    carry = carry_ref[...]
    tri = tri_ref[...]
    for t0 in range(0, tm, sub):
        rs = slice(t0, t0 + sub)
        h = _rms(x_ref[0, rs, :], g_ref[...]).astype(BF16)
        z = jnp.dot(h, w_ref[...], preferred_element_type=F32)
        k = z[:, d_att:2 * d_att]
        v = z[:, 2 * d_att:3 * d_att]
        o = 3 * d_att
        a = z[:, o:o + d_conv]
        gate = z[:, o + d_conv:o + 2 * d_conv]
        fg = z[:, o + 2 * d_conv:o + 2 * d_conv + LANES]
        q_ref[0, rs, :] = (z[:, :d_att] * Q_SCALE).astype(BF16)
        u = a * _sigmoid(gate)
        if prompt:
            kt = k.T
            kt_ref[0, :, rs] = kt
            vt_ref[0, :, rs] = v.T
            ktb_ref[0, :, rs] = kt.astype(BF16)
            vb_ref[0, rs, :] = v.astype(BF16)
            buf_ref[CONV_HALO + t0:CONV_HALO + t0 + sub, :] = u
            _causal_conv(buf_ref, rot_ref, cw_ref, cb_ref, y_ref, t0=t0, tm=sub, rows=min(sub, 128))
            cv_ref[0, rs, :] = _ln_swish(y_ref[rs, :], lg_ref[...], lb_ref[...]).astype(cv_ref.dtype)
        else:
            k_ref[0, rs, :] = k
            v_ref[0, rs, :] = v
            u_ref[0, rs, :] = u
        lf = _log_sigmoid(fg + bf_ref[...])
        lft = lf.T[:N_HEADS]
        lft_ref[0, :, rs] = lft

        chunks = []
        for c in range(sub // LANES):
            cs = _dot3(lft[:, c * LANES:(c + 1) * LANES], tri) + carry
            chunks.append(cs)
            carry = jnp.broadcast_to(cs[:, LANES - 1:LANES], cs.shape)
        crow = jnp.concatenate(chunks, axis=1) * LOG2E
        crow_ref[0, :, rs] = crow
        cfull = jnp.concatenate([crow, jnp.zeros((LANES - N_HEADS, sub), F32)], axis=0)
        ccol_ref[0, rs, :] = cfull.T[:, :N_HEADS]
    carry_ref[...] = carry

    if prompt:
        @pl.when(j == pl.num_programs(1) - 1)
        def _():
            utail_ref[0] = buf_ref[tm:tm + CONV_HALO, :]


def _inproj(x, g, w, bf, tri, conv, tm, d_att, d_conv):
    b, s, d = x.shape
    prompt = conv is not None
    tok = lambda width: pl.BlockSpec((1, tm, width), lambda i, j: (i, j, 0))
    tr = lambda rows: pl.BlockSpec((1, rows, tm), lambda i, j: (i, 0, j))
    shp = lambda width, dt: jax.ShapeDtypeStruct((b, s, width), dt)
    sht = lambda rows, dt: jax.ShapeDtypeStruct((b, rows, s), dt)
    in_specs = [tok(d), _const_spec((1, d)), _const_spec(w.shape, True), _const_spec((1, LANES)),
                _const_spec((LANES, LANES))]
    scratch = [pltpu.VMEM((N_HEADS, LANES), F32)]
    if prompt:
        in_specs += [_const_spec(conv[0].shape)] + [_const_spec((1, d_conv))] * 3
        out_specs = [tok(d_att), tr(d_att), tr(d_att), tr(d_att), tok(d_att), tr(N_HEADS), tok(d_conv),
                     pl.BlockSpec((1, CONV_HALO, d_conv), lambda i, j: (i, 0, 0))]
        out_shape = [shp(d_att, BF16), sht(d_att, F32), sht(d_att, F32), sht(d_att, BF16), shp(d_att, BF16),
                     sht(N_HEADS, F32), shp(d_conv, BF16), jax.ShapeDtypeStruct((b, CONV_HALO, d_conv), F32)]
        scratch += _conv_scratch(tm, d_conv)
    else:
        out_specs = [tok(d_att), tok(d_att), tok(d_att), tr(N_HEADS), tok(d_conv)]
        out_shape = [shp(d_att, BF16), shp(d_att, F32), shp(d_att, F32), sht(N_HEADS, F32), shp(d_conv, F32)]
    return pl.pallas_call(
        functools.partial(_inproj_kernel, d_att=d_att, d_conv=d_conv, prompt=prompt, sub=min(tm, 256)),
        grid=(b, s // tm),
        in_specs=in_specs,
        out_specs=out_specs + [tok(N_HEADS), tr(N_HEADS)],
        out_shape=out_shape + [shp(N_HEADS, F32), sht(N_HEADS, F32)],
        scratch_shapes=scratch,
        compiler_params=_params(2),
        name="inproj",
    )(x, g, w, bf, tri, *(conv or ()))


def _attn_kernel(q_ref, kt_ref, v_ref, ccol_ref, crow_ref, o_ref, *, tq):
    s_len = q_ref.shape[1]
    pair = pl.program_id(1)
    row = lax.broadcasted_iota(jnp.int32, (tq, tq), 0)
    col = lax.broadcasted_iota(jnp.int32, (tq, tq), 1)
    causal = col <= row
    low = lax.broadcasted_iota(jnp.int32, (tq, LANES), 1) < HEAD_DIM
    ccol = ccol_ref[0]
    head_lane = lax.broadcasted_iota(jnp.int32, ccol.shape, 1)
    cq_all = []
    sub = lax.broadcasted_iota(jnp.int32, (2 * LANES // N_HEADS, s_len), 0)
    extra = jnp.zeros(sub.shape, F32)
    for hh in range(2):
        head = 2 * pair + hh
        cq_all.append(jnp.sum(jnp.where(head_lane == head, ccol, 0.0), axis=-1, keepdims=True))
        for n, piece in enumerate(_split3(-crow_ref[0, pl.ds(head, 1), :])):
            extra = jnp.where(sub == 3 * hh + n, piece, extra)
    kt_bias = jnp.concatenate([kt_ref[0], extra.astype(BF16),
                               jnp.zeros((LANES - extra.shape[0], s_len), BF16)], axis=0)
    lane = lax.broadcasted_iota(jnp.int32, (tq, LANES), 1)

    def scores(i, hh):
        k0 = i * tq
        q2 = q_ref[0, k0:k0 + tq, :]
        zero = jnp.zeros_like(q2)
        qh = jnp.where(low, q2, zero) if hh == 0 else jnp.where(low, zero, q2)
        ones = jnp.where((lane >= 3 * hh) & (lane < 3 * hh + 3), 1.0, 0.0).astype(BF16)
        qh = jnp.concatenate([qh, ones], axis=1)
        s_diag = jnp.dot(qh, kt_bias[:, k0:k0 + tq], preferred_element_type=F32)
        s_off = jnp.dot(qh, kt_bias[:, :k0], preferred_element_type=F32) if i > 0 else None
        return s_diag, s_off

    v_ones = jnp.concatenate([v_ref[0], jnp.ones((s_len, LANES), BF16)], axis=1)

    def finish(i, hh, s_diag, s_off):
        k0 = i * tq
        cq = cq_all[hh][k0:k0 + tq]
        s_diag = jnp.where(causal, s_diag, NEG_INF)
        m = jnp.max(s_diag, axis=-1, keepdims=True)
        if i > 0:
            m = jnp.maximum(m, jnp.max(s_off, axis=-1, keepdims=True))
        shift = cq - (m + cq)
        pv = jnp.dot(jnp.exp2(s_diag + shift).astype(BF16), v_ones[k0:k0 + tq], preferred_element_type=F32)
        if i > 0:
            pv = pv + jnp.dot(jnp.exp2(s_off + shift).astype(BF16), v_ones[:k0], preferred_element_type=F32)
        return pv[:, :LANES] * (1.0 / pv[:, LANES:LANES + 1])

    blocks = [(i, hh) for i in range(s_len // tq) for hh in range(2)]
    nxt = scores(*blocks[0])
    halves = []
    for n, (i, hh) in enumerate(blocks):
        cur = nxt
        if n + 1 < len(blocks):
            nxt = scores(*blocks[n + 1])
        halves.append(finish(i, hh, *cur))
        if hh == 1:
            o_ref[0, i * tq:(i + 1) * tq, :] = jnp.where(low, halves[0], halves[1]).astype(BF16)
            halves = []


def _attn(q, ktb, vb, ccol, crow, tq):
    b, s, d_att = q.shape
    cols = pl.BlockSpec((1, s, LANES), lambda i, j: (i, 0, j))
    return pl.pallas_call(
        functools.partial(_attn_kernel, tq=tq),
        grid=(b, d_att // LANES),
        in_specs=[cols, pl.BlockSpec((1, LANES, s), lambda i, j: (i, j, 0)), cols,
                  pl.BlockSpec((1, s, N_HEADS), lambda i, j: (i, 0, 0)),
                  pl.BlockSpec((1, N_HEADS, s), lambda i, j: (i, 0, 0))],
        out_specs=cols,
        out_shape=jax.ShapeDtypeStruct((b, s, d_att), BF16),
        compiler_params=_params(2),
        name="attn",
    )(q, ktb, vb, ccol, crow)


def _decode_kernel(pt_ref, q_ref, kn_ref, vn_ref, cqcol_ref, cqrow_ref, us_ref, kc_ref, vc_ref, lc_ref, o_ref,
                   kbuf, vbuf, lbuf, sem, qbd_ref, m_ref, l_ref, acc_ref, carry_ref, *, n_group, n_steps):
    t = pl.program_id(0)
    js = lax.rem(t, n_steps)
    slot = lax.rem(t, 2)
    n_pages = pt_ref.shape[1]
    t_new = q_ref.shape[1]
    d_att = q_ref.shape[2]
    page = kbuf.shape[-1]
    n_rows = t_new * N_HEADS
    nt_dims = (((1,), (1,)), ((), ()))
    cqcol = cqcol_ref[0]

    def page_copies(step, to_slot, pages):
        out = []
        for g in range(n_group):
            p = pages(step, g)
            out.append(pltpu.make_async_copy(kc_ref.at[p], kbuf.at[to_slot, g], sem.at[to_slot, 0]))
            out.append(pltpu.make_async_copy(vc_ref.at[p], vbuf.at[to_slot, g], sem.at[to_slot, 1]))
            out.append(pltpu.make_async_copy(lc_ref.at[p], lbuf.at[to_slot, g], sem.at[to_slot, 2]))
        return out

    def table_page(step, g):
        return pt_ref[lax.div(step, n_steps), n_pages - 1 - (lax.rem(step, n_steps) * n_group + g)]

    @pl.when(t == 0)
    def _():
        for c in page_copies(t, slot, table_page):
            c.start()

    @pl.when(t + 1 < pl.num_programs(0))
    def _():
        for c in page_copies(t + 1, 1 - slot, table_page):
            c.start()

    @pl.when(js == 0)
    def _():
        q = q_ref[0].astype(F32)
        sub = lax.broadcasted_iota(jnp.int32, (N_HEADS, d_att), 0)
        lane = lax.broadcasted_iota(jnp.int32, (N_HEADS, d_att), 1)
        own = (lane // HEAD_DIM) == sub
        rows = [jnp.where(own, jnp.broadcast_to(q[i:i + 1, :], (N_HEADS, d_att)), 0.0) for i in range(t_new)]
        qbd = jnp.concatenate(rows, axis=0).astype(BF16)
        qbd_ref[...] = qbd
        s = lax.dot_general(qbd, kn_ref[0].astype(BF16), nt_dims, preferred_element_type=F32)
        cqk = jnp.concatenate([cqrow_ref[0]] * t_new, axis=0)
        r = lax.broadcasted_iota(jnp.int32, s.shape, 0)
        c = lax.broadcasted_iota(jnp.int32, s.shape, 1)
        s = jnp.where(c <= r // N_HEADS, s + (cqcol - cqk), NEG_INF)
        m = jnp.max(s, axis=-1, keepdims=True)
        pr = jnp.exp2(s - m)
        m_ref[...] = m
        l_ref[...] = jnp.sum(pr, axis=-1, keepdims=True)
        acc_ref[...] = jnp.dot(pr.astype(BF16), vn_ref[0].astype(BF16), preferred_element_type=F32)
        carry_ref[...] = jnp.zeros_like(carry_ref)

    for c in page_copies(t, slot, lambda step, g: 0):
        c.wait()

    qbd = qbd_ref[...]
    carry = carry_ref[...]
    us = us_ref[...]
    s_parts = []
    r_all = _dot3(lbuf[slot].reshape(n_group * N_HEADS, page), us)
    for g in range(n_group):
        r = r_all[g * N_HEADS:(g + 1) * N_HEADS]
        suffix = (r[:, :page] + carry) * LOG2E
        carry = carry + r[:, page:]
        bias = jnp.concatenate([suffix] * t_new, axis=0) + cqcol
        kt = kbuf[slot, g].reshape(d_att, page).astype(BF16)
        s_parts.append(jnp.dot(qbd, kt, preferred_element_type=F32) + bias)
    carry_ref[...] = carry
    s = jnp.concatenate(s_parts, axis=1)
    m = m_ref[...]
    m_new = jnp.maximum(m, jnp.max(s, axis=-1, keepdims=True))
    alpha = jnp.exp2(m - m_new)
    pr = jnp.exp2(s - m_new)
    m_ref[...] = m_new
    l_ref[...] = alpha * l_ref[...] + jnp.sum(pr, axis=-1, keepdims=True)
    prb = pr.astype(BF16)
    acc = alpha * acc_ref[...]
    for g in range(n_group):
        vt = vbuf[slot, g].reshape(d_att, page).astype(BF16)
        acc = acc + lax.dot_general(prb[:, g * page:(g + 1) * page], vt, nt_dims, preferred_element_type=F32)
    acc_ref[...] = acc

    @pl.when(js == n_steps - 1)
    def _():
        o = acc * (1.0 / l_ref[...])
        sub = lax.broadcasted_iota(jnp.int32, (N_HEADS, d_att), 0)
        lane = lax.broadcasted_iota(jnp.int32, (N_HEADS, d_att), 1)
        own = (lane // HEAD_DIM) == sub
        rows = [jnp.sum(jnp.where(own, o[i * N_HEADS:(i + 1) * N_HEADS], 0.0), axis=0, keepdims=True)
                for i in range(t_new)]
        o_ref[0] = jnp.concatenate(rows, axis=0).astype(o_ref.dtype)


def _decode(page_table, q, kn, vn, cqcol, cqrow, us, cache_k, cache_v, cache_lf, n_group):
    bd, t_new, d_att = q.shape
    n_pages = page_table.shape[1]
    page = cache_k.shape[3]
    n_rows = t_new * N_HEADS
    n_steps = n_pages // n_group
    per_b = lambda shape: pl.BlockSpec((1,) + shape, lambda t, pt: (t // n_steps,) + (0,) * len(shape))
    hbm = pl.BlockSpec(memory_space=pl.ANY)
    grid_spec = pltpu.PrefetchScalarGridSpec(
        num_scalar_prefetch=1,
        grid=(bd * n_steps,),
        in_specs=[per_b((t_new, d_att)), per_b(kn.shape[1:]), per_b(vn.shape[1:]), per_b((n_rows, 1)),
                  per_b(cqrow.shape[1:]), pl.BlockSpec(us.shape, lambda t, pt: (0, 0)), hbm, hbm, hbm],
        out_specs=per_b((t_new, d_att)),
        scratch_shapes=[pltpu.VMEM((2, n_group, N_HEADS, HEAD_DIM, page), F32),
                        pltpu.VMEM((2, n_group, N_HEADS, HEAD_DIM, page), F32),
                        pltpu.VMEM((2, n_group, N_HEADS, page), F32),
                        pltpu.SemaphoreType.DMA((2, 3)),
                        pltpu.VMEM((n_rows, d_att), BF16), pltpu.VMEM((n_rows, 1), F32),
                        pltpu.VMEM((n_rows, 1), F32), pltpu.VMEM((n_rows, d_att), F32),
                        pltpu.VMEM((N_HEADS, page), F32)],
    )
    return pl.pallas_call(
        functools.partial(_decode_kernel, n_group=n_group, n_steps=n_steps),
        grid_spec=grid_spec,
        out_shape=jax.ShapeDtypeStruct((bd, t_new, d_att), BF16),
        compiler_params=_params(1),
        name="decode",
    )(page_table, q, kn, vn, cqcol, cqrow, us, cache_k, cache_v, cache_lf)


def _conv_kernel(u_ref, state_ref, w_ref, b_ref, lg_ref, lb_ref, o_ref, buf_ref, rot_ref, y_ref):
    tm = u_ref.shape[1]
    for n in range(u_ref.shape[0]):
        buf_ref[0:CONV_HALO, :] = state_ref[n]
        buf_ref[CONV_HALO:, :] = u_ref[n]
        _causal_conv(buf_ref, rot_ref, w_ref, b_ref, y_ref, t0=0, tm=tm, rows=tm)
        o_ref[n] = _ln_swish(y_ref[...], lg_ref[...], lb_ref[...])


def _conv(u, state, w, b, lg, lb, per_step):
    bsz, tm, d_conv = u.shape
    assert state.shape == (bsz, CONV_HALO, d_conv)
    tile = pl.BlockSpec((per_step, tm, d_conv), lambda i: (i, 0, 0))
    vec = _const_spec((1, d_conv))
    return pl.pallas_call(
        _conv_kernel,
        grid=(bsz // per_step,),
        in_specs=[tile, pl.BlockSpec((per_step, CONV_HALO, d_conv), lambda i: (i, 0, 0)), _const_spec(w.shape),
                  vec, vec, vec],
        out_specs=tile,
        out_shape=jax.ShapeDtypeStruct((bsz, tm, d_conv), F32),
        scratch_shapes=_conv_scratch(tm, d_conv),
        compiler_params=_params(1),
        name="conv",
    )(u, state, w, b, lg, lb)


def _pool_kernel(x_ref, state_ref, g2_ref, g3_ref, w_ref, sc_ref, o_ref, h_ref, buf_ref, ta_ref, tb_ref, *, pos0):
    tm = x_ref.shape[1]
    pos = pos0 + lax.broadcasted_iota(jnp.int32, (tm, 1), 0)
    for bi in range(x_ref.shape[0]):
        x = x_ref[bi]
        h = _rms(x, g2_ref[...])
        h_ref[bi] = h
        mix = _pool_mix(h, state_ref[bi], pos, w_ref, sc_ref, buf_ref, ta_ref, tb_ref)
        o_ref[bi] = x + _rms(mix, g3_ref[...])


def _pool(x, state, g2, g3, w, scale, pos0, per_step):
    b, tm, d = x.shape
    assert state.shape == (b, POOL_HALO, d)
    tile = pl.BlockSpec((per_step, tm, d), lambda i: (i, 0, 0))
    vec = _const_spec((1, d))
    buf = pltpu.VMEM((tm + POOL_HALO, d), F32)
    return pl.pallas_call(
        functools.partial(_pool_kernel, pos0=pos0),
        grid=(b // per_step,),
        in_specs=[tile, pl.BlockSpec((per_step, POOL_HALO, d), lambda i: (i, 0, 0)), vec, vec,
                  _const_spec(w.shape), vec],
        out_specs=[tile, tile],
        out_shape=[jax.ShapeDtypeStruct((b, tm, d), F32), jax.ShapeDtypeStruct((b, tm, d), F32)],
        scratch_shapes=[buf, buf, buf],
        compiler_params=_params(1),
        name="pool",
    )(x, state, g2, g3, w, scale)


def kernel(x_prompt, x_sample, cache_k, cache_v, cache_logf, state_conv, state_pool, page_table,
           norm_g, ffn_w_gate, ffn_w_up, ffn_w_down, mix_w_in, fgate_b, conv_dw_w, conv_dw_b,
           conv_ln_g, conv_ln_b, mix_w_out, pool_w, pool_scale):
    b, s, d = x_prompt.shape
    bd, t_new, _ = x_sample.shape
    depth = norm_g.shape[0]
    d_att = N_HEADS * HEAD_DIM
    d_conv = d - d_att
    page = cache_k.shape[2]
    past_len = page_table.shape[1] * page
    tm_prompt = 512
    m_sample = bd * t_new
    t_pad = 8

    xp = x_prompt.reshape(b * s, d)
    xs = x_sample.reshape(m_sample, d)
    vec = lambda a: a.reshape(1, -1).astype(F32)

    stacks = (ffn_w_gate.astype(F32), ffn_w_up.astype(F32), ffn_w_down.astype(F32))
    halves = [(layer, half) for layer in range(depth) for half in range(2)]
    ffn_weights = [tuple(w[0, 0].astype(BF16) for w in stacks)]

    def ffn_both(xp, xs, layer, half, mix=None, pool=None):
        gp, gq = vec(norm_g[layer, 4 * half]), vec(norm_g[layer, 4 * half + 1])
        n = halves.index((layer, half))
        convert = (stacks, halves[n + 1]) if n + 1 < len(halves) else None
        xp, xs, *rest = _ffn(xp, xs, gp, gq, *ffn_weights[n], 1024, mix, pool, convert)
        tail = rest.pop(0) if pool is not None else None
        ffn_weights.append(tuple(rest))
        return xp, xs, tail

    r = jnp.arange(LANES)
    tri_incl = (r[:, None] <= r[None, :]).astype(F32)
    tri_group = tri_incl * (r[:, None] // t_new == r[None, :] // t_new).astype(F32)
    rp = jnp.arange(page)
    suffix_total = jnp.concatenate([(rp[:, None] > rp[None, :]).astype(F32), jnp.ones((page, page), F32)], axis=1)

    outs_p = {k: [] for k in ("k", "v", "lf", "conv", "pool")}
    outs_s = {k: [] for k in ("k", "v", "lf", "conv", "pool")}
    for layer in range(depth):
        g = norm_g[layer]
        xp, xs, _ = ffn_both(xp, xs, layer, 0)
        mix = pool = None
        if layer % 2 == 0:
            e = layer // 2
            w_in = mix_w_in[e]
            o = 3 * d_att + N_HEADS
            w_cat = jnp.concatenate(
                [w_in[:, :3 * d_att], w_in[:, o:], w_in[:, 3 * d_att:o],
                 jnp.zeros((d, LANES - N_HEADS), w_in.dtype)], axis=1).astype(BF16)
            bf = jnp.concatenate([fgate_b[e].astype(F32), jnp.zeros((LANES - N_HEADS,), F32)]).reshape(1, LANES)
            wa = mix_w_out[e, :d_att].astype(BF16)
            wc = mix_w_out[e, d_att:].astype(BF16)
            cw, cb = conv_dw_w[e].astype(F32), vec(conv_dw_b[e])
            lg, lb = vec(conv_ln_g[e]), vec(conv_ln_b[e])
            g2, g3 = vec(g[2]), vec(g[3])

            q, kt, vt, ktb, vb, lft, cv, u_tail, ccol, crow = _inproj(
                xp.reshape(b, s, d), g2, w_cat, bf, tri_incl, (cw, cb, lg, lb), tm_prompt, d_att, d_conv)
            att_p = _attn(q, ktb, vb, ccol, crow, 256).reshape(b * s, d_att)
            cv_p = cv.reshape(b * s, d_conv)
            outs_p["k"].append(kt.reshape(b, N_HEADS, HEAD_DIM, s).transpose(0, 3, 1, 2))
            outs_p["v"].append(vt.reshape(b, N_HEADS, HEAD_DIM, s).transpose(0, 3, 1, 2))
            outs_p["lf"].append(lft.transpose(0, 2, 1))
            outs_p["conv"].append(u_tail[:, CONV_HALO - (CONV_W - 1):])

            q, k, v, lft, u, ccol, crow = _inproj(
                xs.reshape(1, m_sample, d), g2, w_cat, bf, tri_group, None, m_sample, d_att, d_conv)
            lf = lft.reshape(N_HEADS, m_sample).T
            pad_rows = lambda a: jnp.pad(a.reshape(bd, t_new, d_att), ((0, 0), (0, 16 - t_new), (0, 0)))
            cqcol = ccol.reshape(bd, t_new * N_HEADS, 1)
            cqrow = jnp.pad(crow.reshape(N_HEADS, bd, t_new).transpose(1, 0, 2), ((0, 0), (0, 0), (0, 16 - t_new)))
            att = _decode(page_table, q.reshape(bd, t_new, d_att), pad_rows(k), pad_rows(v), cqcol, cqrow,
                          suffix_total, cache_k[e].transpose(0, 2, 3, 1), cache_v[e].transpose(0, 2, 3, 1),
                          cache_logf[e].astype(F32).transpose(0, 2, 1), 32)
            st = state_conv[e].astype(F32)
            cv = _conv(jnp.pad(u.reshape(bd, t_new, d_conv), ((0, 0), (0, t_pad - t_new), (0, 0))),
                       jnp.pad(st, ((0, 0), (CONV_HALO - (CONV_W - 1), 0), (0, 0))), cw, cb, lg, lb, 8)
            mix = (att_p, cv_p, att.reshape(m_sample, d_att), cv[:, :t_new].reshape(m_sample, d_conv), wa, wc, g3)
            outs_s["k"].append(k.reshape(bd, t_new, N_HEADS, HEAD_DIM))
            outs_s["v"].append(v.reshape(bd, t_new, N_HEADS, HEAD_DIM))
            outs_s["lf"].append(lf.reshape(bd, t_new, N_HEADS))
            outs_s["conv"].append(jnp.concatenate([st, u.reshape(bd, t_new, d_conv)], axis=1)[:, t_new:])
        else:
            o = layer // 2
            g2, g3 = vec(g[2]), vec(g[3])
            pw = pool_w[o].astype(BF16)
            sc = vec(pool_scale[o])
            n_keep = POOL_WINDOWS[-1] - 1
            pool = (s, g2, g3, pw, sc)
            st = state_pool[o].astype(F32)
            xs3, hs = _pool(jnp.pad(xs.reshape(bd, t_new, d), ((0, 0), (0, t_pad - t_new), (0, 0))),
                            jnp.pad(st, ((0, 0), (POOL_HALO - n_keep, 0), (0, 0))), g2, g3, pw, sc, past_len, 8)
            xs = xs3[:, :t_new].reshape(m_sample, d)
            outs_s["pool"].append(jnp.concatenate([st, hs[:, :t_new]], axis=1)[:, t_new:])
        xp, xs, tail = ffn_both(xp, xs, layer, 1, mix, pool)
        if pool is not None:
            outs_p["pool"].append(tail[:, POOL_TAIL - (POOL_WINDOWS[-1] - 1):])

    stack = lambda l: jnp.stack(l)
    return (xp.reshape(b, s, d), xs.reshape(bd, t_new, d),
            stack(outs_p["k"]), stack(outs_p["v"]), stack(outs_p["lf"]), stack(outs_p["conv"]),
            stack(outs_p["pool"]),
            stack(outs_s["k"]), stack(outs_s["v"]), stack(outs_s["lf"]), stack(outs_s["conv"]),
            stack(outs_s["pool"]))
```

```python
import functools

import jax
import jax.numpy as jnp
from jax import lax
from jax.experimental import pallas as pl
from jax.experimental.pallas import tpu as pltpu

F32 = jnp.float32
BF16 = jnp.bfloat16

RMS_EPS = 1e-6
LN_EPS = 1e-5
NEG_INF = -1e30
LOG2E = 1.4426950408889634

LANES = 128
HEAD_DIM = 64
N_HEADS = 8
Q_SCALE = HEAD_DIM ** -0.5 * LOG2E
CONV_W = 31
CONV_HALO = 32
POOL_WINDOWS = (2, 4, 8, 16)
POOL_HALO = 32
POOL_TAIL = 16
POOL_BLOCK = 256
VMEM_LIMIT_BYTES = 56 * 1024 * 1024
FFN_VMEM_LIMIT_BYTES = 61 * 1024 * 1024


def _params(n_axes, vmem_limit_bytes=VMEM_LIMIT_BYTES):
    return pltpu.CompilerParams(dimension_semantics=("arbitrary",) * n_axes,
                                vmem_limit_bytes=vmem_limit_bytes)


def _const_spec(shape, single_buffer=False):
    zeros = (0,) * len(shape)
    if single_buffer:
        return pl.BlockSpec(shape, lambda *_: zeros, pipeline_mode=pl.Buffered(1))
    return pl.BlockSpec(shape, lambda *_: zeros)


def _rms(x, g):
    ms = jnp.mean(x * x, axis=-1, keepdims=True)
    return x * lax.rsqrt(ms + RMS_EPS) * g


def _sigmoid(x):
    return 1.0 / (1.0 + jnp.exp(-x))


def _log_sigmoid(x):
    return jnp.minimum(x, 0.0) - jnp.log1p(jnp.exp(-jnp.abs(x)))


def _split3(x):
    hi = x.astype(BF16).astype(F32)
    r = x - hi
    mid = r.astype(BF16).astype(F32)
    return hi, mid, r - mid


def _dot3(x, w):
    hi, mid, lo = _split3(x)
    d = lambda a: jnp.dot(a, w, preferred_element_type=F32)
    return d(hi) + d(mid) + d(lo)


def _pool_mix(h, halo, pos, w_ref, sc_ref, buf_ref, ta_ref, tb_ref):
    rows, d = h.shape
    grp = d // len(POOL_WINDOWS)
    n = rows + POOL_HALO
    buf_ref[0:POOL_HALO, :] = halo
    buf_ref[POOL_HALO:n, :] = h
    ta_ref[8:n, :] = buf_ref[8:n, :] + buf_ref[7:n - 1, :]
    tb_ref[16:n, grp:] = ta_ref[16:n, grp:] + ta_ref[14:n - 2, grp:]
    ta_ref[24:n, 2 * grp:] = tb_ref[24:n, 2 * grp:] + tb_ref[20:n - 4, 2 * grp:]
    s16 = ta_ref[32:n, 3 * grp:] + ta_ref[24:n - 8, 3 * grp:]
    totals = (ta_ref[POOL_HALO:n, 0:grp], tb_ref[POOL_HALO:n, grp:2 * grp],
              ta_ref[POOL_HALO:n, 2 * grp:3 * grp], s16)
    outs = []
    for gi, win in enumerate(POOL_WINDOWS):
        cnt = jnp.minimum(pos + 1, win).astype(F32)
        mixed = totals[gi] / cnt - h[:, gi * grp:(gi + 1) * grp]
        outs.append(jnp.dot(mixed.astype(BF16), w_ref[gi], preferred_element_type=F32))
    return jnp.concatenate(outs, axis=-1) * sc_ref[...]


def _convert_slice_copies(i, slot, src_hbm, which, dst_hbm, stage_in, stage_out, sem_in, sem_out, n_slices):
    ins, outs = [], []
    for k in range(3):
        rows = dst_hbm[k].shape[0] // n_slices
        ins.append(pltpu.make_async_copy(src_hbm[k].at[which[0], which[1], pl.ds(i * rows, rows)],
                                         stage_in[k].at[slot], sem_in.at[slot, k]))
        outs.append(pltpu.make_async_copy(stage_out[k].at[slot], dst_hbm[k].at[pl.ds(i * rows, rows)],
                                          sem_out.at[slot, k]))
    return ins, outs


def _ffn_kernel(*refs, chunk, n_prompt, mix, pool, convert):
    xp_ref, xs_ref, gpre_ref, gpost_ref, wg_ref, wu_ref, wd_ref = refs[:7]
    refs = refs[7:]
    if mix:
        attp_ref, cvp_ref, atts_ref, cvs_ref, wa_ref, wc_ref, gmix_ref = refs[:7]
        refs = refs[7:]
    if pool:
        halo_ref, g2p_ref, g3p_ref, pw_ref, psc_ref = refs[:5]
        refs = refs[5:]
    if convert:
        src_hbm, refs = refs[:3], refs[3:]
    op_ref, os_ref = refs[:2]
    refs = refs[2:]
    if pool:
        tail_ref, refs = refs[0], refs[1:]
    if convert:
        dst_hbm, refs = refs[:3], refs[3:]
    a_ref, refs = refs[0], refs[1:]
    if pool:
        pool_bufs, refs = refs[:3], refs[3:]
    if convert:
        stage_in, stage_out, (sem_in, sem_out) = refs[0:3], refs[3:6], refs[6:8]
    i = pl.program_id(0)

    def slice_copies(step):
        return _convert_slice_copies(step, lax.rem(step, 2), src_hbm, convert, dst_hbm, stage_in, stage_out,
                                     sem_in, sem_out, n_prompt)

    def convert_slice():
        @pl.when(i == 0)
        def _():
            for c in slice_copies(i)[0]:
                c.start()

        @pl.when(i + 1 < n_prompt)
        def _():
            for c in slice_copies(i + 1)[0]:
                c.start()

        ins, outs = slice_copies(i)
        for c in ins:
            c.wait()

        @pl.when(i >= 2)
        def _():
            for c in slice_copies(i - 2)[1]:
                c.wait()

        slot = lax.rem(i, 2)
        for k in range(3):
            stage_out[k][slot] = stage_in[k][slot].astype(BF16)
        for c in outs:
            c.start()

    def pooled(x):
        rows = x.shape[0]
        tile_in_seq = lax.rem(i, pool)
        halo = jnp.where(tile_in_seq == 0, 0.0, _rms(halo_ref[...], g2p_ref[...]))
        out = []
        for r0 in range(0, rows, POOL_BLOCK):
            xb = x[r0:r0 + POOL_BLOCK]
            h = _rms(xb, g2p_ref[...])
            pos = tile_in_seq * rows + r0 + lax.broadcasted_iota(jnp.int32, (POOL_BLOCK, 1), 0)
            out.append(xb + _rms(_pool_mix(h, halo, pos, pw_ref, psc_ref, *pool_bufs), g3p_ref[...]))
            halo = h[POOL_BLOCK - POOL_HALO:]
        tail_ref[0] = h[POOL_BLOCK - POOL_TAIL:]
        return jnp.concatenate(out, axis=0)

    def update(x_ref, o_ref, att_ref=None, cv_ref=None, with_pool=False):
        rows = x_ref.shape[0]
        x = x_ref[...]
        if mix:
            mo = jnp.dot(att_ref[...].astype(BF16), wa_ref[...], preferred_element_type=F32)
            mo = mo + jnp.dot(cv_ref[...].astype(BF16), wc_ref[...], preferred_element_type=F32)
            x = x + _rms(mo, gmix_ref[...])
        if with_pool:
            x = pooled(x)
        h = _rms(x, gpre_ref[...]).astype(BF16)
        d_ff = wg_ref.shape[1]
        for c0 in range(0, d_ff, chunk):
            sl = slice(c0, min(c0 + chunk, d_ff))
            g = jnp.dot(h, wg_ref[:, sl], preferred_element_type=F32)
            u = jnp.dot(h, wu_ref[:, sl], preferred_element_type=F32)
            a_ref[0:rows, sl] = (g * _sigmoid(g) * u).astype(BF16)
        y = jnp.dot(a_ref[0:rows, :], wd_ref[...], preferred_element_type=F32)
        o_ref[...] = x + 0.5 * _rms(y, gpost_ref[...])

    @pl.when(i < n_prompt)
    def _():
        if convert:
            convert_slice()
        update(xp_ref, op_ref, *((attp_ref, cvp_ref) if mix else ()), with_pool=bool(pool))

    @pl.when(i == n_prompt)
    def _():
        if convert:
            for step in (n_prompt - 2, n_prompt - 1):
                for c in slice_copies(step)[1]:
                    c.wait()
        update(xs_ref, os_ref, *((atts_ref, cvs_ref) if mix else ()))


def _ffn(xp, xs, g_pre, g_post, wg, wu, wd, tm, mix=None, pool=None, convert=None):
    m, d = xp.shape
    d_ff = wg.shape[1]
    n_prompt = m // tm
    tile = lambda width: pl.BlockSpec((tm, width), lambda i: (jnp.minimum(i, n_prompt - 1), 0))
    hbm = pl.BlockSpec(memory_space=pl.ANY)
    in_specs = [tile(d), _const_spec(xs.shape), _const_spec((1, d)), _const_spec((1, d)),
                _const_spec(wg.shape, True), _const_spec(wu.shape, True), _const_spec(wd.shape, True)]
    out_specs = [tile(d), _const_spec(xs.shape)]
    out_shape = [jax.ShapeDtypeStruct((m, d), F32), jax.ShapeDtypeStruct(xs.shape, F32)]
    scratch = [pltpu.VMEM((tm, d_ff), BF16)]
    operands = [xp, xs, g_pre, g_post, wg, wu, wd]
    if mix is not None:
        att_p, cv_p, att_s, cv_s, w_att, w_cv, _ = mix
        in_specs += [tile(att_p.shape[1]), tile(cv_p.shape[1]), _const_spec(att_s.shape), _const_spec(cv_s.shape),
                     _const_spec(w_att.shape, True), _const_spec(w_cv.shape, True), _const_spec((1, d))]
        operands += list(mix)
    tiles_per_seq = None
    if pool is not None:
        seq_len, g2p, g3p, pw, psc = pool
        tiles_per_seq = seq_len // tm
        per_halo = tm // POOL_HALO
        in_specs += [pl.BlockSpec((POOL_HALO, d),
                                  lambda i: (jnp.maximum(jnp.minimum(i, n_prompt - 1) * per_halo - 1, 0), 0)),
                     _const_spec((1, d)), _const_spec((1, d)), _const_spec(pw.shape, True), _const_spec((1, d))]
        operands += [xp, g2p, g3p, pw, psc]
        out_specs += [pl.BlockSpec((1, POOL_TAIL, d),
                                   lambda i: (jnp.minimum(i, n_prompt - 1) // tiles_per_seq, 0, 0))]
        out_shape += [jax.ShapeDtypeStruct((m // seq_len, POOL_TAIL, d), F32)]
        scratch += [pltpu.VMEM((POOL_BLOCK + POOL_HALO, d), F32)] * 3
    if convert is not None:
        stacks, which = convert
        in_specs += [hbm] * 3
        operands += list(stacks)
        out_specs += [hbm] * 3
        shapes = [(d, d_ff), (d, d_ff), (d_ff, d)]
        out_shape += [jax.ShapeDtypeStruct(sh, BF16) for sh in shapes]
        scratch += [pltpu.VMEM((2, r // n_prompt, c), F32) for r, c in shapes]
        scratch += [pltpu.VMEM((2, r // n_prompt, c), BF16) for r, c in shapes]
        scratch += [pltpu.SemaphoreType.DMA((2, 3)), pltpu.SemaphoreType.DMA((2, 3))]
    return pl.pallas_call(
        functools.partial(_ffn_kernel, chunk=256, n_prompt=n_prompt, mix=mix is not None, pool=tiles_per_seq,
                          convert=None if convert is None else convert[1]),
        grid=(n_prompt + 1,),
        in_specs=in_specs,
        out_specs=out_specs,
        out_shape=out_shape,
        scratch_shapes=scratch,
        compiler_params=_params(1, FFN_VMEM_LIMIT_BYTES),
        name="ffn",
    )(*operands)


def _ln_swish(y, g, b):
    mu = jnp.mean(y, axis=-1, keepdims=True)
    d = y - mu
    var = jnp.mean(d * d, axis=-1, keepdims=True)
    yn = d * lax.rsqrt(var + LN_EPS) * g + b
    return yn * _sigmoid(yn)


def _causal_conv(buf_ref, rot_ref, w_ref, b_ref, y_ref, *, t0, tm, rows):
    d_conv = buf_ref.shape[1]
    base = CONV_HALO - (CONV_W - 1)
    n_rot = tm + CONV_HALO - 8
    for r in range(1, 8):
        rot_ref[r - 1, t0:t0 + n_rot, :] = buf_ref[t0 + r:t0 + r + n_rot, :]
    for cb in range(d_conv // LANES):
        ls = slice(cb * LANES, (cb + 1) * LANES)
        for rb in range(tm // rows):
            acc = jnp.zeros((rows, LANES), F32)
            for tap in range(CONV_W):
                a, r = divmod(base + tap, 8)
                r0 = t0 + rb * rows + 8 * a
                src = buf_ref[r0:r0 + rows, ls] if r == 0 else rot_ref[r - 1, r0:r0 + rows, ls]
                acc = acc + w_ref[tap:tap + 1, ls] * src
            y_ref[t0 + rb * rows:t0 + (rb + 1) * rows, ls] = acc + b_ref[:, ls]


def _conv_scratch(tm, d_conv):
    return [pltpu.VMEM((tm + CONV_HALO, d_conv), F32), pltpu.VMEM((7, tm + CONV_HALO - 8, d_conv), F32),
            pltpu.VMEM((tm, d_conv), F32)]


def _inproj_kernel(x_ref, g_ref, w_ref, bf_ref, tri_ref, *refs, d_att, d_conv, prompt, sub):
    if prompt:
        (cw_ref, cb_ref, lg_ref, lb_ref,
         q_ref, kt_ref, vt_ref, ktb_ref, vb_ref, lft_ref, cv_ref, utail_ref, ccol_ref, crow_ref,
         carry_ref, buf_ref, rot_ref, y_ref) = refs
    else:
        q_ref, k_ref, v_ref, lft_ref, u_ref, ccol_ref, crow_ref, carry_ref = refs
    tm = x_ref.shape[1]
    j = pl.program_id(1)

    @pl.when(j == 0)
    def _():
        carry_ref[...] = jnp.zeros_like(carry_ref)
        if prompt:
            buf_ref[0:CONV_HALO, :] = jnp.zeros((CONV_HALO, d_conv), F32)

    if prompt:
        @pl.when(j > 0)
        def _():
            buf_ref[0:CONV_HALO, :] = buf_ref[tm:tm + CONV_HALO, :]

    carry = carry_ref[...]
    tri = tri_ref[...]
    for t0 in range(0, tm, sub):
        rs = slice(t0, t0 + sub)
        h = _rms(x_ref[0, rs, :], g_ref[...]).astype(BF16)
        z = jnp.dot(h, w_ref[...], preferred_element_type=F32)
        k = z[:, d_att:2 * d_att]
        v = z[:, 2 * d_att:3 * d_att]
        o = 3 * d_att
        a = z[:, o:o + d_conv]
        gate = z[:, o + d_conv:o + 2 * d_conv]
        fg = z[:, o + 2 * d_conv:o + 2 * d_conv + LANES]
        q_ref[0, rs, :] = (z[:, :d_att] * Q_SCALE).astype(BF16)
        u = a * _sigmoid(gate)
        if prompt:
            kt = k.T
            kt_ref[0, :, rs] = kt
            vt_ref[0, :, rs] = v.T
            ktb_ref[0, :, rs] = kt.astype(BF16)
            vb_ref[0, rs, :] = v.astype(BF16)
            buf_ref[CONV_HALO + t0:CONV_HALO + t0 + sub, :] = u
            _causal_conv(buf_ref, rot_ref, cw_ref, cb_ref, y_ref, t0=t0, tm=sub, rows=min(sub, 128))
            cv_ref[0, rs, :] = _ln_swish(y_ref[rs, :], lg_ref[...], lb_ref[...]).astype(cv_ref.dtype)
        else:
            k_ref[0, rs, :] = k
            v_ref[0, rs, :] = v
            u_ref[0, rs, :] = u
        lf = _log_sigmoid(fg + bf_ref[...])
        lft = lf.T[:N_HEADS]
        lft_ref[0, :, rs] = lft

        chunks = []
        for c in range(sub // LANES):
            cs = _dot3(lft[:, c * LANES:(c + 1) * LANES], tri) + carry
            chunks.append(cs)
            carry = jnp.broadcast_to(cs[:, LANES - 1:LANES], cs.shape)
        crow = jnp.concatenate(chunks, axis=1) * LOG2E
        crow_ref[0, :, rs] = crow
        cfull = jnp.concatenate([crow, jnp.zeros((LANES - N_HEADS, sub), F32)], axis=0)
        ccol_ref[0, rs, :] = cfull.T[:, :N_HEADS]
    carry_ref[...] = carry

    if prompt:
        @pl.when(j == pl.num_programs(1) - 1)
        def _():
            utail_ref[0] = buf_ref[tm:tm + CONV_HALO, :]


def _inproj(x, g, w, bf, tri, conv, tm, d_att, d_conv):
    b, s, d = x.shape
    prompt = conv is not None
    tok = lambda width: pl.BlockSpec((1, tm, width), lambda i, j: (i, j, 0))
    tr = lambda rows: pl.BlockSpec((1, rows, tm), lambda i, j: (i, 0, j))
    shp = lambda width, dt: jax.ShapeDtypeStruct((b, s, width), dt)
    sht = lambda rows, dt: jax.ShapeDtypeStruct((b, rows, s), dt)
    in_specs = [tok(d), _const_spec((1, d)), _const_spec(w.shape, True), _const_spec((1, LANES)),
                _const_spec((LANES, LANES))]
    scratch = [pltpu.VMEM((N_HEADS, LANES), F32)]
    if prompt:
        in_specs += [_const_spec(conv[0].shape)] + [_const_spec((1, d_conv))] * 3
        out_specs = [tok(d_att), tr(d_att), tr(d_att), tr(d_att), tok(d_att), tr(N_HEADS), tok(d_conv),
                     pl.BlockSpec((1, CONV_HALO, d_conv), lambda i, j: (i, 0, 0))]
        out_shape = [shp(d_att, BF16), sht(d_att, F32), sht(d_att, F32), sht(d_att, BF16), shp(d_att, BF16),
                     sht(N_HEADS, F32), shp(d_conv, BF16), jax.ShapeDtypeStruct((b, CONV_HALO, d_conv), F32)]
        scratch += _conv_scratch(tm, d_conv)
    else:
        out_specs = [tok(d_att), tok(d_att), tok(d_att), tr(N_HEADS), tok(d_conv)]
        out_shape = [shp(d_att, BF16), shp(d_att, F32), shp(d_att, F32), sht(N_HEADS, F32), shp(d_conv, F32)]
    return pl.pallas_call(
        functools.partial(_inproj_kernel, d_att=d_att, d_conv=d_conv, prompt=prompt, sub=min(tm, 256)),
        grid=(b, s // tm),
        in_specs=in_specs,
        out_specs=out_specs + [tok(N_HEADS), tr(N_HEADS)],
        out_shape=out_shape + [shp(N_HEADS, F32), sht(N_HEADS, F32)],
        scratch_shapes=scratch,
        compiler_params=_params(2),
        name="inproj",
    )(x, g, w, bf, tri, *(conv or ()))


def _attn_decode_kernel(pt_ref, q_ref, kt_ref, v_ref, ccol_ref, crow_ref,
                        qs_ref, kn_ref, vn_ref, cqcol_ref, cqrow_ref, us_ref, kc_ref, vc_ref, lc_ref,
                        o_ref, os_ref, kbuf, vbuf, lbuf, sem, *, tq, n_group):
    s_len = q_ref.shape[1]
    pair = pl.program_id(1)
    t = pl.program_id(0) * pl.num_programs(1) + pair
    n_samples = pl.num_programs(0) * pl.num_programs(1)
    n_pages = pt_ref.shape[1]
    n_sub = n_pages // n_group
    assert n_sub == 2
    t_new = qs_ref.shape[1]
    d_att = qs_ref.shape[2]
    page = kbuf.shape[-1]
    n_rows = t_new * N_HEADS
    nt_dims = (((1,), (1,)), ((), ()))

    def page_copies(sample, sub, real_pages=True):
        out = []
        for g in range(n_group):
            p = pt_ref[sample, n_pages - 1 - (sub * n_group + g)] if real_pages else 0
            out.append(pltpu.make_async_copy(kc_ref.at[p], kbuf.at[sub, g], sem.at[sub, 0]))
            out.append(pltpu.make_async_copy(vc_ref.at[p], vbuf.at[sub, g], sem.at[sub, 1]))
            out.append(pltpu.make_async_copy(lc_ref.at[p], lbuf.at[sub, g], sem.at[sub, 2]))
        return out

    @pl.when(t == 0)
    def _():
        for c in page_copies(t, 0):
            c.start()

    cqcol = cqcol_ref[0]
    st = {}

    def decode_half(sub):
        if sub + 1 < n_sub:
            for c in page_copies(t, sub + 1):
                c.start()
        else:
            @pl.when(t + 1 < n_samples)
            def _():
                for c in page_copies(t + 1, 0):
                    c.start()

        if sub == 0:
            q = qs_ref[0].astype(F32)
            hsub = lax.broadcasted_iota(jnp.int32, (N_HEADS, d_att), 0)
            hlane = lax.broadcasted_iota(jnp.int32, (N_HEADS, d_att), 1)
            own = (hlane // HEAD_DIM) == hsub
            rows = [jnp.where(own, jnp.broadcast_to(q[i:i + 1, :], (N_HEADS, d_att)), 0.0) for i in range(t_new)]
            st["qbd"] = jnp.concatenate(rows, axis=0).astype(BF16)
            s = lax.dot_general(st["qbd"], kn_ref[0].astype(BF16), nt_dims, preferred_element_type=F32)
            cqk = jnp.concatenate([cqrow_ref[0]] * t_new, axis=0)
            r = lax.broadcasted_iota(jnp.int32, s.shape, 0)
            c = lax.broadcasted_iota(jnp.int32, s.shape, 1)
            s = jnp.where(c <= r // N_HEADS, s + (cqcol - cqk), NEG_INF)
            st["m"] = jnp.max(s, axis=-1, keepdims=True)
            pr = jnp.exp2(s - st["m"])
            st["l"] = jnp.sum(pr, axis=-1, keepdims=True)
            st["acc"] = jnp.dot(pr.astype(BF16), vn_ref[0].astype(BF16), preferred_element_type=F32)
            st["carry"] = jnp.zeros((N_HEADS, page), F32)

        for c in page_copies(t, sub, real_pages=False):
            c.wait()

        qbd, carry = st["qbd"], st["carry"]
        us = us_ref[...]
        s_parts = []
        r_all = _dot3(lbuf[sub].reshape(n_group * N_HEADS, page), us)
        for g in range(n_group):
            r = r_all[g * N_HEADS:(g + 1) * N_HEADS]
            suffix = (r[:, :page] + carry) * LOG2E
            carry = carry + r[:, page:]
            bias = jnp.concatenate([suffix] * t_new, axis=0) + cqcol
            kt = kbuf[sub, g].reshape(d_att, page).astype(BF16)
            s_parts.append(jnp.dot(qbd, kt, preferred_element_type=F32) + bias)
        s = jnp.concatenate(s_parts, axis=1)
        m_new = jnp.maximum(st["m"], jnp.max(s, axis=-1, keepdims=True))
        alpha = jnp.exp2(st["m"] - m_new)
        pr = jnp.exp2(s - m_new)
        prb = pr.astype(BF16)
        acc = alpha * st["acc"]
        for g in range(n_group):
            vt = vbuf[sub, g].reshape(d_att, page).astype(BF16)
            acc = acc + lax.dot_general(prb[:, g * page:(g + 1) * page], vt, nt_dims, preferred_element_type=F32)
        st.update(m=m_new, l=alpha * st["l"] + jnp.sum(pr, axis=-1, keepdims=True), acc=acc, carry=carry)

        if sub == n_sub - 1:
            o = acc * (1.0 / st["l"])
            hsub = lax.broadcasted_iota(jnp.int32, (N_HEADS, d_att), 0)
            hlane = lax.broadcasted_iota(jnp.int32, (N_HEADS, d_att), 1)
            own = (hlane // HEAD_DIM) == hsub
            rows = [jnp.sum(jnp.where(own, o[i * N_HEADS:(i + 1) * N_HEADS], 0.0), axis=0, keepdims=True)
                    for i in range(t_new)]
            os_ref[0] = jnp.concatenate(rows, axis=0).astype(os_ref.dtype)

    row = lax.broadcasted_iota(jnp.int32, (tq, tq), 0)
    col = lax.broadcasted_iota(jnp.int32, (tq, tq), 1)
    causal = col <= row
    low = lax.broadcasted_iota(jnp.int32, (tq, LANES), 1) < HEAD_DIM
    ccol = ccol_ref[0]
    head_lane = lax.broadcasted_iota(jnp.int32, ccol.shape, 1)
    cq_all = []
    sub16 = lax.broadcasted_iota(jnp.int32, (2 * LANES // N_HEADS, s_len), 0)
    extra = jnp.zeros(sub16.shape, F32)
    for hh in range(2):
        head = 2 * pair + hh
        cq_all.append(jnp.sum(jnp.where(head_lane == head, ccol, 0.0), axis=-1, keepdims=True))
        for n, piece in enumerate(_split3(-crow_ref[0, pl.ds(head, 1), :])):
            extra = jnp.where(sub16 == 3 * hh + n, piece, extra)
    kt_bias = jnp.concatenate([kt_ref[0], extra.astype(BF16),
                               jnp.zeros((LANES - extra.shape[0], s_len), BF16)], axis=0)
    lane = lax.broadcasted_iota(jnp.int32, (tq, LANES), 1)

    def scores(i, hh):
        k0 = i * tq
        q2 = q_ref[0, k0:k0 + tq, :]
        zero = jnp.zeros_like(q2)
        qh = jnp.where(low, q2, zero) if hh == 0 else jnp.where(low, zero, q2)
        ones = jnp.where((lane >= 3 * hh) & (lane < 3 * hh + 3), 1.0, 0.0).astype(BF16)
        qh = jnp.concatenate([qh, ones], axis=1)
        s_diag = jnp.dot(qh, kt_bias[:, k0:k0 + tq], preferred_element_type=F32)
        s_off = jnp.dot(qh, kt_bias[:, :k0], preferred_element_type=F32) if i > 0 else None
        return s_diag, s_off

    v_ones = jnp.concatenate([v_ref[0], jnp.ones((s_len, LANES), BF16)], axis=1)

    def finish(i, hh, s_diag, s_off):
        k0 = i * tq
        cq = cq_all[hh][k0:k0 + tq]
        s_diag = jnp.where(causal, s_diag, NEG_INF)
        m = jnp.max(s_diag, axis=-1, keepdims=True)
        if i > 0:
            m = jnp.maximum(m, jnp.max(s_off, axis=-1, keepdims=True))
        shift = cq - (m + cq)
        pv = jnp.dot(jnp.exp2(s_diag + shift).astype(BF16), v_ones[k0:k0 + tq], preferred_element_type=F32)
        if i > 0:
            pv = pv + jnp.dot(jnp.exp2(s_off + shift).astype(BF16), v_ones[:k0], preferred_element_type=F32)
        return pv[:, :LANES] * (1.0 / pv[:, LANES:LANES + 1])

    blocks = [(i, hh) for i in range(s_len // tq) for hh in range(2)]
    work = [i + 1 for i, _ in blocks]
    first_half_at = next(n for n in range(len(blocks)) if 2 * sum(work[:n + 1]) >= sum(work))
    nxt = scores(*blocks[0])
    halves = []
    for n, (i, hh) in enumerate(blocks):
        cur = nxt
        if n + 1 < len(blocks):
            nxt = scores(*blocks[n + 1])
        halves.append(finish(i, hh, *cur))
        if hh == 1:
            o_ref[0, i * tq:(i + 1) * tq, :] = jnp.where(low, halves[0], halves[1]).astype(BF16)
            halves = []
        if n == first_half_at:
            decode_half(0)
    decode_half(1)


def _attn_decode(q, ktb, vb, ccol, crow, tq,
                 page_table, qs, kn, vn, cqcol, cqrow, us, cache_k, cache_v, cache_lf, n_group):
    b, s, d_att = q.shape
    bd, t_new, _ = qs.shape
    n_pairs = d_att // LANES
    assert bd == b * n_pairs
    page = cache_k.shape[3]
    n_rows = t_new * N_HEADS
    cols = pl.BlockSpec((1, s, LANES), lambda i, j, pt: (i, 0, j))
    per_s = lambda shape: pl.BlockSpec((1,) + shape, lambda i, j, pt: (i * n_pairs + j,) + (0,) * len(shape))
    hbm = pl.BlockSpec(memory_space=pl.ANY)
    grid_spec = pltpu.PrefetchScalarGridSpec(
        num_scalar_prefetch=1,
        grid=(b, n_pairs),
        in_specs=[cols, pl.BlockSpec((1, LANES, s), lambda i, j, pt: (i, j, 0)), cols,
                  pl.BlockSpec((1, s, N_HEADS), lambda i, j, pt: (i, 0, 0)),
                  pl.BlockSpec((1, N_HEADS, s), lambda i, j, pt: (i, 0, 0)),
                  per_s((t_new, d_att)), per_s(kn.shape[1:]), per_s(vn.shape[1:]), per_s((n_rows, 1)),
                  per_s(cqrow.shape[1:]), pl.BlockSpec(us.shape, lambda i, j, pt: (0, 0)), hbm, hbm, hbm],
        out_specs=[cols, per_s((t_new, d_att))],
        scratch_shapes=[pltpu.VMEM((2, n_group, N_HEADS, HEAD_DIM, page), F32),
                        pltpu.VMEM((2, n_group, N_HEADS, HEAD_DIM, page), F32),
                        pltpu.VMEM((2, n_group, N_HEADS, page), F32),
                        pltpu.SemaphoreType.DMA((2, 3))],
    )
    return pl.pallas_call(
        functools.partial(_attn_decode_kernel, tq=tq, n_group=n_group),
        grid_spec=grid_spec,
        out_shape=[jax.ShapeDtypeStruct((b, s, d_att), BF16), jax.ShapeDtypeStruct((bd, t_new, d_att), BF16)],
        compiler_params=_params(2),
        name="attn_decode",
    )(page_table, q, ktb, vb, ccol, crow, qs, kn, vn, cqcol, cqrow, us, cache_k, cache_v, cache_lf)


def _conv_kernel(u_ref, state_ref, w_ref, b_ref, lg_ref, lb_ref, o_ref, buf_ref, rot_ref, y_ref):
    tm = u_ref.shape[1]
    for n in range(u_ref.shape[0]):
        buf_ref[0:CONV_HALO, :] = state_ref[n]
        buf_ref[CONV_HALO:, :] = u_ref[n]
        _causal_conv(buf_ref, rot_ref, w_ref, b_ref, y_ref, t0=0, tm=tm, rows=tm)
        o_ref[n] = _ln_swish(y_ref[...], lg_ref[...], lb_ref[...])


def _conv(u, state, w, b, lg, lb, per_step):
    bsz, tm, d_conv = u.shape
    assert state.shape == (bsz, CONV_HALO, d_conv)
    tile = pl.BlockSpec((per_step, tm, d_conv), lambda i: (i, 0, 0))
    vec = _const_spec((1, d_conv))
    return pl.pallas_call(
        _conv_kernel,
        grid=(bsz // per_step,),
        in_specs=[tile, pl.BlockSpec((per_step, CONV_HALO, d_conv), lambda i: (i, 0, 0)), _const_spec(w.shape),
                  vec, vec, vec],
        out_specs=tile,
        out_shape=jax.ShapeDtypeStruct((bsz, tm, d_conv), F32),
        scratch_shapes=_conv_scratch(tm, d_conv),
        compiler_params=_params(1),
        name="conv",
    )(u, state, w, b, lg, lb)


def _pool_kernel(x_ref, state_ref, g2_ref, g3_ref, w_ref, sc_ref, o_ref, h_ref, buf_ref, ta_ref, tb_ref, *, pos0):
    tm = x_ref.shape[1]
    pos = pos0 + lax.broadcasted_iota(jnp.int32, (tm, 1), 0)
    for bi in range(x_ref.shape[0]):
        x = x_ref[bi]
        h = _rms(x, g2_ref[...])
        h_ref[bi] = h
        mix = _pool_mix(h, state_ref[bi], pos, w_ref, sc_ref, buf_ref, ta_ref, tb_ref)
        o_ref[bi] = x + _rms(mix, g3_ref[...])


def _pool(x, state, g2, g3, w, scale, pos0, per_step):
    b, tm, d = x.shape
    assert state.shape == (b, POOL_HALO, d)
    tile = pl.BlockSpec((per_step, tm, d), lambda i: (i, 0, 0))
    vec = _const_spec((1, d))
    buf = pltpu.VMEM((tm + POOL_HALO, d), F32)
    return pl.pallas_call(
        functools.partial(_pool_kernel, pos0=pos0),
        grid=(b // per_step,),
        in_specs=[tile, pl.BlockSpec((per_step, POOL_HALO, d), lambda i: (i, 0, 0)), vec, vec,
                  _const_spec(w.shape), vec],
        out_specs=[tile, tile],
        out_shape=[jax.ShapeDtypeStruct((b, tm, d), F32), jax.ShapeDtypeStruct((b, tm, d), F32)],
        scratch_shapes=[buf, buf, buf],
        compiler_params=_params(1),
        name="pool",
    )(x, state, g2, g3, w, scale)


def kernel(x_prompt, x_sample, cache_k, cache_v, cache_logf, state_conv, state_pool, page_table,
           norm_g, ffn_w_gate, ffn_w_up, ffn_w_down, mix_w_in, fgate_b, conv_dw_w, conv_dw_b,
           conv_ln_g, conv_ln_b, mix_w_out, pool_w, pool_scale):
    b, s, d = x_prompt.shape
    bd, t_new, _ = x_sample.shape
    depth = norm_g.shape[0]
    d_att = N_HEADS * HEAD_DIM
    d_conv = d - d_att
    page = cache_k.shape[2]
    past_len = page_table.shape[1] * page
    tm_prompt = 512
    m_sample = bd * t_new
    t_pad = 8

    xp = x_prompt.reshape(b * s, d)
    xs = x_sample.reshape(m_sample, d)
    vec = lambda a: a.reshape(1, -1).astype(F32)

    stacks = (ffn_w_gate.astype(F32), ffn_w_up.astype(F32), ffn_w_down.astype(F32))
    halves = [(layer, half) for layer in range(depth) for half in range(2)]
    ffn_weights = [tuple(w[0, 0].astype(BF16) for w in stacks)]

    def ffn_both(xp, xs, layer, half, mix=None, pool=None):
        gp, gq = vec(norm_g[layer, 4 * half]), vec(norm_g[layer, 4 * half + 1])
        n = halves.index((layer, half))
        convert = (stacks, halves[n + 1]) if n + 1 < len(halves) else None
        xp, xs, *rest = _ffn(xp, xs, gp, gq, *ffn_weights[n], 1024, mix, pool, convert)
        tail = rest.pop(0) if pool is not None else None
        ffn_weights.append(tuple(rest))
        return xp, xs, tail

    r = jnp.arange(LANES)
    tri_incl = (r[:, None] <= r[None, :]).astype(F32)
    tri_group = tri_incl * (r[:, None] // t_new == r[None, :] // t_new).astype(F32)
    rp = jnp.arange(page)
    suffix_total = jnp.concatenate([(rp[:, None] > rp[None, :]).astype(F32), jnp.ones((page, page), F32)], axis=1)

    outs_p = {k: [] for k in ("k", "v", "lf", "conv", "pool")}
    outs_s = {k: [] for k in ("k", "v", "lf", "conv", "pool")}
    for layer in range(depth):
        g = norm_g[layer]
        xp, xs, _ = ffn_both(xp, xs, layer, 0)
        mix = pool = None
        if layer % 2 == 0:
            e = layer // 2
            w_in = mix_w_in[e]
            o = 3 * d_att + N_HEADS
            w_cat = jnp.concatenate(
                [w_in[:, :3 * d_att], w_in[:, o:], w_in[:, 3 * d_att:o],
                 jnp.zeros((d, LANES - N_HEADS), w_in.dtype)], axis=1).astype(BF16)
            bf = jnp.concatenate([fgate_b[e].astype(F32), jnp.zeros((LANES - N_HEADS,), F32)]).reshape(1, LANES)
            wa = mix_w_out[e, :d_att].astype(BF16)
            wc = mix_w_out[e, d_att:].astype(BF16)
            cw, cb = conv_dw_w[e].astype(F32), vec(conv_dw_b[e])
            lg, lb = vec(conv_ln_g[e]), vec(conv_ln_b[e])
            g2, g3 = vec(g[2]), vec(g[3])

            q, kt, vt, ktb, vb, lft, cv, u_tail, ccol, crow = _inproj(
                xp.reshape(b, s, d), g2, w_cat, bf, tri_incl, (cw, cb, lg, lb), tm_prompt, d_att, d_conv)
            prompt_att = (q, ktb, vb, ccol, crow)
            cv_p = cv.reshape(b * s, d_conv)
            outs_p["k"].append(kt.reshape(b, N_HEADS, HEAD_DIM, s).transpose(0, 3, 1, 2))
            outs_p["v"].append(vt.reshape(b, N_HEADS, HEAD_DIM, s).transpose(0, 3, 1, 2))
            outs_p["lf"].append(lft.transpose(0, 2, 1))
            outs_p["conv"].append(u_tail[:, CONV_HALO - (CONV_W - 1):])

            q, k, v, lft, u, ccol, crow = _inproj(
                xs.reshape(1, m_sample, d), g2, w_cat, bf, tri_group, None, m_sample, d_att, d_conv)
            lf = lft.reshape(N_HEADS, m_sample).T
            pad_rows = lambda a: jnp.pad(a.reshape(bd, t_new, d_att), ((0, 0), (0, 16 - t_new), (0, 0)))
            cqcol = ccol.reshape(bd, t_new * N_HEADS, 1)
            cqrow = jnp.pad(crow.reshape(N_HEADS, bd, t_new).transpose(1, 0, 2), ((0, 0), (0, 0), (0, 16 - t_new)))
            att_p, att = _attn_decode(
                *prompt_att, 256, page_table, q.reshape(bd, t_new, d_att), pad_rows(k), pad_rows(v), cqcol, cqrow,
                suffix_total, cache_k[e].transpose(0, 2, 3, 1), cache_v[e].transpose(0, 2, 3, 1),
                cache_logf[e].astype(F32).transpose(0, 2, 1), 32)
            att_p = att_p.reshape(b * s, d_att)
            st = state_conv[e].astype(F32)
            cv = _conv(jnp.pad(u.reshape(bd, t_new, d_conv), ((0, 0), (0, t_pad - t_new), (0, 0))),
                       jnp.pad(st, ((0, 0), (CONV_HALO - (CONV_W - 1), 0), (0, 0))), cw, cb, lg, lb, 8)
            mix = (att_p, cv_p, att.reshape(m_sample, d_att), cv[:, :t_new].reshape(m_sample, d_conv), wa, wc, g3)
            outs_s["k"].append(k.reshape(bd, t_new, N_HEADS, HEAD_DIM))
            outs_s["v"].append(v.reshape(bd, t_new, N_HEADS, HEAD_DIM))
            outs_s["lf"].append(lf.reshape(bd, t_new, N_HEADS))
            outs_s["conv"].append(jnp.concatenate([st, u.reshape(bd, t_new, d_conv)], axis=1)[:, t_new:])
        else:
            o = layer // 2
            g2, g3 = vec(g[2]), vec(g[3])
            pw = pool_w[o].astype(BF16)
            sc = vec(pool_scale[o])
            n_keep = POOL_WINDOWS[-1] - 1
            pool = (s, g2, g3, pw, sc)
            st = state_pool[o].astype(F32)
            xs3, hs = _pool(jnp.pad(xs.reshape(bd, t_new, d), ((0, 0), (0, t_pad - t_new), (0, 0))),
                            jnp.pad(st, ((0, 0), (POOL_HALO - n_keep, 0), (0, 0))), g2, g3, pw, sc, past_len, 8)
            xs = xs3[:, :t_new].reshape(m_sample, d)
            outs_s["pool"].append(jnp.concatenate([st, hs[:, :t_new]], axis=1)[:, t_new:])
        xp, xs, tail = ffn_both(xp, xs, layer, 1, mix, pool)
        if pool is not None:
            outs_p["pool"].append(tail[:, POOL_TAIL - (POOL_WINDOWS[-1] - 1):])

    stack = lambda l: jnp.stack(l)
    return (xp.reshape(b, s, d), xs.reshape(bd, t_new, d),
            stack(outs_p["k"]), stack(outs_p["v"]), stack(outs_p["lf"]), stack(outs_p["conv"]),
            stack(outs_p["pool"]),
            stack(outs_s["k"]), stack(outs_s["v"]), stack(outs_s["lf"]), stack(outs_s["conv"]),
            stack(outs_s["pool"]))
```

```python
import functools

import jax
import jax.numpy as jnp
from jax import lax
from jax.experimental import pallas as pl
from jax.experimental.pallas import tpu as pltpu

F32 = jnp.float32
BF16 = jnp.bfloat16

RMS_EPS = 1e-6
LN_EPS = 1e-5
NEG_INF = -1e30
LOG2E = 1.4426950408889634

LANES = 128
HEAD_DIM = 64
N_HEADS = 8
Q_SCALE = HEAD_DIM ** -0.5 * LOG2E
CONV_W = 31
CONV_HALO = 32
POOL_WINDOWS = (2, 4, 8, 16)
POOL_HALO = 32
POOL_TAIL = 16
POOL_BLOCK = 256
VMEM_LIMIT_BYTES = 56 * 1024 * 1024
FFN_VMEM_LIMIT_BYTES = 61 * 1024 * 1024


def _params(n_axes, vmem_limit_bytes=VMEM_LIMIT_BYTES):
    return pltpu.CompilerParams(dimension_semantics=("arbitrary",) * n_axes,
                                vmem_limit_bytes=vmem_limit_bytes)


def _const_spec(shape, single_buffer=False):
    zeros = (0,) * len(shape)
    if single_buffer:
        return pl.BlockSpec(shape, lambda *_: zeros, pipeline_mode=pl.Buffered(1))
    return pl.BlockSpec(shape, lambda *_: zeros)


def _rms(x, g):
    ms = jnp.mean(x * x, axis=-1, keepdims=True)
    return x * lax.rsqrt(ms + RMS_EPS) * g


def _sigmoid(x):
    return 1.0 / (1.0 + jnp.exp(-x))


def _log_sigmoid(x):
    return jnp.minimum(x, 0.0) - jnp.log1p(jnp.exp(-jnp.abs(x)))


def _split3(x):
    hi = x.astype(BF16).astype(F32)
    r = x - hi
    mid = r.astype(BF16).astype(F32)
    return hi, mid, r - mid


def _dot3(x, w):
    hi, mid, lo = _split3(x)
    d = lambda a: jnp.dot(a, w, preferred_element_type=F32)
    return d(hi) + d(mid) + d(lo)


def _pool_mix(h, halo, pos, w_ref, sc_ref, buf_ref, ta_ref, tb_ref):
    rows, d = h.shape
    grp = d // len(POOL_WINDOWS)
    n = rows + POOL_HALO
    buf_ref[0:POOL_HALO, :] = halo
    buf_ref[POOL_HALO:n, :] = h
    ta_ref[8:n, :] = buf_ref[8:n, :] + buf_ref[7:n - 1, :]
    tb_ref[16:n, grp:] = ta_ref[16:n, grp:] + ta_ref[14:n - 2, grp:]
    ta_ref[24:n, 2 * grp:] = tb_ref[24:n, 2 * grp:] + tb_ref[20:n - 4, 2 * grp:]
    s16 = ta_ref[32:n, 3 * grp:] + ta_ref[24:n - 8, 3 * grp:]
    totals = (ta_ref[POOL_HALO:n, 0:grp], tb_ref[POOL_HALO:n, grp:2 * grp],
              ta_ref[POOL_HALO:n, 2 * grp:3 * grp], s16)
    outs = []
    for gi, win in enumerate(POOL_WINDOWS):
        cnt = jnp.minimum(pos + 1, win).astype(F32)
        mixed = totals[gi] / cnt - h[:, gi * grp:(gi + 1) * grp]
        outs.append(jnp.dot(mixed.astype(BF16), w_ref[gi], preferred_element_type=F32))
    return jnp.concatenate(outs, axis=-1) * sc_ref[...]


def _convert_slice_copies(i, slot, src_hbm, which, dst_hbm, stage_in, stage_out, sem_in, sem_out, n_slices):
    ins, outs = [], []
    for k in range(3):
        rows = dst_hbm[k].shape[0] // n_slices
        ins.append(pltpu.make_async_copy(src_hbm[k].at[which[0], which[1], pl.ds(i * rows, rows)],
                                         stage_in[k].at[slot], sem_in.at[slot, k]))
        outs.append(pltpu.make_async_copy(stage_out[k].at[slot], dst_hbm[k].at[pl.ds(i * rows, rows)],
                                          sem_out.at[slot, k]))
    return ins, outs


def _ffn_kernel(*refs, chunk, n_prompt, mix, pool, convert):
    xp_ref, xs_ref, gpre_ref, gpost_ref, wg_ref, wu_ref, wd_ref = refs[:7]
    refs = refs[7:]
    if mix:
        attp_ref, cvp_ref, atts_ref, cvs_ref, wa_ref, wc_ref, gmix_ref = refs[:7]
        refs = refs[7:]
    if pool:
        halo_ref, g2p_ref, g3p_ref, pw_ref, psc_ref = refs[:5]
        refs = refs[5:]
    if convert:
        src_hbm, refs = refs[:3], refs[3:]
    op_ref, os_ref = refs[:2]
    refs = refs[2:]
    if pool:
        tail_ref, refs = refs[0], refs[1:]
    if convert:
        dst_hbm, refs = refs[:3], refs[3:]
    a_ref, refs = refs[0], refs[1:]
    if pool:
        pool_bufs, refs = refs[:3], refs[3:]
    if convert:
        stage_in, stage_out, (sem_in, sem_out) = refs[0:3], refs[3:6], refs[6:8]
    i = pl.program_id(0)

    def slice_copies(step):
        return _convert_slice_copies(step, lax.rem(step, 2), src_hbm, convert, dst_hbm, stage_in, stage_out,
                                     sem_in, sem_out, n_prompt)

    def convert_slice():
        @pl.when(i == 0)
        def _():
            for c in slice_copies(i)[0]:
                c.start()

        @pl.when(i + 1 < n_prompt)
        def _():
            for c in slice_copies(i + 1)[0]:
                c.start()

        ins, outs = slice_copies(i)
        for c in ins:
            c.wait()

        @pl.when(i >= 2)
        def _():
            for c in slice_copies(i - 2)[1]:
                c.wait()

        slot = lax.rem(i, 2)
        for k in range(3):
            stage_out[k][slot] = stage_in[k][slot].astype(BF16)
        for c in outs:
            c.start()

    def pooled(x):
        rows = x.shape[0]
        tile_in_seq = lax.rem(i, pool)
        halo = jnp.where(tile_in_seq == 0, 0.0, _rms(halo_ref[...], g2p_ref[...]))
        out = []
        for r0 in range(0, rows, POOL_BLOCK):
            xb = x[r0:r0 + POOL_BLOCK]
            h = _rms(xb, g2p_ref[...])
            pos = tile_in_seq * rows + r0 + lax.broadcasted_iota(jnp.int32, (POOL_BLOCK, 1), 0)
            out.append(xb + _rms(_pool_mix(h, halo, pos, pw_ref, psc_ref, *pool_bufs), g3p_ref[...]))
            halo = h[POOL_BLOCK - POOL_HALO:]
        tail_ref[0] = h[POOL_BLOCK - POOL_TAIL:]
        return jnp.concatenate(out, axis=0)

    def update(x_ref, o_ref, att_ref=None, cv_ref=None, with_pool=False):
        rows = x_ref.shape[0]
        x = x_ref[...]
        if mix:
            mo = jnp.dot(att_ref[...].astype(BF16), wa_ref[...], preferred_element_type=F32)
            mo = mo + jnp.dot(cv_ref[...].astype(BF16), wc_ref[...], preferred_element_type=F32)
            x = x + _rms(mo, gmix_ref[...])
        if with_pool:
            x = pooled(x)
        h = _rms(x, gpre_ref[...]).astype(BF16)
        d_ff = wg_ref.shape[1]
        for c0 in range(0, d_ff, chunk):
            sl = slice(c0, min(c0 + chunk, d_ff))
            g = jnp.dot(h, wg_ref[:, sl], preferred_element_type=F32)
            u = jnp.dot(h, wu_ref[:, sl], preferred_element_type=F32)
            a_ref[0:rows, sl] = (g * _sigmoid(g) * u).astype(BF16)
        y = jnp.dot(a_ref[0:rows, :], wd_ref[...], preferred_element_type=F32)
        o_ref[...] = x + 0.5 * _rms(y, gpost_ref[...])

    @pl.when(i < n_prompt)
    def _():
        if convert:
            convert_slice()
        update(xp_ref, op_ref, *((attp_ref, cvp_ref) if mix else ()), with_pool=bool(pool))

    @pl.when(i == n_prompt)
    def _():
        if convert:
            for step in (n_prompt - 2, n_prompt - 1):
                for c in slice_copies(step)[1]:
                    c.wait()
        update(xs_ref, os_ref, *((atts_ref, cvs_ref) if mix else ()))


def _ffn(xp, xs, g_pre, g_post, wg, wu, wd, tm, mix=None, pool=None, convert=None):
    m, d = xp.shape
    d_ff = wg.shape[1]
    n_prompt = m // tm
    tile = lambda width: pl.BlockSpec((tm, width), lambda i: (jnp.minimum(i, n_prompt - 1), 0))
    hbm = pl.BlockSpec(memory_space=pl.ANY)
    in_specs = [tile(d), _const_spec(xs.shape), _const_spec((1, d)), _const_spec((1, d)),
                _const_spec(wg.shape, True), _const_spec(wu.shape, True), _const_spec(wd.shape, True)]
    out_specs = [tile(d), _const_spec(xs.shape)]
    out_shape = [jax.ShapeDtypeStruct((m, d), F32), jax.ShapeDtypeStruct(xs.shape, F32)]
    scratch = [pltpu.VMEM((tm, d_ff), BF16)]
    operands = [xp, xs, g_pre, g_post, wg, wu, wd]
    if mix is not None:
        att_p, cv_p, att_s, cv_s, w_att, w_cv, _ = mix
        in_specs += [tile(att_p.shape[1]), tile(cv_p.shape[1]), _const_spec(att_s.shape), _const_spec(cv_s.shape),
                     _const_spec(w_att.shape, True), _const_spec(w_cv.shape, True), _const_spec((1, d))]
        operands += list(mix)
    tiles_per_seq = None
    if pool is not None:
        seq_len, g2p, g3p, pw, psc = pool
        tiles_per_seq = seq_len // tm
        per_halo = tm // POOL_HALO
        in_specs += [pl.BlockSpec((POOL_HALO, d),
                                  lambda i: (jnp.maximum(jnp.minimum(i, n_prompt - 1) * per_halo - 1, 0), 0)),
                     _const_spec((1, d)), _const_spec((1, d)), _const_spec(pw.shape, True), _const_spec((1, d))]
        operands += [xp, g2p, g3p, pw, psc]
        out_specs += [pl.BlockSpec((1, POOL_TAIL, d),
                                   lambda i: (jnp.minimum(i, n_prompt - 1) // tiles_per_seq, 0, 0))]
        out_shape += [jax.ShapeDtypeStruct((m // seq_len, POOL_TAIL, d), F32)]
        scratch += [pltpu.VMEM((POOL_BLOCK + POOL_HALO, d), F32)] * 3
    if convert is not None:
        stacks, which = convert
        in_specs += [hbm] * 3
        operands += list(stacks)
        out_specs += [hbm] * 3
        shapes = [(d, d_ff), (d, d_ff), (d_ff, d)]
        out_shape += [jax.ShapeDtypeStruct(sh, BF16) for sh in shapes]
        scratch += [pltpu.VMEM((2, r // n_prompt, c), F32) for r, c in shapes]
        scratch += [pltpu.VMEM((2, r // n_prompt, c), BF16) for r, c in shapes]
        scratch += [pltpu.SemaphoreType.DMA((2, 3)), pltpu.SemaphoreType.DMA((2, 3))]
    return pl.pallas_call(
        functools.partial(_ffn_kernel, chunk=256, n_prompt=n_prompt, mix=mix is not None, pool=tiles_per_seq,
                          convert=None if convert is None else convert[1]),
        grid=(n_prompt + 1,),
        in_specs=in_specs,
        out_specs=out_specs,
        out_shape=out_shape,
        scratch_shapes=scratch,
        compiler_params=_params(1, FFN_VMEM_LIMIT_BYTES),
        name="ffn",
    )(*operands)


def _ln_swish(y, g, b):
    mu = jnp.mean(y, axis=-1, keepdims=True)
    d = y - mu
    var = jnp.mean(d * d, axis=-1, keepdims=True)
    yn = d * lax.rsqrt(var + LN_EPS) * g + b
    return yn * _sigmoid(yn)


def _causal_conv(buf_ref, rot_ref, w_ref, b_ref, y_ref, *, t0, tm, rows):
    d_conv = buf_ref.shape[1]
    base = CONV_HALO - (CONV_W - 1)
    n_rot = tm + CONV_HALO - 8
    for r in range(1, 8):
        rot_ref[r - 1, t0:t0 + n_rot, :] = buf_ref[t0 + r:t0 + r + n_rot, :]
    for cb in range(d_conv // LANES):
        ls = slice(cb * LANES, (cb + 1) * LANES)
        for rb in range(tm // rows):
            acc = jnp.zeros((rows, LANES), F32)
            for tap in range(CONV_W):
                a, r = divmod(base + tap, 8)
                r0 = t0 + rb * rows + 8 * a
                src = buf_ref[r0:r0 + rows, ls] if r == 0 else rot_ref[r - 1, r0:r0 + rows, ls]
                acc = acc + w_ref[tap:tap + 1, ls] * src
            y_ref[t0 + rb * rows:t0 + (rb + 1) * rows, ls] = acc + b_ref[:, ls]


def _conv_scratch(tm, d_conv):
    return [pltpu.VMEM((tm + CONV_HALO, d_conv), F32), pltpu.VMEM((7, tm + CONV_HALO - 8, d_conv), F32),
            pltpu.VMEM((tm, d_conv), F32)]


def _inproj_kernel(x_ref, g_ref, wqkv_ref, wag_ref, wf_ref, bf_ref, tri_ref, *refs, d_att, d_conv, prompt, sub):
    if prompt:
        (cw_ref, cb_ref, lg_ref, lb_ref,
         q_ref, kt_ref, vt_ref, ktb_ref, vb_ref, lft_ref, cv_ref, utail_ref, ccol_ref, crow_ref,
         carry_ref, buf_ref, rot_ref, y_ref) = refs
    else:
        q_ref, k_ref, v_ref, lft_ref, u_ref, ccol_ref, crow_ref, carry_ref = refs
    tm = x_ref.shape[1]
    j = pl.program_id(1)

    @pl.when(j == 0)
    def _():
        carry_ref[...] = jnp.zeros_like(carry_ref)
        if prompt:
            buf_ref[0:CONV_HALO, :] = jnp.zeros((CONV_HALO, d_conv), F32)

    if prompt:
        @pl.when(j > 0)
        def _():
            buf_ref[0:CONV_HALO, :] = buf_ref[tm:tm + CONV_HALO, :]

    carry = carry_ref[...]
    tri = tri_ref[...]
    for t0 in range(0, tm, sub):
        rs = slice(t0, t0 + sub)
        h = _rms(x_ref[0, rs, :], g_ref[...]).astype(BF16)
        z = jnp.dot(h, wqkv_ref[...], preferred_element_type=F32)
        k = z[:, d_att:2 * d_att]
        v = z[:, 2 * d_att:3 * d_att]
        q_ref[0, rs, :] = (z[:, :d_att] * Q_SCALE).astype(BF16)
        z = jnp.dot(h, wag_ref[...], preferred_element_type=F32)
        u = z[:, :d_conv] * _sigmoid(z[:, d_conv:])
        fg = jnp.dot(h, wf_ref[...], preferred_element_type=F32)
        if prompt:
            kt = k.T
            kt_ref[0, :, rs] = kt
            vt_ref[0, :, rs] = v.T
            ktb_ref[0, :, rs] = kt.astype(BF16)
            vb_ref[0, rs, :] = v.astype(BF16)
            buf_ref[CONV_HALO + t0:CONV_HALO + t0 + sub, :] = u
            _causal_conv(buf_ref, rot_ref, cw_ref, cb_ref, y_ref, t0=t0, tm=sub, rows=min(sub, 128))
            cv_ref[0, rs, :] = _ln_swish(y_ref[rs, :], lg_ref[...], lb_ref[...]).astype(cv_ref.dtype)
        else:
            k_ref[0, rs, :] = k
            v_ref[0, rs, :] = v
            u_ref[0, rs, :] = u
        lf = _log_sigmoid(fg + bf_ref[...])
        lft = lf.T[:N_HEADS]
        lft_ref[0, :, rs] = lft

        chunks = []
        for c in range(sub // LANES):
            cs = _dot3(lft[:, c * LANES:(c + 1) * LANES], tri) + carry
            chunks.append(cs)
            carry = jnp.broadcast_to(cs[:, LANES - 1:LANES], cs.shape)
        crow = jnp.concatenate(chunks, axis=1) * LOG2E
        crow_ref[0, :, rs] = crow
        cfull = jnp.concatenate([crow, jnp.zeros((LANES - N_HEADS, sub), F32)], axis=0)
        ccol_ref[0, rs, :] = cfull.T[:, :N_HEADS]
    carry_ref[...] = carry

    if prompt:
        @pl.when(j == pl.num_programs(1) - 1)
        def _():
            utail_ref[0] = buf_ref[tm:tm + CONV_HALO, :]


def _inproj(x, g, w, bf, tri, conv, tm, d_att, d_conv):
    b, s, d = x.shape
    prompt = conv is not None
    tok = lambda width: pl.BlockSpec((1, tm, width), lambda i, j: (i, j, 0))
    tr = lambda rows: pl.BlockSpec((1, rows, tm), lambda i, j: (i, 0, j))
    shp = lambda width, dt: jax.ShapeDtypeStruct((b, s, width), dt)
    sht = lambda rows, dt: jax.ShapeDtypeStruct((b, rows, s), dt)
    in_specs = [tok(d), _const_spec((1, d))] + [_const_spec(wk.shape, True) for wk in w] + [
        _const_spec((1, LANES)), _const_spec((LANES, LANES))]
    scratch = [pltpu.VMEM((N_HEADS, LANES), F32)]
    if prompt:
        in_specs += [_const_spec(conv[0].shape)] + [_const_spec((1, d_conv))] * 3
        out_specs = [tok(d_att), tr(d_att), tr(d_att), tr(d_att), tok(d_att), tr(N_HEADS), tok(d_conv),
                     pl.BlockSpec((1, CONV_HALO, d_conv), lambda i, j: (i, 0, 0))]
        out_shape = [shp(d_att, BF16), sht(d_att, F32), sht(d_att, F32), sht(d_att, BF16), shp(d_att, BF16),
                     sht(N_HEADS, F32), shp(d_conv, BF16), jax.ShapeDtypeStruct((b, CONV_HALO, d_conv), F32)]
        scratch += _conv_scratch(tm, d_conv)
    else:
        out_specs = [tok(d_att), tok(d_att), tok(d_att), tr(N_HEADS), tok(d_conv)]
        out_shape = [shp(d_att, BF16), shp(d_att, F32), shp(d_att, F32), sht(N_HEADS, F32), shp(d_conv, F32)]
    return pl.pallas_call(
        functools.partial(_inproj_kernel, d_att=d_att, d_conv=d_conv, prompt=prompt, sub=min(tm, 256)),
        grid=(b, s // tm),
        in_specs=in_specs,
        out_specs=out_specs + [tok(N_HEADS), tr(N_HEADS)],
        out_shape=out_shape + [shp(N_HEADS, F32), sht(N_HEADS, F32)],
        scratch_shapes=scratch,
        compiler_params=_params(2),
        name="inproj",
    )(x, g, *w, bf, tri, *(conv or ()))


def _attn_decode_kernel(pt_ref, q_ref, kt_ref, v_ref, ccol_ref, crow_ref,
                        qs_ref, kn_ref, vn_ref, cqcol_ref, cqrow_ref, us_ref, kc_ref, vc_ref, lc_ref,
                        o_ref, os_ref, kbuf, vbuf, lbuf, sem, *, tq, n_group):
    s_len = q_ref.shape[1]
    pair = pl.program_id(1)
    t = pl.program_id(0) * pl.num_programs(1) + pair
    n_samples = pl.num_programs(0) * pl.num_programs(1)
    n_pages = pt_ref.shape[1]
    n_sub = n_pages // n_group
    assert n_sub == 2
    t_new = qs_ref.shape[1]
    d_att = qs_ref.shape[2]
    page = kbuf.shape[-1]
    n_rows = t_new * N_HEADS
    nt_dims = (((1,), (1,)), ((), ()))

    def page_copies(sample, sub, real_pages=True):
        out = []
        for g in range(n_group):
            p = pt_ref[sample, n_pages - 1 - (sub * n_group + g)] if real_pages else 0
            out.append(pltpu.make_async_copy(kc_ref.at[p], kbuf.at[sub, g], sem.at[sub, 0]))
            out.append(pltpu.make_async_copy(vc_ref.at[p], vbuf.at[sub, g], sem.at[sub, 1]))
            out.append(pltpu.make_async_copy(lc_ref.at[p], lbuf.at[sub, g], sem.at[sub, 2]))
        return out

    @pl.when(t == 0)
    def _():
        for c in page_copies(t, 0):
            c.start()

    for c in page_copies(t, 1):
        c.start()

    cqcol = cqcol_ref[0]
    st = {}

    def decode_half(sub):
        if sub == 0:
            q = qs_ref[0].astype(F32)
            hsub = lax.broadcasted_iota(jnp.int32, (N_HEADS, d_att), 0)
            hlane = lax.broadcasted_iota(jnp.int32, (N_HEADS, d_att), 1)
            own = (hlane // HEAD_DIM) == hsub
            rows = [jnp.where(own, jnp.broadcast_to(q[i:i + 1, :], (N_HEADS, d_att)), 0.0) for i in range(t_new)]
            st["qbd"] = jnp.concatenate(rows, axis=0).astype(BF16)
            s = lax.dot_general(st["qbd"], kn_ref[0].astype(BF16), nt_dims, preferred_element_type=F32)
            cqk = jnp.concatenate([cqrow_ref[0]] * t_new, axis=0)
            r = lax.broadcasted_iota(jnp.int32, s.shape, 0)
            c = lax.broadcasted_iota(jnp.int32, s.shape, 1)
            s = jnp.where(c <= r // N_HEADS, s + (cqcol - cqk), NEG_INF)
            st["m"] = jnp.max(s, axis=-1, keepdims=True)
            pr = jnp.exp2(s - st["m"])
            st["l"] = jnp.sum(pr, axis=-1, keepdims=True)
            st["acc"] = jnp.dot(pr.astype(BF16), vn_ref[0].astype(BF16), preferred_element_type=F32)
            st["carry"] = jnp.zeros((N_HEADS, page), F32)

        for c in page_copies(t, sub, real_pages=False):
            c.wait()

        qbd, carry = st["qbd"], st["carry"]
        us = us_ref[...]
        s_parts = []
        r_all = _dot3(lbuf[sub].reshape(n_group * N_HEADS, page), us)
        for g in range(n_group):
            r = r_all[g * N_HEADS:(g + 1) * N_HEADS]
            suffix = (r[:, :page] + carry) * LOG2E
            carry = carry + r[:, page:]
            bias = jnp.concatenate([suffix] * t_new, axis=0) + cqcol
            kt = kbuf[sub, g].reshape(d_att, page).astype(BF16)
            s_parts.append(jnp.dot(qbd, kt, preferred_element_type=F32) + bias)
        s = jnp.concatenate(s_parts, axis=1)
        m_new = jnp.maximum(st["m"], jnp.max(s, axis=-1, keepdims=True))
        alpha = jnp.exp2(st["m"] - m_new)
        pr = jnp.exp2(s - m_new)
        prb = pr.astype(BF16)
        acc = alpha * st["acc"]
        for g in range(n_group):
            vt = vbuf[sub, g].reshape(d_att, page).astype(BF16)
            acc = acc + lax.dot_general(prb[:, g * page:(g + 1) * page], vt, nt_dims, preferred_element_type=F32)
        st.update(m=m_new, l=alpha * st["l"] + jnp.sum(pr, axis=-1, keepdims=True), acc=acc, carry=carry)

        if sub == 0:
            @pl.when(t + 1 < n_samples)
            def _():
                for c in page_copies(t + 1, 0):
                    c.start()

        if sub == n_sub - 1:
            o = acc * (1.0 / st["l"])
            hsub = lax.broadcasted_iota(jnp.int32, (N_HEADS, d_att), 0)
            hlane = lax.broadcasted_iota(jnp.int32, (N_HEADS, d_att), 1)
            own = (hlane // HEAD_DIM) == hsub
            rows = [jnp.sum(jnp.where(own, o[i * N_HEADS:(i + 1) * N_HEADS], 0.0), axis=0, keepdims=True)
                    for i in range(t_new)]
            os_ref[0] = jnp.concatenate(rows, axis=0).astype(os_ref.dtype)

    row = lax.broadcasted_iota(jnp.int32, (tq, tq), 0)
    col = lax.broadcasted_iota(jnp.int32, (tq, tq), 1)
    causal = col <= row
    low = lax.broadcasted_iota(jnp.int32, (tq, LANES), 1) < HEAD_DIM
    ccol = ccol_ref[0]
    head_lane = lax.broadcasted_iota(jnp.int32, ccol.shape, 1)
    cq_all = []
    sub16 = lax.broadcasted_iota(jnp.int32, (2 * LANES // N_HEADS, s_len), 0)
    extra = jnp.zeros(sub16.shape, F32)
    for hh in range(2):
        head = 2 * pair + hh
        cq_all.append(jnp.sum(jnp.where(head_lane == head, ccol, 0.0), axis=-1, keepdims=True))
        for n, piece in enumerate(_split3(-crow_ref[0, pl.ds(head, 1), :])):
            extra = jnp.where(sub16 == 3 * hh + n, piece, extra)
    kt_bias = jnp.concatenate([kt_ref[0], extra.astype(BF16),
                               jnp.zeros((LANES - extra.shape[0], s_len), BF16)], axis=0)
    lane = lax.broadcasted_iota(jnp.int32, (tq, LANES), 1)

    def scores(i, hh):
        k0 = i * tq
        q2 = q_ref[0, k0:k0 + tq, :]
        zero = jnp.zeros_like(q2)
        qh = jnp.where(low, q2, zero) if hh == 0 else jnp.where(low, zero, q2)
        ones = jnp.where((lane >= 3 * hh) & (lane < 3 * hh + 3), 1.0, 0.0).astype(BF16)
        qh = jnp.concatenate([qh, ones], axis=1)
        s_diag = jnp.dot(qh, kt_bias[:, k0:k0 + tq], preferred_element_type=F32)
        s_off = jnp.dot(qh, kt_bias[:, :k0], preferred_element_type=F32) if i > 0 else None
        return s_diag, s_off

    v_ones = jnp.concatenate([v_ref[0], jnp.ones((s_len, LANES), BF16)], axis=1)

    def finish(i, hh, s_diag, s_off):
        k0 = i * tq
        cq = cq_all[hh][k0:k0 + tq]
        s_diag = jnp.where(causal, s_diag, NEG_INF)
        m = jnp.max(s_diag, axis=-1, keepdims=True)
        if i > 0:
            m = jnp.maximum(m, jnp.max(s_off, axis=-1, keepdims=True))
        shift = cq - (m + cq)
        pv = jnp.dot(jnp.exp2(s_diag + shift).astype(BF16), v_ones[k0:k0 + tq], preferred_element_type=F32)
        if i > 0:
            pv = pv + jnp.dot(jnp.exp2(s_off + shift).astype(BF16), v_ones[:k0], preferred_element_type=F32)
        return pv[:, :LANES] * (1.0 / pv[:, LANES:LANES + 1])

    blocks = [(i, hh) for i in range(s_len // tq) for hh in range(2)]
    work = [i + 1 for i, _ in blocks]
    first_half_at = next(n for n in range(len(blocks)) if 2 * sum(work[:n + 1]) >= sum(work))
    nxt = scores(*blocks[0])
    halves = []
    for n, (i, hh) in enumerate(blocks):
        cur = nxt
        if n + 1 < len(blocks):
            nxt = scores(*blocks[n + 1])
        halves.append(finish(i, hh, *cur))
        if hh == 1:
            o_ref[0, i * tq:(i + 1) * tq, :] = jnp.where(low, halves[0], halves[1]).astype(BF16)
            halves = []
        if n == first_half_at:
            decode_half(0)
    decode_half(1)


def _attn_decode(q, ktb, vb, ccol, crow, tq,
                 page_table, qs, kn, vn, cqcol, cqrow, us, cache_k, cache_v, cache_lf, n_group):
    b, s, d_att = q.shape
    bd, t_new, _ = qs.shape
    n_pairs = d_att // LANES
    assert bd == b * n_pairs
    page = cache_k.shape[3]
    n_rows = t_new * N_HEADS
    cols = pl.BlockSpec((1, s, LANES), lambda i, j, pt: (i, 0, j))
    per_s = lambda shape: pl.BlockSpec((1,) + shape, lambda i, j, pt: (i * n_pairs + j,) + (0,) * len(shape))
    hbm = pl.BlockSpec(memory_space=pl.ANY)
    grid_spec = pltpu.PrefetchScalarGridSpec(
        num_scalar_prefetch=1,
        grid=(b, n_pairs),
        in_specs=[cols, pl.BlockSpec((1, LANES, s), lambda i, j, pt: (i, j, 0)), cols,
                  pl.BlockSpec((1, s, N_HEADS), lambda i, j, pt: (i, 0, 0)),
                  pl.BlockSpec((1, N_HEADS, s), lambda i, j, pt: (i, 0, 0)),
                  per_s((t_new, d_att)), per_s(kn.shape[1:]), per_s(vn.shape[1:]), per_s((n_rows, 1)),
                  per_s(cqrow.shape[1:]), pl.BlockSpec(us.shape, lambda i, j, pt: (0, 0)), hbm, hbm, hbm],
        out_specs=[cols, per_s((t_new, d_att))],
        scratch_shapes=[pltpu.VMEM((2, n_group, N_HEADS, HEAD_DIM, page), F32),
                        pltpu.VMEM((2, n_group, N_HEADS, HEAD_DIM, page), F32),
                        pltpu.VMEM((2, n_group, N_HEADS, page), F32),
                        pltpu.SemaphoreType.DMA((2, 3))],
    )
    return pl.pallas_call(
        functools.partial(_attn_decode_kernel, tq=tq, n_group=n_group),
        grid_spec=grid_spec,
        out_shape=[jax.ShapeDtypeStruct((b, s, d_att), BF16), jax.ShapeDtypeStruct((bd, t_new, d_att), BF16)],
        compiler_params=_params(2),
        name="attn_decode",
    )(page_table, q, ktb, vb, ccol, crow, qs, kn, vn, cqcol, cqrow, us, cache_k, cache_v, cache_lf)


def _conv_kernel(u_ref, state_ref, w_ref, b_ref, lg_ref, lb_ref, o_ref, buf_ref, rot_ref, y_ref):
    tm = u_ref.shape[1]
    for n in range(u_ref.shape[0]):
        buf_ref[0:CONV_HALO, :] = state_ref[n]
        buf_ref[CONV_HALO:, :] = u_ref[n]
        _causal_conv(buf_ref, rot_ref, w_ref, b_ref, y_ref, t0=0, tm=tm, rows=tm)
        o_ref[n] = _ln_swish(y_ref[...], lg_ref[...], lb_ref[...])


def _conv(u, state, w, b, lg, lb, per_step):
    bsz, tm, d_conv = u.shape
    assert state.shape == (bsz, CONV_HALO, d_conv)
    tile = pl.BlockSpec((per_step, tm, d_conv), lambda i: (i, 0, 0))
    vec = _const_spec((1, d_conv))
    return pl.pallas_call(
        _conv_kernel,
        grid=(bsz // per_step,),
        in_specs=[tile, pl.BlockSpec((per_step, CONV_HALO, d_conv), lambda i: (i, 0, 0)), _const_spec(w.shape),
                  vec, vec, vec],
        out_specs=tile,
        out_shape=jax.ShapeDtypeStruct((bsz, tm, d_conv), F32),
        scratch_shapes=_conv_scratch(tm, d_conv),
        compiler_params=_params(1),
        name="conv",
    )(u, state, w, b, lg, lb)


def _pool_kernel(x_ref, state_ref, g2_ref, g3_ref, w_ref, sc_ref, o_ref, h_ref, buf_ref, ta_ref, tb_ref, *, pos0):
    tm = x_ref.shape[1]
    pos = pos0 + lax.broadcasted_iota(jnp.int32, (tm, 1), 0)
    for bi in range(x_ref.shape[0]):
        x = x_ref[bi]
        h = _rms(x, g2_ref[...])
        h_ref[bi] = h
        mix = _pool_mix(h, state_ref[bi], pos, w_ref, sc_ref, buf_ref, ta_ref, tb_ref)
        o_ref[bi] = x + _rms(mix, g3_ref[...])


def _pool(x, state, g2, g3, w, scale, pos0, per_step):
    b, tm, d = x.shape
    assert state.shape == (b, POOL_HALO, d)
    tile = pl.BlockSpec((per_step, tm, d), lambda i: (i, 0, 0))
    vec = _const_spec((1, d))
    buf = pltpu.VMEM((tm + POOL_HALO, d), F32)
    return pl.pallas_call(
        functools.partial(_pool_kernel, pos0=pos0),
        grid=(b // per_step,),
        in_specs=[tile, pl.BlockSpec((per_step, POOL_HALO, d), lambda i: (i, 0, 0)), vec, vec,
                  _const_spec(w.shape), vec],
        out_specs=[tile, tile],
        out_shape=[jax.ShapeDtypeStruct((b, tm, d), F32), jax.ShapeDtypeStruct((b, tm, d), F32)],
        scratch_shapes=[buf, buf, buf],
        compiler_params=_params(1),
        name="pool",
    )(x, state, g2, g3, w, scale)


def kernel(x_prompt, x_sample, cache_k, cache_v, cache_logf, state_conv, state_pool, page_table,
           norm_g, ffn_w_gate, ffn_w_up, ffn_w_down, mix_w_in, fgate_b, conv_dw_w, conv_dw_b,
           conv_ln_g, conv_ln_b, mix_w_out, pool_w, pool_scale):
    b, s, d = x_prompt.shape
    bd, t_new, _ = x_sample.shape
    depth = norm_g.shape[0]
    d_att = N_HEADS * HEAD_DIM
    d_conv = d - d_att
    page = cache_k.shape[2]
    past_len = page_table.shape[1] * page
    tm_prompt = 512
    m_sample = bd * t_new
    t_pad = 8

    xp = x_prompt.reshape(b * s, d)
    xs = x_sample.reshape(m_sample, d)
    vec = lambda a: a.reshape(1, -1).astype(F32)

    stacks = (ffn_w_gate.astype(F32), ffn_w_up.astype(F32), ffn_w_down.astype(F32))
    halves = [(layer, half) for layer in range(depth) for half in range(2)]
    ffn_weights = [tuple(w[0, 0].astype(BF16) for w in stacks)]

    def ffn_both(xp, xs, layer, half, mix=None, pool=None):
        gp, gq = vec(norm_g[layer, 4 * half]), vec(norm_g[layer, 4 * half + 1])
        n = halves.index((layer, half))
        convert = (stacks, halves[n + 1]) if n + 1 < len(halves) else None
        xp, xs, *rest = _ffn(xp, xs, gp, gq, *ffn_weights[n], 1024, mix, pool, convert)
        tail = rest.pop(0) if pool is not None else None
        ffn_weights.append(tuple(rest))
        return xp, xs, tail

    r = jnp.arange(LANES)
    tri_incl = (r[:, None] <= r[None, :]).astype(F32)
    tri_group = tri_incl * (r[:, None] // t_new == r[None, :] // t_new).astype(F32)
    rp = jnp.arange(page)
    suffix_total = jnp.concatenate([(rp[:, None] > rp[None, :]).astype(F32), jnp.ones((page, page), F32)], axis=1)

    outs_p = {k: [] for k in ("k", "v", "lf", "conv", "pool")}
    outs_s = {k: [] for k in ("k", "v", "lf", "conv", "pool")}
    for layer in range(depth):
        g = norm_g[layer]
        xp, xs, _ = ffn_both(xp, xs, layer, 0)
        mix = pool = None
        if layer % 2 == 0:
            e = layer // 2
            w_in = mix_w_in[e]
            o = 3 * d_att + N_HEADS
            w_cat = (w_in[:, :3 * d_att].astype(BF16), w_in[:, o:].astype(BF16),
                     jnp.pad(w_in[:, 3 * d_att:o], ((0, 0), (0, LANES - N_HEADS))).astype(BF16))
            bf = jnp.concatenate([fgate_b[e].astype(F32), jnp.zeros((LANES - N_HEADS,), F32)]).reshape(1, LANES)
            wa = mix_w_out[e, :d_att].astype(BF16)
            wc = mix_w_out[e, d_att:].astype(BF16)
            cw, cb = conv_dw_w[e].astype(F32), vec(conv_dw_b[e])
            lg, lb = vec(conv_ln_g[e]), vec(conv_ln_b[e])
            g2, g3 = vec(g[2]), vec(g[3])

            q, kt, vt, ktb, vb, lft, cv, u_tail, ccol, crow = _inproj(
                xp.reshape(b, s, d), g2, w_cat, bf, tri_incl, (cw, cb, lg, lb), tm_prompt, d_att, d_conv)
            prompt_att = (q, ktb, vb, ccol, crow)
            cv_p = cv.reshape(b * s, d_conv)
            outs_p["k"].append(kt.reshape(b, N_HEADS, HEAD_DIM, s).transpose(0, 3, 1, 2))
            outs_p["v"].append(vt.reshape(b, N_HEADS, HEAD_DIM, s).transpose(0, 3, 1, 2))
            outs_p["lf"].append(lft.transpose(0, 2, 1))
            outs_p["conv"].append(u_tail[:, CONV_HALO - (CONV_W - 1):])

            q, k, v, lft, u, ccol, crow = _inproj(
                xs.reshape(1, m_sample, d), g2, w_cat, bf, tri_group, None, m_sample, d_att, d_conv)
            lf = lft.reshape(N_HEADS, m_sample).T
            pad_rows = lambda a: jnp.pad(a.reshape(bd, t_new, d_att), ((0, 0), (0, 16 - t_new), (0, 0)))
            cqcol = ccol.reshape(bd, t_new * N_HEADS, 1)
            cqrow = jnp.pad(crow.reshape(N_HEADS, bd, t_new).transpose(1, 0, 2), ((0, 0), (0, 0), (0, 16 - t_new)))
            att_p, att = _attn_decode(
                *prompt_att, 256, page_table, q.reshape(bd, t_new, d_att), pad_rows(k), pad_rows(v), cqcol, cqrow,
                suffix_total, cache_k[e].transpose(0, 2, 3, 1), cache_v[e].transpose(0, 2, 3, 1),
                cache_logf[e].astype(F32).transpose(0, 2, 1), 32)
            att_p = att_p.reshape(b * s, d_att)
            st = state_conv[e].astype(F32)
            cv = _conv(jnp.pad(u.reshape(bd, t_new, d_conv), ((0, 0), (0, t_pad - t_new), (0, 0))),
                       jnp.pad(st, ((0, 0), (CONV_HALO - (CONV_W - 1), 0), (0, 0))), cw, cb, lg, lb, 8)
            mix = (att_p, cv_p, att.reshape(m_sample, d_att), cv[:, :t_new].reshape(m_sample, d_conv), wa, wc, g3)
            outs_s["k"].append(k.reshape(bd, t_new, N_HEADS, HEAD_DIM))
            outs_s["v"].append(v.reshape(bd, t_new, N_HEADS, HEAD_DIM))
            outs_s["lf"].append(lf.reshape(bd, t_new, N_HEADS))
            outs_s["conv"].append(jnp.concatenate([st, u.reshape(bd, t_new, d_conv)], axis=1)[:, t_new:])
        else:
            o = layer // 2
            g2, g3 = vec(g[2]), vec(g[3])
            pw = pool_w[o].astype(BF16)
            sc = vec(pool_scale[o])
            n_keep = POOL_WINDOWS[-1] - 1
            pool = (s, g2, g3, pw, sc)
            st = state_pool[o].astype(F32)
            xs3, hs = _pool(jnp.pad(xs.reshape(bd, t_new, d), ((0, 0), (0, t_pad - t_new), (0, 0))),
                            jnp.pad(st, ((0, 0), (POOL_HALO - n_keep, 0), (0, 0))), g2, g3, pw, sc, past_len, 8)
            xs = xs3[:, :t_new].reshape(m_sample, d)
            outs_s["pool"].append(jnp.concatenate([st, hs[:, :t_new]], axis=1)[:, t_new:])
        xp, xs, tail = ffn_both(xp, xs, layer, 1, mix, pool)
        if pool is not None:
            outs_p["pool"].append(tail[:, POOL_TAIL - (POOL_WINDOWS[-1] - 1):])

    stack = lambda l: jnp.stack(l)
    return (xp.reshape(b, s, d), xs.reshape(bd, t_new, d),
            stack(outs_p["k"]), stack(outs_p["v"]), stack(outs_p["lf"]), stack(outs_p["conv"]),
            stack(outs_p["pool"]),
            stack(outs_s["k"]), stack(outs_s["v"]), stack(outs_s["lf"]), stack(outs_s["conv"]),
            stack(outs_s["pool"]))
```

```python
import functools

import jax
import jax.numpy as jnp
from jax import lax
from jax.experimental import pallas as pl
from jax.experimental.pallas import tpu as pltpu

F32 = jnp.float32
BF16 = jnp.bfloat16

RMS_EPS = 1e-6
LN_EPS = 1e-5
NEG_INF = -1e30
LOG2E = 1.4426950408889634

LANES = 128
HEAD_DIM = 64
N_HEADS = 8
Q_SCALE = HEAD_DIM ** -0.5 * LOG2E
CONV_W = 31
CONV_HALO = 32
POOL_WINDOWS = (2, 4, 8, 16)
POOL_HALO = 32
POOL_TAIL = 16
POOL_BLOCK = 256
VMEM_LIMIT_BYTES = 56 * 1024 * 1024
FFN_VMEM_LIMIT_BYTES = 61 * 1024 * 1024


def _params(n_axes, vmem_limit_bytes=VMEM_LIMIT_BYTES):
    return pltpu.CompilerParams(dimension_semantics=("arbitrary",) * n_axes,
                                vmem_limit_bytes=vmem_limit_bytes)


def _const_spec(shape, single_buffer=False):
    zeros = (0,) * len(shape)
    if single_buffer:
        return pl.BlockSpec(shape, lambda *_: zeros, pipeline_mode=pl.Buffered(1))
    return pl.BlockSpec(shape, lambda *_: zeros)


def _rms(x, g):
    ms = jnp.mean(x * x, axis=-1, keepdims=True)
    return x * lax.rsqrt(ms + RMS_EPS) * g


def _sigmoid(x):
    return 1.0 / (1.0 + jnp.exp(-x))


def _log_sigmoid(x):
    return jnp.minimum(x, 0.0) - jnp.log1p(jnp.exp(-jnp.abs(x)))


def _split3(x):
    hi = x.astype(BF16).astype(F32)
    r = x - hi
    mid = r.astype(BF16).astype(F32)
    return hi, mid, r - mid


def _dot3(x, w):
    hi, mid, lo = _split3(x)
    d = lambda a: jnp.dot(a, w, preferred_element_type=F32)
    return d(hi) + d(mid) + d(lo)


def _pool_mix(h, halo, pos, w_ref, sc_ref, buf_ref, ta_ref, tb_ref):
    rows, d = h.shape
    grp = d // len(POOL_WINDOWS)
    n = rows + POOL_HALO
    buf_ref[0:POOL_HALO, :] = halo
    buf_ref[POOL_HALO:n, :] = h
    ta_ref[8:n, :] = buf_ref[8:n, :] + buf_ref[7:n - 1, :]
    tb_ref[16:n, grp:] = ta_ref[16:n, grp:] + ta_ref[14:n - 2, grp:]
    ta_ref[24:n, 2 * grp:] = tb_ref[24:n, 2 * grp:] + tb_ref[20:n - 4, 2 * grp:]
    s16 = ta_ref[32:n, 3 * grp:] + ta_ref[24:n - 8, 3 * grp:]
    totals = (ta_ref[POOL_HALO:n, 0:grp], tb_ref[POOL_HALO:n, grp:2 * grp],
              ta_ref[POOL_HALO:n, 2 * grp:3 * grp], s16)
    outs = []
    for gi, win in enumerate(POOL_WINDOWS):
        cnt = jnp.minimum(pos + 1, win).astype(F32)
        mixed = totals[gi] / cnt - h[:, gi * grp:(gi + 1) * grp]
        outs.append(jnp.dot(mixed.astype(BF16), w_ref[gi], preferred_element_type=F32))
    return jnp.concatenate(outs, axis=-1) * sc_ref[...]


def _convert_slice_copies(i, slot, src_hbm, which, dst_hbm, stage_in, stage_out, sem_in, sem_out, n_slices):
    ins, outs = [], []
    for k in range(3):
        rows = dst_hbm[k].shape[0] // n_slices
        ins.append(pltpu.make_async_copy(src_hbm[k].at[which[0], which[1], pl.ds(i * rows, rows)],
                                         stage_in[k].at[slot], sem_in.at[slot, k]))
        outs.append(pltpu.make_async_copy(stage_out[k].at[slot], dst_hbm[k].at[pl.ds(i * rows, rows)],
                                          sem_out.at[slot, k]))
    return ins, outs


def _ffn_kernel(*refs, chunk, n_prompt, mix, pool, convert):
    xp_ref, xs_ref, gpre_ref, gpost_ref, wg_ref, wu_ref, wd_ref = refs[:7]
    refs = refs[7:]
    if mix:
        attp_ref, cvp_ref, atts_ref, cvs_ref, wa_ref, wc_ref, gmix_ref = refs[:7]
        refs = refs[7:]
    if pool:
        halo_ref, g2p_ref, g3p_ref, pw_ref, psc_ref = refs[:5]
        refs = refs[5:]
    if convert:
        src_hbm, refs = refs[:3], refs[3:]
    op_ref, os_ref = refs[:2]
    refs = refs[2:]
    if pool:
        tail_ref, refs = refs[0], refs[1:]
    if convert:
        dst_hbm, refs = refs[:3], refs[3:]
    a_ref, refs = refs[0], refs[1:]
    if pool:
        pool_bufs, refs = refs[:3], refs[3:]
    if convert:
        stage_in, stage_out, (sem_in, sem_out) = refs[0:3], refs[3:6], refs[6:8]
    i = pl.program_id(0)

    def slice_copies(step):
        return _convert_slice_copies(step, lax.rem(step, 2), src_hbm, convert, dst_hbm, stage_in, stage_out,
                                     sem_in, sem_out, n_prompt)

    def convert_slice():
        @pl.when(i == 0)
        def _():
            for c in slice_copies(i)[0]:
                c.start()

        @pl.when(i + 1 < n_prompt)
        def _():
            for c in slice_copies(i + 1)[0]:
                c.start()

        ins, outs = slice_copies(i)
        for c in ins:
            c.wait()

        @pl.when(i >= 2)
        def _():
            for c in slice_copies(i - 2)[1]:
                c.wait()

        slot = lax.rem(i, 2)
        for k in range(3):
            stage_out[k][slot] = stage_in[k][slot].astype(BF16)
        for c in outs:
            c.start()

    def pooled(x):
        rows = x.shape[0]
        tile_in_seq = lax.rem(i, pool)
        halo = jnp.where(tile_in_seq == 0, 0.0, _rms(halo_ref[...], g2p_ref[...]))
        out = []
        for r0 in range(0, rows, POOL_BLOCK):
            xb = x[r0:r0 + POOL_BLOCK]
            h = _rms(xb, g2p_ref[...])
            pos = tile_in_seq * rows + r0 + lax.broadcasted_iota(jnp.int32, (POOL_BLOCK, 1), 0)
            out.append(xb + _rms(_pool_mix(h, halo, pos, pw_ref, psc_ref, *pool_bufs), g3p_ref[...]))
            halo = h[POOL_BLOCK - POOL_HALO:]
        tail_ref[0] = h[POOL_BLOCK - POOL_TAIL:]
        return jnp.concatenate(out, axis=0)

    def update(x_ref, o_ref, att_ref=None, cv_ref=None, with_pool=False):
        rows = x_ref.shape[0]
        x = x_ref[...]
        if mix:
            mo = jnp.dot(att_ref[...].astype(BF16), wa_ref[...], preferred_element_type=F32)
            mo = mo + jnp.dot(cv_ref[...].astype(BF16), wc_ref[...], preferred_element_type=F32)
            x = x + _rms(mo, gmix_ref[...])
        if with_pool:
            x = pooled(x)
        h = _rms(x, gpre_ref[...]).astype(BF16)
        d_ff = wg_ref.shape[1]
        for c0 in range(0, d_ff, chunk):
            sl = slice(c0, min(c0 + chunk, d_ff))
            g = jnp.dot(h, wg_ref[:, sl], preferred_element_type=F32)
            u = jnp.dot(h, wu_ref[:, sl], preferred_element_type=F32)
            a_ref[0:rows, sl] = (g * _sigmoid(g) * u).astype(BF16)
        y = jnp.dot(a_ref[0:rows, :], wd_ref[...], preferred_element_type=F32)
        o_ref[...] = x + 0.5 * _rms(y, gpost_ref[...])

    @pl.when(i < n_prompt)
    def _():
        if convert:
            convert_slice()
        update(xp_ref, op_ref, *((attp_ref, cvp_ref) if mix else ()), with_pool=bool(pool))

    @pl.when(i == n_prompt)
    def _():
        if convert:
            for step in (n_prompt - 2, n_prompt - 1):
                for c in slice_copies(step)[1]:
                    c.wait()
        update(xs_ref, os_ref, *((atts_ref, cvs_ref) if mix else ()))


def _ffn(xp, xs, g_pre, g_post, wg, wu, wd, tm, mix=None, pool=None, convert=None):
    m, d = xp.shape
    d_ff = wg.shape[1]
    n_prompt = m // tm
    tile = lambda width: pl.BlockSpec((tm, width), lambda i: (jnp.minimum(i, n_prompt - 1), 0))
    hbm = pl.BlockSpec(memory_space=pl.ANY)
    in_specs = [tile(d), _const_spec(xs.shape), _const_spec((1, d)), _const_spec((1, d)),
                _const_spec(wg.shape, True), _const_spec(wu.shape, True), _const_spec(wd.shape, True)]
    out_specs = [tile(d), _const_spec(xs.shape)]
    out_shape = [jax.ShapeDtypeStruct((m, d), F32), jax.ShapeDtypeStruct(xs.shape, F32)]
    scratch = [pltpu.VMEM((tm, d_ff), BF16)]
    operands = [xp, xs, g_pre, g_post, wg, wu, wd]
    if mix is not None:
        att_p, cv_p, att_s, cv_s, w_att, w_cv, _ = mix
        in_specs += [tile(att_p.shape[1]), tile(cv_p.shape[1]), _const_spec(att_s.shape), _const_spec(cv_s.shape),
                     _const_spec(w_att.shape, True), _const_spec(w_cv.shape, True), _const_spec((1, d))]
        operands += list(mix)
    tiles_per_seq = None
    if pool is not None:
        seq_len, g2p, g3p, pw, psc = pool
        tiles_per_seq = seq_len // tm
        per_halo = tm // POOL_HALO
        in_specs += [pl.BlockSpec((POOL_HALO, d),
                                  lambda i: (jnp.maximum(jnp.minimum(i, n_prompt - 1) * per_halo - 1, 0), 0)),
                     _const_spec((1, d)), _const_spec((1, d)), _const_spec(pw.shape, True), _const_spec((1, d))]
        operands += [xp, g2p, g3p, pw, psc]
        out_specs += [pl.BlockSpec((1, POOL_TAIL, d),
                                   lambda i: (jnp.minimum(i, n_prompt - 1) // tiles_per_seq, 0, 0))]
        out_shape += [jax.ShapeDtypeStruct((m // seq_len, POOL_TAIL, d), F32)]
        scratch += [pltpu.VMEM((POOL_BLOCK + POOL_HALO, d), F32)] * 3
    if convert is not None:
        stacks, which = convert
        in_specs += [hbm] * 3
        operands += list(stacks)
        out_specs += [hbm] * 3
        shapes = [(d, d_ff), (d, d_ff), (d_ff, d)]
        out_shape += [jax.ShapeDtypeStruct(sh, BF16) for sh in shapes]
        scratch += [pltpu.VMEM((2, r // n_prompt, c), F32) for r, c in shapes]
        scratch += [pltpu.VMEM((2, r // n_prompt, c), BF16) for r, c in shapes]
        scratch += [pltpu.SemaphoreType.DMA((2, 3)), pltpu.SemaphoreType.DMA((2, 3))]
    return pl.pallas_call(
        functools.partial(_ffn_kernel, chunk=256, n_prompt=n_prompt, mix=mix is not None, pool=tiles_per_seq,
                          convert=None if convert is None else convert[1]),
        grid=(n_prompt + 1,),
        in_specs=in_specs,
        out_specs=out_specs,
        out_shape=out_shape,
        scratch_shapes=scratch,
        compiler_params=_params(1, FFN_VMEM_LIMIT_BYTES),
        name="ffn",
    )(*operands)


def _ln_swish(y, g, b):
    mu = jnp.mean(y, axis=-1, keepdims=True)
    d = y - mu
    var = jnp.mean(d * d, axis=-1, keepdims=True)
    yn = d * lax.rsqrt(var + LN_EPS) * g + b
    return yn * _sigmoid(yn)


def _causal_conv(buf_ref, rot_ref, w_ref, b_ref, y_ref, *, t0, tm, rows):
    d_conv = buf_ref.shape[1]
    base = CONV_HALO - (CONV_W - 1)
    n_rot = tm + CONV_HALO - 8
    for r in range(1, 8):
        rot_ref[r - 1, t0:t0 + n_rot, :] = buf_ref[t0 + r:t0 + r + n_rot, :]
    for cb in range(d_conv // LANES):
        ls = slice(cb * LANES, (cb + 1) * LANES)
        for rb in range(tm // rows):
            acc = jnp.zeros((rows, LANES), F32)
            for tap in range(CONV_W):
                a, r = divmod(base + tap, 8)
                r0 = t0 + rb * rows + 8 * a
                src = buf_ref[r0:r0 + rows, ls] if r == 0 else rot_ref[r - 1, r0:r0 + rows, ls]
                acc = acc + w_ref[tap:tap + 1, ls] * src
            y_ref[t0 + rb * rows:t0 + (rb + 1) * rows, ls] = acc + b_ref[:, ls]


def _conv_scratch(tm, d_conv):
    return [pltpu.VMEM((tm + CONV_HALO, d_conv), F32), pltpu.VMEM((7, tm + CONV_HALO - 8, d_conv), F32),
            pltpu.VMEM((tm, d_conv), F32)]


def _inproj_kernel(x_ref, g_ref, wqkv_ref, wag_ref, wf_ref, bf_ref, tri_ref, *refs, d_att, d_conv, prompt, sub):
    if prompt:
        (cw_ref, cb_ref, lg_ref, lb_ref,
         q_ref, kt_ref, vt_ref, ktb_ref, vb_ref, lft_ref, cv_ref, utail_ref, ccol_ref, crow_ref,
         carry_ref, buf_ref, rot_ref, y_ref) = refs
    else:
        q_ref, k_ref, v_ref, lft_ref, u_ref, ccol_ref, crow_ref, carry_ref = refs
    tm = x_ref.shape[1]
    j = pl.program_id(1)

    @pl.when(j == 0)
    def _():
        carry_ref[...] = jnp.zeros_like(carry_ref)
        if prompt:
            buf_ref[0:CONV_HALO, :] = jnp.zeros((CONV_HALO, d_conv), F32)

    if prompt:
        @pl.when(j > 0)
        def _():
            buf_ref[0:CONV_HALO, :] = buf_ref[tm:tm + CONV_HALO, :]

    carry = carry_ref[...]
    tri = tri_ref[...]
    for t0 in range(0, tm, sub):
        rs = slice(t0, t0 + sub)
        h = _rms(x_ref[0, rs, :], g_ref[...]).astype(BF16)
        z = jnp.dot(h, wqkv_ref[...], preferred_element_type=F32)
        k = z[:, d_att:2 * d_att]
        v = z[:, 2 * d_att:3 * d_att]
        q_ref[0, rs, :] = (z[:, :d_att] * Q_SCALE).astype(BF16)
        z = jnp.dot(h, wag_ref[...], preferred_element_type=F32)
        u = z[:, :d_conv] * _sigmoid(z[:, d_conv:])
        fg = jnp.dot(h, wf_ref[...], preferred_element_type=F32)
        if prompt:
            kt = k.T
            kt_ref[0, :, rs] = kt
            vt_ref[0, :, rs] = v.T
            ktb_ref[0, :, rs] = kt.astype(BF16)
            vb_ref[0, rs, :] = v.astype(BF16)
            buf_ref[CONV_HALO + t0:CONV_HALO + t0 + sub, :] = u
            _causal_conv(buf_ref, rot_ref, cw_ref, cb_ref, y_ref, t0=t0, tm=sub, rows=min(sub, 128))
            cv_ref[0, rs, :] = _ln_swish(y_ref[rs, :], lg_ref[...], lb_ref[...]).astype(cv_ref.dtype)
        else:
            k_ref[0, rs, :] = k
            v_ref[0, rs, :] = v
            u_ref[0, rs, :] = u
        lf = _log_sigmoid(fg + bf_ref[...])
        lft = lf.T[:N_HEADS]
        lft_ref[0, :, rs] = lft

        chunks = []
        for c in range(sub // LANES):
            cs = _dot3(lft[:, c * LANES:(c + 1) * LANES], tri) + carry
            chunks.append(cs)
            carry = jnp.broadcast_to(cs[:, LANES - 1:LANES], cs.shape)
        crow = jnp.concatenate(chunks, axis=1) * LOG2E
        crow_ref[0, :, rs] = crow
        cfull = jnp.concatenate([crow, jnp.zeros((LANES - N_HEADS, sub), F32)], axis=0)
        ccol_ref[0, rs, :] = cfull.T[:, :N_HEADS]
    carry_ref[...] = carry

    if prompt:
        @pl.when(j == pl.num_programs(1) - 1)
        def _():
            utail_ref[0] = buf_ref[tm:tm + CONV_HALO, :]


def _inproj(x, g, w, bf, tri, conv, tm, d_att, d_conv):
    b, s, d = x.shape
    prompt = conv is not None
    tok = lambda width: pl.BlockSpec((1, tm, width), lambda i, j: (i, j, 0))
    tr = lambda rows: pl.BlockSpec((1, rows, tm), lambda i, j: (i, 0, j))
    shp = lambda width, dt: jax.ShapeDtypeStruct((b, s, width), dt)
    sht = lambda rows, dt: jax.ShapeDtypeStruct((b, rows, s), dt)
    in_specs = [tok(d), _const_spec((1, d))] + [_const_spec(wk.shape, True) for wk in w] + [
        _const_spec((1, LANES)), _const_spec((LANES, LANES))]
    scratch = [pltpu.VMEM((N_HEADS, LANES), F32)]
    if prompt:
        in_specs += [_const_spec(conv[0].shape)] + [_const_spec((1, d_conv))] * 3
        out_specs = [tok(d_att), tr(d_att), tr(d_att), tr(d_att), tok(d_att), tr(N_HEADS), tok(d_conv),
                     pl.BlockSpec((1, CONV_HALO, d_conv), lambda i, j: (i, 0, 0))]
        out_shape = [shp(d_att, BF16), sht(d_att, F32), sht(d_att, F32), sht(d_att, BF16), shp(d_att, BF16),
                     sht(N_HEADS, F32), shp(d_conv, BF16), jax.ShapeDtypeStruct((b, CONV_HALO, d_conv), F32)]
        scratch += _conv_scratch(tm, d_conv)
    else:
        out_specs = [tok(d_att), tok(d_att), tok(d_att), tr(N_HEADS), tok(d_conv)]
        out_shape = [shp(d_att, BF16), shp(d_att, F32), shp(d_att, F32), sht(N_HEADS, F32), shp(d_conv, F32)]
    return pl.pallas_call(
        functools.partial(_inproj_kernel, d_att=d_att, d_conv=d_conv, prompt=prompt, sub=min(tm, 256)),
        grid=(b, s // tm),
        in_specs=in_specs,
        out_specs=out_specs + [tok(N_HEADS), tr(N_HEADS)],
        out_shape=out_shape + [shp(N_HEADS, F32), sht(N_HEADS, F32)],
        scratch_shapes=scratch,
        compiler_params=_params(2),
        name="inproj",
    )(x, g, *w, bf, tri, *(conv or ()))


def _attn_decode_kernel(pt_ref, q_ref, kt_ref, v_ref, ccol_ref, crow_ref,
                        qs_ref, kn_ref, vn_ref, cqcol_ref, cqrow_ref, us_ref, kc_ref, vc_ref, lc_ref,
                        o_ref, os_ref, kbuf, vbuf, lbuf, sem, *, tq, n_group):
    s_len = q_ref.shape[1]
    pair = pl.program_id(1)
    t = pl.program_id(0) * pl.num_programs(1) + pair
    n_samples = pl.num_programs(0) * pl.num_programs(1)
    n_pages = pt_ref.shape[1]
    n_sub = n_pages // n_group
    assert n_sub == 2
    t_new = qs_ref.shape[1]
    d_att = qs_ref.shape[2]
    page = kbuf.shape[-1]
    n_rows = t_new * N_HEADS
    nt_dims = (((1,), (1,)), ((), ()))

    def page_copies(sample, sub, real_pages=True):
        out = []
        for g in range(n_group):
            p = pt_ref[sample, n_pages - 1 - (sub * n_group + g)] if real_pages else 0
            out.append(pltpu.make_async_copy(kc_ref.at[p], kbuf.at[sub, g], sem.at[sub, 0]))
            out.append(pltpu.make_async_copy(vc_ref.at[p], vbuf.at[sub, g], sem.at[sub, 1]))
            out.append(pltpu.make_async_copy(lc_ref.at[p], lbuf.at[sub, g], sem.at[sub, 2]))
        return out

    def start_pages(sample, sub):
        for n, c in enumerate(page_copies(sample, sub)):
            c.start(priority=(n // 3) % 2)

    @pl.when(t == 0)
    def _():
        start_pages(t, 0)

    start_pages(t, 1)

    cqcol = cqcol_ref[0]
    st = {}

    def decode_half(sub):
        if sub == 0:
            q = qs_ref[0].astype(F32)
            hsub = lax.broadcasted_iota(jnp.int32, (N_HEADS, d_att), 0)
            hlane = lax.broadcasted_iota(jnp.int32, (N_HEADS, d_att), 1)
            own = (hlane // HEAD_DIM) == hsub
            rows = [jnp.where(own, jnp.broadcast_to(q[i:i + 1, :], (N_HEADS, d_att)), 0.0) for i in range(t_new)]
            st["qbd"] = jnp.concatenate(rows, axis=0).astype(BF16)
            s = lax.dot_general(st["qbd"], kn_ref[0].astype(BF16), nt_dims, preferred_element_type=F32)
            cqk = jnp.concatenate([cqrow_ref[0]] * t_new, axis=0)
            r = lax.broadcasted_iota(jnp.int32, s.shape, 0)
            c = lax.broadcasted_iota(jnp.int32, s.shape, 1)
            s = jnp.where(c <= r // N_HEADS, s + (cqcol - cqk), NEG_INF)
            st["m"] = jnp.max(s, axis=-1, keepdims=True)
            pr = jnp.exp2(s - st["m"])
            st["l"] = jnp.sum(pr, axis=-1, keepdims=True)
            st["acc"] = jnp.dot(pr.astype(BF16), vn_ref[0].astype(BF16), preferred_element_type=F32)
            st["carry"] = jnp.zeros((N_HEADS, page), F32)

        for c in page_copies(t, sub, real_pages=False):
            c.wait()

        qbd, carry = st["qbd"], st["carry"]
        us = us_ref[...]
        s_parts = []
        r_all = _dot3(lbuf[sub].reshape(n_group * N_HEADS, page), us)
        for g in range(n_group):
            r = r_all[g * N_HEADS:(g + 1) * N_HEADS]
            suffix = (r[:, :page] + carry) * LOG2E
            carry = carry + r[:, page:]
            bias = jnp.concatenate([suffix] * t_new, axis=0) + cqcol
            kt = kbuf[sub, g].reshape(d_att, page).astype(BF16)
            s_parts.append(jnp.dot(qbd, kt, preferred_element_type=F32) + bias)
        s = jnp.concatenate(s_parts, axis=1)
        m_new = jnp.maximum(st["m"], jnp.max(s, axis=-1, keepdims=True))
        alpha = jnp.exp2(st["m"] - m_new)
        pr = jnp.exp2(s - m_new)
        prb = pr.astype(BF16)
        acc = alpha * st["acc"]
        for g in range(n_group):
            vt = vbuf[sub, g].reshape(d_att, page).astype(BF16)
            acc = acc + lax.dot_general(prb[:, g * page:(g + 1) * page], vt, nt_dims, preferred_element_type=F32)
        st.update(m=m_new, l=alpha * st["l"] + jnp.sum(pr, axis=-1, keepdims=True), acc=acc, carry=carry)

        if sub == 0:
            @pl.when(t + 1 < n_samples)
            def _():
                start_pages(t + 1, 0)

        if sub == n_sub - 1:
            o = acc * (1.0 / st["l"])
            hsub = lax.broadcasted_iota(jnp.int32, (N_HEADS, d_att), 0)
            hlane = lax.broadcasted_iota(jnp.int32, (N_HEADS, d_att), 1)
            own = (hlane // HEAD_DIM) == hsub
            rows = [jnp.sum(jnp.where(own, o[i * N_HEADS:(i + 1) * N_HEADS], 0.0), axis=0, keepdims=True)
                    for i in range(t_new)]
            os_ref[0] = jnp.concatenate(rows, axis=0).astype(os_ref.dtype)

    row = lax.broadcasted_iota(jnp.int32, (tq, tq), 0)
    col = lax.broadcasted_iota(jnp.int32, (tq, tq), 1)
    causal = col <= row
    low = lax.broadcasted_iota(jnp.int32, (tq, LANES), 1) < HEAD_DIM
    ccol = ccol_ref[0]
    head_lane = lax.broadcasted_iota(jnp.int32, ccol.shape, 1)
    cq_all = []
    sub16 = lax.broadcasted_iota(jnp.int32, (2 * LANES // N_HEADS, s_len), 0)
    extra = jnp.zeros(sub16.shape, F32)
    for hh in range(2):
        head = 2 * pair + hh
        cq_all.append(jnp.sum(jnp.where(head_lane == head, ccol, 0.0), axis=-1, keepdims=True))
        for n, piece in enumerate(_split3(-crow_ref[0, pl.ds(head, 1), :])):
            extra = jnp.where(sub16 == 3 * hh + n, piece, extra)
    kt_bias = jnp.concatenate([kt_ref[0], extra.astype(BF16),
                               jnp.zeros((LANES - extra.shape[0], s_len), BF16)], axis=0)
    lane = lax.broadcasted_iota(jnp.int32, (tq, LANES), 1)

    def scores(i, hh):
        k0 = i * tq
        q2 = q_ref[0, k0:k0 + tq, :]
        zero = jnp.zeros_like(q2)
        qh = jnp.where(low, q2, zero) if hh == 0 else jnp.where(low, zero, q2)
        ones = jnp.where((lane >= 3 * hh) & (lane < 3 * hh + 3), 1.0, 0.0).astype(BF16)
        qh = jnp.concatenate([qh, ones], axis=1)
        s_diag = jnp.dot(qh, kt_bias[:, k0:k0 + tq], preferred_element_type=F32)
        s_off = jnp.dot(qh, kt_bias[:, :k0], preferred_element_type=F32) if i > 0 else None
        return s_diag, s_off

    v_ones = jnp.concatenate([v_ref[0], jnp.ones((s_len, LANES), BF16)], axis=1)

    def finish(i, hh, s_diag, s_off):
        k0 = i * tq
        cq = cq_all[hh][k0:k0 + tq]
        s_diag = jnp.where(causal, s_diag, NEG_INF)
        m = jnp.max(s_diag, axis=-1, keepdims=True)
        if i > 0:
            m = jnp.maximum(m, jnp.max(s_off, axis=-1, keepdims=True))
        shift = cq - (m + cq)
        pv = jnp.dot(jnp.exp2(s_diag + shift).astype(BF16), v_ones[k0:k0 + tq], preferred_element_type=F32)
        if i > 0:
            pv = pv + jnp.dot(jnp.exp2(s_off + shift).astype(BF16), v_ones[:k0], preferred_element_type=F32)
        return pv[:, :LANES] * (1.0 / pv[:, LANES:LANES + 1])

    blocks = [(i, hh) for i in range(s_len // tq) for hh in range(2)]
    work = [i + 1 for i, _ in blocks]
    first_half_at = next(n for n in range(len(blocks)) if 2 * sum(work[:n + 1]) >= sum(work))
    nxt = scores(*blocks[0])
    halves = []
    for n, (i, hh) in enumerate(blocks):
        cur = nxt
        if n + 1 < len(blocks):
            nxt = scores(*blocks[n + 1])
        halves.append(finish(i, hh, *cur))
        if hh == 1:
            o_ref[0, i * tq:(i + 1) * tq, :] = jnp.where(low, halves[0], halves[1]).astype(BF16)
            halves = []
        if n == first_half_at:
            decode_half(0)
    decode_half(1)


def _attn_decode(q, ktb, vb, ccol, crow, tq,
                 page_table, qs, kn, vn, cqcol, cqrow, us, cache_k, cache_v, cache_lf, n_group):
    b, s, d_att = q.shape
    bd, t_new, _ = qs.shape
    n_pairs = d_att // LANES
    assert bd == b * n_pairs
    page = cache_k.shape[3]
    n_rows = t_new * N_HEADS
    cols = pl.BlockSpec((1, s, LANES), lambda i, j, pt: (i, 0, j))
    per_s = lambda shape: pl.BlockSpec((1,) + shape, lambda i, j, pt: (i * n_pairs + j,) + (0,) * len(shape))
    hbm = pl.BlockSpec(memory_space=pl.ANY)
    grid_spec = pltpu.PrefetchScalarGridSpec(
        num_scalar_prefetch=1,
        grid=(b, n_pairs),
        in_specs=[cols, pl.BlockSpec((1, LANES, s), lambda i, j, pt: (i, j, 0)), cols,
                  pl.BlockSpec((1, s, N_HEADS), lambda i, j, pt: (i, 0, 0)),
                  pl.BlockSpec((1, N_HEADS, s), lambda i, j, pt: (i, 0, 0)),
                  per_s((t_new, d_att)), per_s(kn.shape[1:]), per_s(vn.shape[1:]), per_s((n_rows, 1)),
                  per_s(cqrow.shape[1:]), pl.BlockSpec(us.shape, lambda i, j, pt: (0, 0)), hbm, hbm, hbm],
        out_specs=[cols, per_s((t_new, d_att))],
        scratch_shapes=[pltpu.VMEM((2, n_group, N_HEADS, HEAD_DIM, page), F32),
                        pltpu.VMEM((2, n_group, N_HEADS, HEAD_DIM, page), F32),
                        pltpu.VMEM((2, n_group, N_HEADS, page), F32),
                        pltpu.SemaphoreType.DMA((2, 3))],
    )
    return pl.pallas_call(
        functools.partial(_attn_decode_kernel, tq=tq, n_group=n_group),
        grid_spec=grid_spec,
        out_shape=[jax.ShapeDtypeStruct((b, s, d_att), BF16), jax.ShapeDtypeStruct((bd, t_new, d_att), BF16)],
        compiler_params=_params(2),
        name="attn_decode",
    )(page_table, q, ktb, vb, ccol, crow, qs, kn, vn, cqcol, cqrow, us, cache_k, cache_v, cache_lf)


def _conv_kernel(u_ref, state_ref, w_ref, b_ref, lg_ref, lb_ref, o_ref, buf_ref, rot_ref, y_ref):
    tm = u_ref.shape[1]
    for n in range(u_ref.shape[0]):
        buf_ref[0:CONV_HALO, :] = state_ref[n]
        buf_ref[CONV_HALO:, :] = u_ref[n]
        _causal_conv(buf_ref, rot_ref, w_ref, b_ref, y_ref, t0=0, tm=tm, rows=tm)
        o_ref[n] = _ln_swish(y_ref[...], lg_ref[...], lb_ref[...])


def _conv(u, state, w, b, lg, lb, per_step):
    bsz, tm, d_conv = u.shape
    assert state.shape == (bsz, CONV_HALO, d_conv)
    tile = pl.BlockSpec((per_step, tm, d_conv), lambda i: (i, 0, 0))
    vec = _const_spec((1, d_conv))
    return pl.pallas_call(
        _conv_kernel,
        grid=(bsz // per_step,),
        in_specs=[tile, pl.BlockSpec((per_step, CONV_HALO, d_conv), lambda i: (i, 0, 0)), _const_spec(w.shape),
                  vec, vec, vec],
        out_specs=tile,
        out_shape=jax.ShapeDtypeStruct((bsz, tm, d_conv), F32),
        scratch_shapes=_conv_scratch(tm, d_conv),
        compiler_params=_params(1),
        name="conv",
    )(u, state, w, b, lg, lb)


def _pool_kernel(x_ref, state_ref, g2_ref, g3_ref, w_ref, sc_ref, o_ref, h_ref, buf_ref, ta_ref, tb_ref, *, pos0):
    tm = x_ref.shape[1]
    pos = pos0 + lax.broadcasted_iota(jnp.int32, (tm, 1), 0)
    for bi in range(x_ref.shape[0]):
        x = x_ref[bi]
        h = _rms(x, g2_ref[...])
        h_ref[bi] = h
        mix = _pool_mix(h, state_ref[bi], pos, w_ref, sc_ref, buf_ref, ta_ref, tb_ref)
        o_ref[bi] = x + _rms(mix, g3_ref[...])


def _pool(x, state, g2, g3, w, scale, pos0, per_step):
    b, tm, d = x.shape
    assert state.shape == (b, POOL_HALO, d)
    tile = pl.BlockSpec((per_step, tm, d), lambda i: (i, 0, 0))
    vec = _const_spec((1, d))
    buf = pltpu.VMEM((tm + POOL_HALO, d), F32)
    return pl.pallas_call(
        functools.partial(_pool_kernel, pos0=pos0),
        grid=(b // per_step,),
        in_specs=[tile, pl.BlockSpec((per_step, POOL_HALO, d), lambda i: (i, 0, 0)), vec, vec,
                  _const_spec(w.shape), vec],
        out_specs=[tile, tile],
        out_shape=[jax.ShapeDtypeStruct((b, tm, d), F32), jax.ShapeDtypeStruct((b, tm, d), F32)],
        scratch_shapes=[buf, buf, buf],
        compiler_params=_params(1),
        name="pool",
    )(x, state, g2, g3, w, scale)


def kernel(x_prompt, x_sample, cache_k, cache_v, cache_logf, state_conv, state_pool, page_table,
           norm_g, ffn_w_gate, ffn_w_up, ffn_w_down, mix_w_in, fgate_b, conv_dw_w, conv_dw_b,
           conv_ln_g, conv_ln_b, mix_w_out, pool_w, pool_scale):
    b, s, d = x_prompt.shape
    bd, t_new, _ = x_sample.shape
    depth = norm_g.shape[0]
    d_att = N_HEADS * HEAD_DIM
    d_conv = d - d_att
    page = cache_k.shape[2]
    past_len = page_table.shape[1] * page
    tm_prompt = 512
    m_sample = bd * t_new
    t_pad = 8

    xp = x_prompt.reshape(b * s, d)
    xs = x_sample.reshape(m_sample, d)
    vec = lambda a: a.reshape(1, -1).astype(F32)

    stacks = (ffn_w_gate.astype(F32), ffn_w_up.astype(F32), ffn_w_down.astype(F32))
    halves = [(layer, half) for layer in range(depth) for half in range(2)]
    ffn_weights = [tuple(w[0, 0].astype(BF16) for w in stacks)]

    def ffn_both(xp, xs, layer, half, mix=None, pool=None):
        gp, gq = vec(norm_g[layer, 4 * half]), vec(norm_g[layer, 4 * half + 1])
        n = halves.index((layer, half))
        convert = (stacks, halves[n + 1]) if n + 1 < len(halves) else None
        xp, xs, *rest = _ffn(xp, xs, gp, gq, *ffn_weights[n], 1024, mix, pool, convert)
        tail = rest.pop(0) if pool is not None else None
        ffn_weights.append(tuple(rest))
        return xp, xs, tail

    r = jnp.arange(LANES)
    tri_incl = (r[:, None] <= r[None, :]).astype(F32)
    tri_group = tri_incl * (r[:, None] // t_new == r[None, :] // t_new).astype(F32)
    rp = jnp.arange(page)
    suffix_total = jnp.concatenate([(rp[:, None] > rp[None, :]).astype(F32), jnp.ones((page, page), F32)], axis=1)

    outs_p = {k: [] for k in ("k", "v", "lf", "conv", "pool")}
    outs_s = {k: [] for k in ("k", "v", "lf", "conv", "pool")}
    for layer in range(depth):
        g = norm_g[layer]
        xp, xs, _ = ffn_both(xp, xs, layer, 0)
        mix = pool = None
        if layer % 2 == 0:
            e = layer // 2
            w_in = mix_w_in[e]
            o = 3 * d_att + N_HEADS
            w_cat = (w_in[:, :3 * d_att].astype(BF16), w_in[:, o:].astype(BF16),
                     jnp.pad(w_in[:, 3 * d_att:o], ((0, 0), (0, LANES - N_HEADS))).astype(BF16))
            bf = jnp.concatenate([fgate_b[e].astype(F32), jnp.zeros((LANES - N_HEADS,), F32)]).reshape(1, LANES)
            wa = mix_w_out[e, :d_att].astype(BF16)
            wc = mix_w_out[e, d_att:].astype(BF16)
            cw, cb = conv_dw_w[e].astype(F32), vec(conv_dw_b[e])
            lg, lb = vec(conv_ln_g[e]), vec(conv_ln_b[e])
            g2, g3 = vec(g[2]), vec(g[3])

            q, kt, vt, ktb, vb, lft, cv, u_tail, ccol, crow = _inproj(
                xp.reshape(b, s, d), g2, w_cat, bf, tri_incl, (cw, cb, lg, lb), tm_prompt, d_att, d_conv)
            prompt_att = (q, ktb, vb, ccol, crow)
            cv_p = cv.reshape(b * s, d_conv)
            outs_p["k"].append(kt.reshape(b, N_HEADS, HEAD_DIM, s).transpose(0, 3, 1, 2))
            outs_p["v"].append(vt.reshape(b, N_HEADS, HEAD_DIM, s).transpose(0, 3, 1, 2))
            outs_p["lf"].append(lft.transpose(0, 2, 1))
            outs_p["conv"].append(u_tail[:, CONV_HALO - (CONV_W - 1):])

            q, k, v, lft, u, ccol, crow = _inproj(
                xs.reshape(1, m_sample, d), g2, w_cat, bf, tri_group, None, m_sample, d_att, d_conv)
            lf = lft.reshape(N_HEADS, m_sample).T
            pad_rows = lambda a: jnp.pad(a.reshape(bd, t_new, d_att), ((0, 0), (0, 16 - t_new), (0, 0)))
            cqcol = ccol.reshape(bd, t_new * N_HEADS, 1)
            cqrow = jnp.pad(crow.reshape(N_HEADS, bd, t_new).transpose(1, 0, 2), ((0, 0), (0, 0), (0, 16 - t_new)))
            att_p, att = _attn_decode(
                *prompt_att, 256, page_table, q.reshape(bd, t_new, d_att), pad_rows(k), pad_rows(v), cqcol, cqrow,
                suffix_total, cache_k[e].transpose(0, 2, 3, 1), cache_v[e].transpose(0, 2, 3, 1),
                cache_logf[e].astype(F32).transpose(0, 2, 1), 32)
            att_p = att_p.reshape(b * s, d_att)
            st = state_conv[e].astype(F32)
            cv = _conv(jnp.pad(u.reshape(bd, t_new, d_conv), ((0, 0), (0, t_pad - t_new), (0, 0))),
                       jnp.pad(st, ((0, 0), (CONV_HALO - (CONV_W - 1), 0), (0, 0))), cw, cb, lg, lb, 8)
            mix = (att_p, cv_p, att.reshape(m_sample, d_att), cv[:, :t_new].reshape(m_sample, d_conv), wa, wc, g3)
            outs_s["k"].append(k.reshape(bd, t_new, N_HEADS, HEAD_DIM))
            outs_s["v"].append(v.reshape(bd, t_new, N_HEADS, HEAD_DIM))
            outs_s["lf"].append(lf.reshape(bd, t_new, N_HEADS))
            outs_s["conv"].append(jnp.concatenate([st, u.reshape(bd, t_new, d_conv)], axis=1)[:, t_new:])
        else:
            o = layer // 2
            g2, g3 = vec(g[2]), vec(g[3])
            pw = pool_w[o].astype(BF16)
            sc = vec(pool_scale[o])
            n_keep = POOL_WINDOWS[-1] - 1
            pool = (s, g2, g3, pw, sc)
            st = state_pool[o].astype(F32)
            xs3, hs = _pool(jnp.pad(xs.reshape(bd, t_new, d), ((0, 0), (0, t_pad - t_new), (0, 0))),
                            jnp.pad(st, ((0, 0), (POOL_HALO - n_keep, 0), (0, 0))), g2, g3, pw, sc, past_len, 8)
            xs = xs3[:, :t_new].reshape(m_sample, d)
            outs_s["pool"].append(jnp.concatenate([st, hs[:, :t_new]], axis=1)[:, t_new:])
        xp, xs, tail = ffn_both(xp, xs, layer, 1, mix, pool)
        if pool is not None:
            outs_p["pool"].append(tail[:, POOL_TAIL - (POOL_WINDOWS[-1] - 1):])

    stack = lambda l: jnp.stack(l)
    return (xp.reshape(b, s, d), xs.reshape(bd, t_new, d),
            stack(outs_p["k"]), stack(outs_p["v"]), stack(outs_p["lf"]), stack(outs_p["conv"]),
            stack(outs_p["pool"]),
            stack(outs_s["k"]), stack(outs_s["v"]), stack(outs_s["lf"]), stack(outs_s["conv"]),
            stack(outs_s["pool"]))
```

```python
import functools

import jax
import jax.numpy as jnp
from jax import lax
from jax.experimental import pallas as pl
from jax.experimental.pallas import tpu as pltpu

F32 = jnp.float32
BF16 = jnp.bfloat16

RMS_EPS = 1e-6
LN_EPS = 1e-5
NEG_INF = -1e30
LOG2E = 1.4426950408889634

LANES = 128
HEAD_DIM = 64
N_HEADS = 8
Q_SCALE = HEAD_DIM ** -0.5 * LOG2E
CONV_W = 31
CONV_HALO = 32
POOL_WINDOWS = (2, 4, 8, 16)
POOL_HALO = 32
POOL_TAIL = 16
POOL_BLOCK = 256
VMEM_LIMIT_BYTES = 56 * 1024 * 1024
FFN_VMEM_LIMIT_BYTES = 61 * 1024 * 1024


def _params(n_axes, vmem_limit_bytes=VMEM_LIMIT_BYTES):
    return pltpu.CompilerParams(dimension_semantics=("arbitrary",) * n_axes,
                                vmem_limit_bytes=vmem_limit_bytes)


def _const_spec(shape, single_buffer=False):
    zeros = (0,) * len(shape)
    if single_buffer:
        return pl.BlockSpec(shape, lambda *_: zeros, pipeline_mode=pl.Buffered(1))
    return pl.BlockSpec(shape, lambda *_: zeros)


def _rms(x, g):
    ms = jnp.mean(x * x, axis=-1, keepdims=True)
    return x * lax.rsqrt(ms + RMS_EPS) * g


def _sigmoid(x):
    return 1.0 / (1.0 + jnp.exp(-x))


def _log_sigmoid(x):
    return jnp.minimum(x, 0.0) - jnp.log1p(jnp.exp(-jnp.abs(x)))


def _split3(x):
    hi = x.astype(BF16).astype(F32)
    r = x - hi
    mid = r.astype(BF16).astype(F32)
    return hi, mid, r - mid


def _dot3(x, w):
    hi, mid, lo = _split3(x)
    d = lambda a: jnp.dot(a, w, preferred_element_type=F32)
    return d(hi) + d(mid) + d(lo)


def _pool_mix(h, halo, pos, w_ref, sc_ref, buf_ref, ta_ref, tb_ref):
    rows, d = h.shape
    grp = d // len(POOL_WINDOWS)
    n = rows + POOL_HALO
    buf_ref[0:POOL_HALO, :] = halo
    buf_ref[POOL_HALO:n, :] = h
    ta_ref[8:n, :] = buf_ref[8:n, :] + buf_ref[7:n - 1, :]
    tb_ref[16:n, grp:] = ta_ref[16:n, grp:] + ta_ref[14:n - 2, grp:]
    ta_ref[24:n, 2 * grp:] = tb_ref[24:n, 2 * grp:] + tb_ref[20:n - 4, 2 * grp:]
    s16 = ta_ref[32:n, 3 * grp:] + ta_ref[24:n - 8, 3 * grp:]
    totals = (ta_ref[POOL_HALO:n, 0:grp], tb_ref[POOL_HALO:n, grp:2 * grp],
              ta_ref[POOL_HALO:n, 2 * grp:3 * grp], s16)
    outs = []
    for gi, win in enumerate(POOL_WINDOWS):
        cnt = jnp.minimum(pos + 1, win).astype(F32)
        mixed = totals[gi] / cnt - h[:, gi * grp:(gi + 1) * grp]
        outs.append(jnp.dot(mixed.astype(BF16), w_ref[gi], preferred_element_type=F32))
    return jnp.concatenate(outs, axis=-1) * sc_ref[...]


def _convert_slice_copies(i, slot, src_hbm, which, dst_hbm, stage_in, stage_out, sem_in, sem_out, n_slices):
    ins, outs = [], []
    for k in range(3):
        rows = dst_hbm[k].shape[0] // n_slices
        ins.append(pltpu.make_async_copy(src_hbm[k].at[which[0], which[1], pl.ds(i * rows, rows)],
                                         stage_in[k].at[slot], sem_in.at[slot, k]))
        outs.append(pltpu.make_async_copy(stage_out[k].at[slot], dst_hbm[k].at[pl.ds(i * rows, rows)],
                                          sem_out.at[slot, k]))
    return ins, outs


def _ffn_kernel(*refs, chunk, n_prompt, mix, pool, convert):
    xp_ref, xs_ref, gpre_ref, gpost_ref, wg_ref, wu_ref, wd_ref = refs[:7]
    refs = refs[7:]
    if mix:
        attp_ref, cvp_ref, atts_ref, cvs_ref, wa_ref, wc_ref, gmix_ref = refs[:7]
        refs = refs[7:]
    if pool:
        halo_ref, g2p_ref, g3p_ref, pw_ref, psc_ref = refs[:5]
        refs = refs[5:]
    if convert:
        src_hbm, refs = refs[:3], refs[3:]
    op_ref, os_ref = refs[:2]
    refs = refs[2:]
    if pool:
        tail_ref, refs = refs[0], refs[1:]
    if convert:
        dst_hbm, refs = refs[:3], refs[3:]
    a_ref, refs = refs[0], refs[1:]
    if pool:
        pool_bufs, refs = refs[:3], refs[3:]
    if convert:
        stage_in, stage_out, (sem_in, sem_out) = refs[0:3], refs[3:6], refs[6:8]
    i = pl.program_id(0)

    def slice_copies(step):
        return _convert_slice_copies(step, lax.rem(step, 2), src_hbm, convert, dst_hbm, stage_in, stage_out,
                                     sem_in, sem_out, n_prompt)

    def convert_slice():
        @pl.when(i == 0)
        def _():
            for c in slice_copies(i)[0]:
                c.start()

        @pl.when(i + 1 < n_prompt)
        def _():
            for c in slice_copies(i + 1)[0]:
                c.start()

        ins, outs = slice_copies(i)
        for c in ins:
            c.wait()

        @pl.when(i >= 2)
        def _():
            for c in slice_copies(i - 2)[1]:
                c.wait()

        slot = lax.rem(i, 2)
        for k in range(3):
            stage_out[k][slot] = stage_in[k][slot].astype(BF16)
        for c in outs:
            c.start()

    def pooled(x):
        rows = x.shape[0]
        tile_in_seq = lax.rem(i, pool)
        halo = jnp.where(tile_in_seq == 0, 0.0, _rms(halo_ref[...], g2p_ref[...]))
        out = []
        for r0 in range(0, rows, POOL_BLOCK):
            xb = x[r0:r0 + POOL_BLOCK]
            h = _rms(xb, g2p_ref[...])
            pos = tile_in_seq * rows + r0 + lax.broadcasted_iota(jnp.int32, (POOL_BLOCK, 1), 0)
            out.append(xb + _rms(_pool_mix(h, halo, pos, pw_ref, psc_ref, *pool_bufs), g3p_ref[...]))
            halo = h[POOL_BLOCK - POOL_HALO:]
        tail_ref[0] = h[POOL_BLOCK - POOL_TAIL:]
        return jnp.concatenate(out, axis=0)

    def update(x_ref, o_ref, att_ref=None, cv_ref=None, with_pool=False):
        rows = x_ref.shape[0]
        x = x_ref[...]
        if mix:
            mo = jnp.dot(att_ref[...].astype(BF16), wa_ref[...], preferred_element_type=F32)
            mo = mo + jnp.dot(cv_ref[...].astype(BF16), wc_ref[...], preferred_element_type=F32)
            x = x + _rms(mo, gmix_ref[...])
        if with_pool:
            x = pooled(x)
        h = _rms(x, gpre_ref[...]).astype(BF16)
        d_ff = wg_ref.shape[1]
        for c0 in range(0, d_ff, chunk):
            sl = slice(c0, min(c0 + chunk, d_ff))
            g = jnp.dot(h, wg_ref[:, sl], preferred_element_type=F32)
            u = jnp.dot(h, wu_ref[:, sl], preferred_element_type=F32)
            a_ref[0:rows, sl] = (g * _sigmoid(g) * u).astype(BF16)
        y = jnp.dot(a_ref[0:rows, :], wd_ref[...], preferred_element_type=F32)
        o_ref[...] = x + 0.5 * _rms(y, gpost_ref[...])

    @pl.when(i < n_prompt)
    def _():
        if convert:
            convert_slice()
        update(xp_ref, op_ref, *((attp_ref, cvp_ref) if mix else ()), with_pool=bool(pool))

    @pl.when(i == n_prompt)
    def _():
        if convert:
            for step in (n_prompt - 2, n_prompt - 1):
                for c in slice_copies(step)[1]:
                    c.wait()
        update(xs_ref, os_ref, *((atts_ref, cvs_ref) if mix else ()))


def _ffn(xp, xs, g_pre, g_post, wg, wu, wd, tm, mix=None, pool=None, convert=None):
    m, d = xp.shape
    d_ff = wg.shape[1]
    n_prompt = m // tm
    tile = lambda width: pl.BlockSpec((tm, width), lambda i: (jnp.minimum(i, n_prompt - 1), 0))
    hbm = pl.BlockSpec(memory_space=pl.ANY)
    in_specs = [tile(d), _const_spec(xs.shape), _const_spec((1, d)), _const_spec((1, d)),
                _const_spec(wg.shape, True), _const_spec(wu.shape, True), _const_spec(wd.shape, True)]
    out_specs = [tile(d), _const_spec(xs.shape)]
    out_shape = [jax.ShapeDtypeStruct((m, d), F32), jax.ShapeDtypeStruct(xs.shape, F32)]
    scratch = [pltpu.VMEM((tm, d_ff), BF16)]
    operands = [xp, xs, g_pre, g_post, wg, wu, wd]
    if mix is not None:
        att_p, cv_p, att_s, cv_s, w_att, w_cv, _ = mix
        in_specs += [tile(att_p.shape[1]), tile(cv_p.shape[1]), _const_spec(att_s.shape), _const_spec(cv_s.shape),
                     _const_spec(w_att.shape, True), _const_spec(w_cv.shape, True), _const_spec((1, d))]
        operands += list(mix)
    tiles_per_seq = None
    if pool is not None:
        seq_len, g2p, g3p, pw, psc = pool
        tiles_per_seq = seq_len // tm
        per_halo = tm // POOL_HALO
        in_specs += [pl.BlockSpec((POOL_HALO, d),
                                  lambda i: (jnp.maximum(jnp.minimum(i, n_prompt - 1) * per_halo - 1, 0), 0)),
                     _const_spec((1, d)), _const_spec((1, d)), _const_spec(pw.shape, True), _const_spec((1, d))]
        operands += [xp, g2p, g3p, pw, psc]
        out_specs += [pl.BlockSpec((1, POOL_TAIL, d),
                                   lambda i: (jnp.minimum(i, n_prompt - 1) // tiles_per_seq, 0, 0))]
        out_shape += [jax.ShapeDtypeStruct((m // seq_len, POOL_TAIL, d), F32)]
        scratch += [pltpu.VMEM((POOL_BLOCK + POOL_HALO, d), F32)] * 3
    if convert is not None:
        stacks, which = convert
        in_specs += [hbm] * 3
        operands += list(stacks)
        out_specs += [hbm] * 3
        shapes = [(d, d_ff), (d, d_ff), (d_ff, d)]
        out_shape += [jax.ShapeDtypeStruct(sh, BF16) for sh in shapes]
        scratch += [pltpu.VMEM((2, r // n_prompt, c), F32) for r, c in shapes]
        scratch += [pltpu.VMEM((2, r // n_prompt, c), BF16) for r, c in shapes]
        scratch += [pltpu.SemaphoreType.DMA((2, 3)), pltpu.SemaphoreType.DMA((2, 3))]
    return pl.pallas_call(
        functools.partial(_ffn_kernel, chunk=256, n_prompt=n_prompt, mix=mix is not None, pool=tiles_per_seq,
                          convert=None if convert is None else convert[1]),
        grid=(n_prompt + 1,),
        in_specs=in_specs,
        out_specs=out_specs,
        out_shape=out_shape,
        scratch_shapes=scratch,
        compiler_params=_params(1, FFN_VMEM_LIMIT_BYTES),
        name="ffn",
    )(*operands)


def _ln_swish(y, g, b):
    mu = jnp.mean(y, axis=-1, keepdims=True)
    d = y - mu
    var = jnp.mean(d * d, axis=-1, keepdims=True)
    yn = d * lax.rsqrt(var + LN_EPS) * g + b
    return yn * _sigmoid(yn)


def _causal_conv(buf_ref, rot_ref, w_ref, b_ref, y_ref, *, t0, tm, rows):
    d_conv = buf_ref.shape[1]
    base = CONV_HALO - (CONV_W - 1)
    n_rot = tm + CONV_HALO - 8
    for r in range(1, 8):
        rot_ref[r - 1, t0:t0 + n_rot, :] = buf_ref[t0 + r:t0 + r + n_rot, :]
    for cb in range(d_conv // LANES):
        ls = slice(cb * LANES, (cb + 1) * LANES)
        for rb in range(tm // rows):
            acc = jnp.zeros((rows, LANES), F32)
            for tap in range(CONV_W):
                a, r = divmod(base + tap, 8)
                r0 = t0 + rb * rows + 8 * a
                src = buf_ref[r0:r0 + rows, ls] if r == 0 else rot_ref[r - 1, r0:r0 + rows, ls]
                acc = acc + w_ref[tap:tap + 1, ls] * src
            y_ref[t0 + rb * rows:t0 + (rb + 1) * rows, ls] = acc + b_ref[:, ls]


def _conv_scratch(tm, d_conv):
    return [pltpu.VMEM((tm + CONV_HALO, d_conv), F32), pltpu.VMEM((7, tm + CONV_HALO - 8, d_conv), F32),
            pltpu.VMEM((tm, d_conv), F32)]


def _inproj_kernel(x_ref, g_ref, wqkv_ref, wag_ref, wf_ref, bf_ref, tri_ref, *refs, d_att, d_conv, prompt, sub):
    if prompt:
        (cw_ref, cb_ref, lg_ref, lb_ref,
         q_ref, kt_ref, vt_ref, ktb_ref, vb_ref, lft_ref, cv_ref, utail_ref, ccol_ref, crow_ref,
         carry_ref, buf_ref, rot_ref, y_ref) = refs
    else:
        q_ref, k_ref, v_ref, lft_ref, u_ref, ccol_ref, crow_ref, carry_ref = refs
    tm = x_ref.shape[1]
    j = pl.program_id(1)

    @pl.when(j == 0)
    def _():
        carry_ref[...] = jnp.zeros_like(carry_ref)
        if prompt:
            buf_ref[0:CONV_HALO, :] = jnp.zeros((CONV_HALO, d_conv), F32)

    if prompt:
        @pl.when(j > 0)
        def _():
            buf_ref[0:CONV_HALO, :] = buf_ref[tm:tm + CONV_HALO, :]

    carry = carry_ref[...]
    tri = tri_ref[...]
    for t0 in range(0, tm, sub):
        rs = slice(t0, t0 + sub)
        h = _rms(x_ref[0, rs, :], g_ref[...]).astype(BF16)
        z = jnp.dot(h, wqkv_ref[...], preferred_element_type=F32)
        k = z[:, d_att:2 * d_att]
        v = z[:, 2 * d_att:3 * d_att]
        q_ref[0, rs, :] = (z[:, :d_att] * Q_SCALE).astype(BF16)
        z = jnp.dot(h, wag_ref[...], preferred_element_type=F32)
        u = z[:, :d_conv] * _sigmoid(z[:, d_conv:])
        fg = jnp.dot(h, wf_ref[...], preferred_element_type=F32)
        if prompt:
            kt = k.T
            kt_ref[0, :, rs] = kt
            vt_ref[0, :, rs] = v.T
            ktb_ref[0, :, rs] = kt.astype(BF16)
            vb_ref[0, rs, :] = v.astype(BF16)
            buf_ref[CONV_HALO + t0:CONV_HALO + t0 + sub, :] = u
            _causal_conv(buf_ref, rot_ref, cw_ref, cb_ref, y_ref, t0=t0, tm=sub, rows=min(sub, 128))
            cv_ref[0, rs, :] = _ln_swish(y_ref[rs, :], lg_ref[...], lb_ref[...]).astype(cv_ref.dtype)
        else:
            k_ref[0, rs, :] = k
            v_ref[0, rs, :] = v
            u_ref[0, rs, :] = u
        lf = _log_sigmoid(fg + bf_ref[...])
        lft = lf.T[:N_HEADS]
        lft_ref[0, :, rs] = lft

        chunks = []
        for c in range(sub // LANES):
            cs = _dot3(lft[:, c * LANES:(c + 1) * LANES], tri) + carry
            chunks.append(cs)
            carry = jnp.broadcast_to(cs[:, LANES - 1:LANES], cs.shape)
        crow = jnp.concatenate(chunks, axis=1) * LOG2E
        crow_ref[0, :, rs] = crow
        cfull = jnp.concatenate([crow, jnp.zeros((LANES - N_HEADS, sub), F32)], axis=0)
        ccol_ref[0, rs, :] = cfull.T[:, :N_HEADS]
    carry_ref[...] = carry

    if prompt:
        @pl.when(j == pl.num_programs(1) - 1)
        def _():
            utail_ref[0] = buf_ref[tm:tm + CONV_HALO, :]


def _inproj(x, g, w, bf, tri, conv, tm, d_att, d_conv):
    b, s, d = x.shape
    prompt = conv is not None
    tok = lambda width: pl.BlockSpec((1, tm, width), lambda i, j: (i, j, 0))
    tr = lambda rows: pl.BlockSpec((1, rows, tm), lambda i, j: (i, 0, j))
    shp = lambda width, dt: jax.ShapeDtypeStruct((b, s, width), dt)
    sht = lambda rows, dt: jax.ShapeDtypeStruct((b, rows, s), dt)
    in_specs = [tok(d), _const_spec((1, d))] + [_const_spec(wk.shape, True) for wk in w] + [
        _const_spec((1, LANES)), _const_spec((LANES, LANES))]
    scratch = [pltpu.VMEM((N_HEADS, LANES), F32)]
    if prompt:
        in_specs += [_const_spec(conv[0].shape)] + [_const_spec((1, d_conv))] * 3
        out_specs = [tok(d_att), tr(d_att), tr(d_att), tr(d_att), tok(d_att), tr(N_HEADS), tok(d_conv),
                     pl.BlockSpec((1, CONV_HALO, d_conv), lambda i, j: (i, 0, 0))]
        out_shape = [shp(d_att, BF16), sht(d_att, F32), sht(d_att, F32), sht(d_att, BF16), shp(d_att, BF16),
                     sht(N_HEADS, F32), shp(d_conv, BF16), jax.ShapeDtypeStruct((b, CONV_HALO, d_conv), F32)]
        scratch += _conv_scratch(tm, d_conv)
    else:
        out_specs = [tok(d_att), tok(d_att), tok(d_att), tr(N_HEADS), tok(d_conv)]
        out_shape = [shp(d_att, BF16), shp(d_att, F32), shp(d_att, F32), sht(N_HEADS, F32), shp(d_conv, F32)]
    return pl.pallas_call(
        functools.partial(_inproj_kernel, d_att=d_att, d_conv=d_conv, prompt=prompt, sub=min(tm, 256)),
        grid=(b, s // tm),
        in_specs=in_specs,
        out_specs=out_specs + [tok(N_HEADS), tr(N_HEADS)],
        out_shape=out_shape + [shp(N_HEADS, F32), sht(N_HEADS, F32)],
        scratch_shapes=scratch,
        compiler_params=_params(2),
        name="inproj",
    )(x, g, *w, bf, tri, *(conv or ()))


def _attn_decode_kernel(pt_ref, q_ref, kt_ref, v_ref, ccol_ref, crow_ref,
                        qs_ref, kn_ref, vn_ref, cqcol_ref, cqrow_ref, us_ref, kc_ref, vc_ref, lc_ref,
                        o_ref, os_ref, kbuf, vbuf, lbuf, sem, *, tq, n_group):
    s_len = q_ref.shape[1]
    pair = pl.program_id(1)
    t = pl.program_id(0) * pl.num_programs(1) + pair
    n_samples = pl.num_programs(0) * pl.num_programs(1)
    n_pages = pt_ref.shape[1]
    n_sub = n_pages // n_group
    assert n_sub == 2
    t_new = qs_ref.shape[1]
    d_att = qs_ref.shape[2]
    page = kbuf.shape[-1]
    n_rows = t_new * N_HEADS
    nt_dims = (((1,), (1,)), ((), ()))

    def page_copies(sample, sub, real_pages=True):
        out = []
        for g in range(n_group):
            p = pt_ref[sample, n_pages - 1 - (sub * n_group + g)] if real_pages else 0
            out.append(pltpu.make_async_copy(kc_ref.at[p], kbuf.at[sub, g], sem.at[sub, 0]))
            out.append(pltpu.make_async_copy(vc_ref.at[p], vbuf.at[sub, g], sem.at[sub, 1]))
            out.append(pltpu.make_async_copy(lc_ref.at[p], lbuf.at[sub, g], sem.at[sub, 2]))
        return out

    @pl.when(t == 0)
    def _():
        for c in page_copies(t, 0):
            c.start()

    for c in page_copies(t, 1):
        c.start()

    cqcol = cqcol_ref[0]
    st = {}

    def decode_half(sub):
        if sub == 0:
            q = qs_ref[0].astype(F32)
            hsub = lax.broadcasted_iota(jnp.int32, (N_HEADS, d_att), 0)
            hlane = lax.broadcasted_iota(jnp.int32, (N_HEADS, d_att), 1)
            own = (hlane // HEAD_DIM) == hsub
            rows = [jnp.where(own, jnp.broadcast_to(q[i:i + 1, :], (N_HEADS, d_att)), 0.0) for i in range(t_new)]
            st["qbd"] = jnp.concatenate(rows, axis=0).astype(BF16)
            s = lax.dot_general(st["qbd"], kn_ref[0].astype(BF16), nt_dims, preferred_element_type=F32)
            cqk = jnp.concatenate([cqrow_ref[0]] * t_new, axis=0)
            r = lax.broadcasted_iota(jnp.int32, s.shape, 0)
            c = lax.broadcasted_iota(jnp.int32, s.shape, 1)
            s = jnp.where(c <= r // N_HEADS, s + (cqcol - cqk), NEG_INF)
            st["m"] = jnp.max(s, axis=-1, keepdims=True)
            pr = jnp.exp2(s - st["m"])
            st["l"] = jnp.sum(pr, axis=-1, keepdims=True)
            st["acc"] = jnp.dot(pr.astype(BF16), vn_ref[0].astype(BF16), preferred_element_type=F32)
            st["carry"] = jnp.zeros((N_HEADS, page), F32)

        for c in page_copies(t, sub, real_pages=False):
            c.wait()

        qbd, carry = st["qbd"], st["carry"]
        us = us_ref[...]
        s_parts = []
        r_all = _dot3(lbuf[sub].reshape(n_group * N_HEADS, page), us)
        for g in range(n_group):
            r = r_all[g * N_HEADS:(g + 1) * N_HEADS]
            suffix = (r[:, :page] + carry) * LOG2E
            carry = carry + r[:, page:]
            bias = jnp.concatenate([suffix] * t_new, axis=0) + cqcol
            kt = kbuf[sub, g].reshape(d_att, page).astype(BF16)
            s_parts.append(jnp.dot(qbd, kt, preferred_element_type=F32) + bias)
        s = jnp.concatenate(s_parts, axis=1)
        m_new = jnp.maximum(st["m"], jnp.max(s, axis=-1, keepdims=True))
        alpha = jnp.exp2(st["m"] - m_new)
        pr = jnp.exp2(s - m_new)
        prb = pr.astype(BF16)
        acc = alpha * st["acc"]
        for g in range(n_group):
            vt = vbuf[sub, g].reshape(d_att, page).astype(BF16)
            acc = acc + lax.dot_general(prb[:, g * page:(g + 1) * page], vt, nt_dims, preferred_element_type=F32)
        st.update(m=m_new, l=alpha * st["l"] + jnp.sum(pr, axis=-1, keepdims=True), acc=acc, carry=carry)

        if sub == 0:
            @pl.when(t + 1 < n_samples)
            def _():
                for c in page_copies(t + 1, 0):
                    c.start()

        if sub == n_sub - 1:
            o = acc * (1.0 / st["l"])
            hsub = lax.broadcasted_iota(jnp.int32, (N_HEADS, d_att), 0)
            hlane = lax.broadcasted_iota(jnp.int32, (N_HEADS, d_att), 1)
            own = (hlane // HEAD_DIM) == hsub
            rows = [jnp.sum(jnp.where(own, o[i * N_HEADS:(i + 1) * N_HEADS], 0.0), axis=0, keepdims=True)
                    for i in range(t_new)]
            os_ref[0] = jnp.concatenate(rows, axis=0).astype(os_ref.dtype)

    row = lax.broadcasted_iota(jnp.int32, (tq, tq), 0)
    col = lax.broadcasted_iota(jnp.int32, (tq, tq), 1)
    causal = col <= row
    low = lax.broadcasted_iota(jnp.int32, (tq, LANES), 1) < HEAD_DIM
    ccol = ccol_ref[0]
    head_lane = lax.broadcasted_iota(jnp.int32, ccol.shape, 1)
    cq_all = []
    sub16 = lax.broadcasted_iota(jnp.int32, (2 * LANES // N_HEADS, s_len), 0)
    extra = jnp.zeros(sub16.shape, F32)
    for hh in range(2):
        head = 2 * pair + hh
        cq_all.append(jnp.sum(jnp.where(head_lane == head, ccol, 0.0), axis=-1, keepdims=True))
        for n, piece in enumerate(_split3(-crow_ref[0, pl.ds(head, 1), :])):
            extra = jnp.where(sub16 == 3 * hh + n, piece, extra)
    kt_bias = jnp.concatenate([kt_ref[0], extra.astype(BF16),
                               jnp.zeros((LANES - extra.shape[0], s_len), BF16)], axis=0)
    lane = lax.broadcasted_iota(jnp.int32, (tq, LANES), 1)

    def scores(i, hh):
        k0 = i * tq
        q2 = q_ref[0, k0:k0 + tq, :]
        zero = jnp.zeros_like(q2)
        qh = jnp.where(low, q2, zero) if hh == 0 else jnp.where(low, zero, q2)
        ones = jnp.where((lane >= 3 * hh) & (lane < 3 * hh + 3), 1.0, 0.0).astype(BF16)
        qh = jnp.concatenate([qh, ones], axis=1)
        s_diag = jnp.dot(qh, kt_bias[:, k0:k0 + tq], preferred_element_type=F32)
        s_off = jnp.dot(qh, kt_bias[:, :k0], preferred_element_type=F32) if i > 0 else None
        return s_diag, s_off

    v_ones = jnp.concatenate([v_ref[0], jnp.ones((s_len, LANES), BF16)], axis=1)

    def finish(i, hh, s_diag, s_off):
        k0 = i * tq
        cq = cq_all[hh][k0:k0 + tq]
        s_diag = jnp.where(causal, s_diag, NEG_INF)
        m = jnp.max(s_diag, axis=-1, keepdims=True)
        if i > 0:
            m = jnp.maximum(m, jnp.max(s_off, axis=-1, keepdims=True))
        shift = cq - (m + cq)
        pv = jnp.dot(jnp.exp2(s_diag + shift).astype(BF16), v_ones[k0:k0 + tq], preferred_element_type=F32)
        if i > 0:
            pv = pv + jnp.dot(jnp.exp2(s_off + shift).astype(BF16), v_ones[:k0], preferred_element_type=F32)
        return pv[:, :LANES] * (1.0 / pv[:, LANES:LANES + 1])

    blocks = [(i, hh) for i in range(s_len // tq) for hh in range(2)]
    work = [i + 1 for i, _ in blocks]
    first_half_at = next(n for n in range(len(blocks)) if 2 * sum(work[:n + 1]) >= sum(work))
    nxt = scores(*blocks[0])
    halves = []
    for n, (i, hh) in enumerate(blocks):
        cur = nxt
        if n + 1 < len(blocks):
            nxt = scores(*blocks[n + 1])
        halves.append(finish(i, hh, *cur))
        if hh == 1:
            o_ref[0, i * tq:(i + 1) * tq, :] = jnp.where(low, halves[0], halves[1]).astype(BF16)
            halves = []
        if n == first_half_at:
            decode_half(0)
    decode_half(1)


def _attn_decode(q, ktb, vb, ccol, crow, tq,
                 page_table, qs, kn, vn, cqcol, cqrow, us, cache_k, cache_v, cache_lf, n_group):
    b, s, d_att = q.shape
    bd, t_new, _ = qs.shape
    n_pairs = d_att // LANES
    assert bd == b * n_pairs
    page = cache_k.shape[3]
    n_rows = t_new * N_HEADS
    cols = pl.BlockSpec((1, s, LANES), lambda i, j, pt: (i, 0, j))
    per_s = lambda shape: pl.BlockSpec((1,) + shape, lambda i, j, pt: (i * n_pairs + j,) + (0,) * len(shape))
    hbm = pl.BlockSpec(memory_space=pl.ANY)
    grid_spec = pltpu.PrefetchScalarGridSpec(
        num_scalar_prefetch=1,
        grid=(b, n_pairs),
        in_specs=[cols, pl.BlockSpec((1, LANES, s), lambda i, j, pt: (i, j, 0)), cols,
                  pl.BlockSpec((1, s, N_HEADS), lambda i, j, pt: (i, 0, 0)),
                  pl.BlockSpec((1, N_HEADS, s), lambda i, j, pt: (i, 0, 0)),
                  per_s((t_new, d_att)), per_s(kn.shape[1:]), per_s(vn.shape[1:]), per_s((n_rows, 1)),
                  per_s(cqrow.shape[1:]), pl.BlockSpec(us.shape, lambda i, j, pt: (0, 0)), hbm, hbm, hbm],
        out_specs=[cols, per_s((t_new, d_att))],
        scratch_shapes=[pltpu.VMEM((2, n_group, N_HEADS, HEAD_DIM, page), F32),
                        pltpu.VMEM((2, n_group, N_HEADS, HEAD_DIM, page), F32),
                        pltpu.VMEM((2, n_group, N_HEADS, page), F32),
                        pltpu.SemaphoreType.DMA((2, 3))],
    )
    return pl.pallas_call(
        functools.partial(_attn_decode_kernel, tq=tq, n_group=n_group),
        grid_spec=grid_spec,
        out_shape=[jax.ShapeDtypeStruct((b, s, d_att), BF16), jax.ShapeDtypeStruct((bd, t_new, d_att), BF16)],
        compiler_params=_params(2),
        name="attn_decode",
    )(page_table, q, ktb, vb, ccol, crow, qs, kn, vn, cqcol, cqrow, us, cache_k, cache_v, cache_lf)


def _conv_kernel(u_ref, state_ref, w_ref, b_ref, lg_ref, lb_ref, o_ref, buf_ref, rot_ref, y_ref):
    tm = u_ref.shape[1]
    for n in range(u_ref.shape[0]):
        buf_ref[0:CONV_HALO, :] = state_ref[n]
        buf_ref[CONV_HALO:, :] = u_ref[n]
        _causal_conv(buf_ref, rot_ref, w_ref, b_ref, y_ref, t0=0, tm=tm, rows=tm)
        o_ref[n] = _ln_swish(y_ref[...], lg_ref[...], lb_ref[...])


def _conv(u, state, w, b, lg, lb, per_step):
    bsz, tm, d_conv = u.shape
    assert state.shape == (bsz, CONV_HALO, d_conv)
    tile = pl.BlockSpec((per_step, tm, d_conv), lambda i: (i, 0, 0))
    vec = _const_spec((1, d_conv))
    return pl.pallas_call(
        _conv_kernel,
        grid=(bsz // per_step,),
        in_specs=[tile, pl.BlockSpec((per_step, CONV_HALO, d_conv), lambda i: (i, 0, 0)), _const_spec(w.shape),
                  vec, vec, vec],
        out_specs=tile,
        out_shape=jax.ShapeDtypeStruct((bsz, tm, d_conv), F32),
        scratch_shapes=_conv_scratch(tm, d_conv),
        compiler_params=_params(1),
        name="conv",
    )(u, state, w, b, lg, lb)


def _pool_kernel(x_ref, state_ref, g2_ref, g3_ref, w_ref, sc_ref, o_ref, h_ref, buf_ref, ta_ref, tb_ref, *, pos0):
    tm = x_ref.shape[1]
    pos = pos0 + lax.broadcasted_iota(jnp.int32, (tm, 1), 0)
    for bi in range(x_ref.shape[0]):
        x = x_ref[bi]
        h = _rms(x, g2_ref[...])
        h_ref[bi] = h
        mix = _pool_mix(h, state_ref[bi], pos, w_ref, sc_ref, buf_ref, ta_ref, tb_ref)
        o_ref[bi] = x + _rms(mix, g3_ref[...])


def _pool(x, state, g2, g3, w, scale, pos0, per_step):
    b, tm, d = x.shape
    assert state.shape == (b, POOL_HALO, d)
    tile = pl.BlockSpec((per_step, tm, d), lambda i: (i, 0, 0))
    vec = _const_spec((1, d))
    buf = pltpu.VMEM((tm + POOL_HALO, d), F32)
    return pl.pallas_call(
        functools.partial(_pool_kernel, pos0=pos0),
        grid=(b // per_step,),
        in_specs=[tile, pl.BlockSpec((per_step, POOL_HALO, d), lambda i: (i, 0, 0)), vec, vec,
                  _const_spec(w.shape), vec],
        out_specs=[tile, tile],
        out_shape=[jax.ShapeDtypeStruct((b, tm, d), F32), jax.ShapeDtypeStruct((b, tm, d), F32)],
        scratch_shapes=[buf, buf, buf],
        compiler_params=_params(1),
        name="pool",
    )(x, state, g2, g3, w, scale)


def kernel(x_prompt, x_sample, cache_k, cache_v, cache_logf, state_conv, state_pool, page_table,
           norm_g, ffn_w_gate, ffn_w_up, ffn_w_down, mix_w_in, fgate_b, conv_dw_w, conv_dw_b,
           conv_ln_g, conv_ln_b, mix_w_out, pool_w, pool_scale):
    b, s, d = x_prompt.shape
    bd, t_new, _ = x_sample.shape
    depth = norm_g.shape[0]
    d_att = N_HEADS * HEAD_DIM
    d_conv = d - d_att
    page = cache_k.shape[2]
    past_len = page_table.shape[1] * page
    tm_prompt = 1024
    m_sample = bd * t_new
    t_pad = 8

    xp = x_prompt.reshape(b * s, d)
    xs = x_sample.reshape(m_sample, d)
    vec = lambda a: a.reshape(1, -1).astype(F32)

    stacks = (ffn_w_gate.astype(F32), ffn_w_up.astype(F32), ffn_w_down.astype(F32))
    halves = [(layer, half) for layer in range(depth) for half in range(2)]
    ffn_weights = [tuple(w[0, 0].astype(BF16) for w in stacks)]

    def ffn_both(xp, xs, layer, half, mix=None, pool=None):
        gp, gq = vec(norm_g[layer, 4 * half]), vec(norm_g[layer, 4 * half + 1])
        n = halves.index((layer, half))
        convert = (stacks, halves[n + 1]) if n + 1 < len(halves) else None
        xp, xs, *rest = _ffn(xp, xs, gp, gq, *ffn_weights[n], 1024, mix, pool, convert)
        tail = rest.pop(0) if pool is not None else None
        ffn_weights.append(tuple(rest))
        return xp, xs, tail

    r = jnp.arange(LANES)
    tri_incl = (r[:, None] <= r[None, :]).astype(F32)
    tri_group = tri_incl * (r[:, None] // t_new == r[None, :] // t_new).astype(F32)
    rp = jnp.arange(page)
    suffix_total = jnp.concatenate([(rp[:, None] > rp[None, :]).astype(F32), jnp.ones((page, page), F32)], axis=1)

    outs_p = {k: [] for k in ("k", "v", "lf", "conv", "pool")}
    outs_s = {k: [] for k in ("k", "v", "lf", "conv", "pool")}
    for layer in range(depth):
        g = norm_g[layer]
        xp, xs, _ = ffn_both(xp, xs, layer, 0)
        mix = pool = None
        if layer % 2 == 0:
            e = layer // 2
            w_in = mix_w_in[e]
            o = 3 * d_att + N_HEADS
            w_cat = (w_in[:, :3 * d_att].astype(BF16), w_in[:, o:].astype(BF16),
                     jnp.pad(w_in[:, 3 * d_att:o], ((0, 0), (0, LANES - N_HEADS))).astype(BF16))
            bf = jnp.concatenate([fgate_b[e].astype(F32), jnp.zeros((LANES - N_HEADS,), F32)]).reshape(1, LANES)
            wa = mix_w_out[e, :d_att].astype(BF16)
            wc = mix_w_out[e, d_att:].astype(BF16)
            cw, cb = conv_dw_w[e].astype(F32), vec(conv_dw_b[e])
            lg, lb = vec(conv_ln_g[e]), vec(conv_ln_b[e])
            g2, g3 = vec(g[2]), vec(g[3])

            q, kt, vt, ktb, vb, lft, cv, u_tail, ccol, crow = _inproj(
                xp.reshape(b, s, d), g2, w_cat, bf, tri_incl, (cw, cb, lg, lb), tm_prompt, d_att, d_conv)
            prompt_att = (q, ktb, vb, ccol, crow)
            cv_p = cv.reshape(b * s, d_conv)
            outs_p["k"].append(kt.reshape(b, N_HEADS, HEAD_DIM, s).transpose(0, 3, 1, 2))
            outs_p["v"].append(vt.reshape(b, N_HEADS, HEAD_DIM, s).transpose(0, 3, 1, 2))
            outs_p["lf"].append(lft.transpose(0, 2, 1))
            outs_p["conv"].append(u_tail[:, CONV_HALO - (CONV_W - 1):])

            q, k, v, lft, u, ccol, crow = _inproj(
                xs.reshape(1, m_sample, d), g2, w_cat, bf, tri_group, None, m_sample, d_att, d_conv)
            lf = lft.reshape(N_HEADS, m_sample).T
            pad_rows = lambda a: jnp.pad(a.reshape(bd, t_new, d_att), ((0, 0), (0, 16 - t_new), (0, 0)))
            cqcol = ccol.reshape(bd, t_new * N_HEADS, 1)
            cqrow = jnp.pad(crow.reshape(N_HEADS, bd, t_new).transpose(1, 0, 2), ((0, 0), (0, 0), (0, 16 - t_new)))
            att_p, att = _attn_decode(
                *prompt_att, 256, page_table, q.reshape(bd, t_new, d_att), pad_rows(k), pad_rows(v), cqcol, cqrow,
                suffix_total, cache_k[e].transpose(0, 2, 3, 1), cache_v[e].transpose(0, 2, 3, 1),
                cache_logf[e].astype(F32).transpose(0, 2, 1), 32)
            att_p = att_p.reshape(b * s, d_att)
            st = state_conv[e].astype(F32)
            cv = _conv(jnp.pad(u.reshape(bd, t_new, d_conv), ((0, 0), (0, t_pad - t_new), (0, 0))),
                       jnp.pad(st, ((0, 0), (CONV_HALO - (CONV_W - 1), 0), (0, 0))), cw, cb, lg, lb, 8)
            mix = (att_p, cv_p, att.reshape(m_sample, d_att), cv[:, :t_new].reshape(m_sample, d_conv), wa, wc, g3)
            outs_s["k"].append(k.reshape(bd, t_new, N_HEADS, HEAD_DIM))
            outs_s["v"].append(v.reshape(bd, t_new, N_HEADS, HEAD_DIM))
            outs_s["lf"].append(lf.reshape(bd, t_new, N_HEADS))
            outs_s["conv"].append(jnp.concatenate([st, u.reshape(bd, t_new, d_conv)], axis=1)[:, t_new:])
        else:
            o = layer // 2
            g2, g3 = vec(g[2]), vec(g[3])
            pw = pool_w[o].astype(BF16)
            sc = vec(pool_scale[o])
            n_keep = POOL_WINDOWS[-1] - 1
            pool = (s, g2, g3, pw, sc)
            st = state_pool[o].astype(F32)
            xs3, hs = _pool(jnp.pad(xs.reshape(bd, t_new, d), ((0, 0), (0, t_pad - t_new), (0, 0))),
                            jnp.pad(st, ((0, 0), (POOL_HALO - n_keep, 0), (0, 0))), g2, g3, pw, sc, past_len, 8)
            xs = xs3[:, :t_new].reshape(m_sample, d)
            outs_s["pool"].append(jnp.concatenate([st, hs[:, :t_new]], axis=1)[:, t_new:])
        xp, xs, tail = ffn_both(xp, xs, layer, 1, mix, pool)
        if pool is not None:
            outs_p["pool"].append(tail[:, POOL_TAIL - (POOL_WINDOWS[-1] - 1):])

    stack = lambda l: jnp.stack(l)
    return (xp.reshape(b, s, d), xs.reshape(bd, t_new, d),
            stack(outs_p["k"]), stack(outs_p["v"]), stack(outs_p["lf"]), stack(outs_p["conv"]),
            stack(outs_p["pool"]),
            stack(outs_s["k"]), stack(outs_s["v"]), stack(outs_s["lf"]), stack(outs_s["conv"]),
            stack(outs_s["pool"]))
```
